```python
import jax
import jax.numpy as jnp
from jax import lax
import numpy as np

D_MODEL = 1024
BATCH = 2
SEQ = 8192
DEPTH = 2

GRID_W = 64
CTX_LEN = 256
N_EVEN = (DEPTH + 1) // 2
N_ODD = DEPTH // 2
N_MOD = 9
D_FF = 2816
D_POOL = D_MODEL // 4
POOL_WINDOWS = (2, 4, 8, 16)
N_POOL_GROUPS = len(POOL_WINDOWS)
POOL_GROUP_DIM = D_POOL // N_POOL_GROUPS
D_HGRN = D_MODEL - D_POOL
HGRN_HEAD_DIM = 128
N_HGRN_HEADS = D_HGRN // HGRN_HEAD_DIM
HGRN_CHUNK = 64
D_IN_AB = D_POOL + 5 * D_HGRN
D_GMLP = D_MODEL
GMLP_CHUNK = 128
GMLP_GROUP_DIM = 128
N_GMLP_GROUPS = D_GMLP // GMLP_GROUP_DIM
EPS = 1e-6

kernel_name = 'hybrid_pool_hgrn2_gmlp_macaron_dit'


def rms_norm(x, g):
    xf = x.astype(jnp.float32)
    y = xf * lax.rsqrt(jnp.mean(xf * xf, axis=-1, keepdims=True) + EPS)
    return (y * g.astype(jnp.float32)).astype(x.dtype)


def layer_norm(x, g, b):
    xf = x.astype(jnp.float32)
    xc = xf - jnp.mean(xf, axis=-1, keepdims=True)
    y = xc * lax.rsqrt(jnp.mean(xc * xc, axis=-1, keepdims=True) + EPS)
    return (y * g.astype(jnp.float32) + b.astype(jnp.float32)).astype(x.dtype)


def adaln(x, g, mod, s):
    return rms_norm(x, g) * (1 + mod[:, :, 3 * s + 1]) + mod[:, :, 3 * s]


def gate_of(mod, s):
    return mod[:, :, 3 * s + 2]


def swiglu(h, w1, w3, w2):
    return (jax.nn.silu(h @ w1) * (h @ w3)) @ w2


def centred_mean(x, w):
    L = x.shape[-2]
    cs = jnp.cumsum(x.astype(jnp.float32), axis=-2)
    cs = jnp.concatenate([jnp.zeros_like(cs[..., :1, :]), cs], axis=-2)
    t = jnp.arange(L)
    lo = jnp.clip(t - w // 2, 0, L)
    hi = jnp.clip(t - w // 2 + w, 0, L)
    total = jnp.take(cs, hi, axis=-2) - jnp.take(cs, lo, axis=-2)
    count = (hi - lo).astype(jnp.float32)[:, None]
    return (total / count).astype(x.dtype)


def pool_mixer(xa, w_pool, scale, on_grid):
    B, L, _ = xa.shape
    if on_grid:
        rows = L // GRID_W
        xr = xa.reshape(B, rows, GRID_W, D_POOL)
    else:
        xr = xa
    groups = jnp.split(xr, N_POOL_GROUPS, axis=-1)
    y = jnp.stack([centred_mean(g, w) - g for g, w in zip(groups, POOL_WINDOWS)], axis=-2)
    y = jnp.einsum('...gc,gcd->...gd', y, w_pool)
    return y.reshape(B, L, D_POOL) * scale


def split_heads(a):
    B, L, _ = a.shape
    return a.reshape(B, L, N_HGRN_HEADS, HGRN_HEAD_DIM).transpose(0, 2, 1, 3).astype(jnp.float32)


def hgrn_prep(p, lb):
    q, v, f_fwd, f_bwd, g = jnp.split(p, 5, axis=-1)
    q = split_heads(jax.nn.silu(q)) * (HGRN_HEAD_DIM ** -0.5)
    v = split_heads(v)
    dirs = []
    for f, lbd in ((f_fwd, lb[0]), (f_bwd, lb[1])):
        lbh = lbd.astype(jnp.float32).reshape(1, N_HGRN_HEADS, 1, HGRN_HEAD_DIM)
        forget = lbh + (1 - lbh) * jax.nn.sigmoid(split_heads(f))
        dirs.append((1 - forget, jnp.log(forget)))
    return q, v, dirs, g


def hgrn_chunk_scan(q, k, v, logf, s0):
    B, H, L, _ = q.shape
    n = L // HGRN_CHUNK

    def to_chunks(a):
        return jnp.moveaxis(a.reshape(B, H, n, HGRN_CHUNK, a.shape[-1]), 2, 0)

    mask = jnp.tril(jnp.ones((HGRN_CHUNK, HGRN_CHUNK), dtype=bool))[:, :, None]

    def step(S, inp):
        qc, kc, vc, lf = inp
        b = jnp.cumsum(lf, axis=2)
        o_inter = jnp.einsum('bhjk,bhkv->bhjv', qc * jnp.exp(b), S)
        diff = b[:, :, :, None, :] - b[:, :, None, :, :]
        decay = jnp.exp(jnp.where(mask, diff, -jnp.inf))
        a = jnp.einsum('bhjk,bhjik,bhik->bhji', qc, decay, kc)
        o = o_inter + jnp.einsum('bhji,bhiv->bhjv', a, vc)
        b_last = b[:, :, -1:, :]
        S = jnp.exp(b[:, :, -1, :])[..., None] * S + jnp.einsum('bhik,bhiv->bhkv', kc * jnp.exp(b_last - b), vc)
        return S, o

    S, o = lax.scan(step, s0, (to_chunks(q), to_chunks(k), to_chunks(v), to_chunks(logf)))
    o = jnp.moveaxis(o, 0, 2).reshape(B, H, L, v.shape[-1])
    return o, S


def hgrn_readout(o, gate, g_norm, dtype):
    B, H, L, dv = o.shape
    o = o.transpose(0, 2, 1, 3)
    o = o * lax.rsqrt(jnp.mean(o * o, axis=-1, keepdims=True) + EPS)
    o = o.reshape(B, L, D_HGRN) * g_norm.astype(jnp.float32)
    return (o * jax.nn.silu(gate.astype(jnp.float32))).astype(dtype)


def hgrn_mixer(p_lat, p_ctx, lb, g_norm):
    q_l, v_l, dirs_l, gate_l = hgrn_prep(p_lat, lb)
    q_c, v_c, dirs_c, gate_c = hgrn_prep(p_ctx, lb)
    B = p_lat.shape[0]
    s0 = jnp.zeros((B, N_HGRN_HEADS, HGRN_HEAD_DIM, HGRN_HEAD_DIM), jnp.float32)
    o_lat = jnp.zeros_like(v_l)
    o_ctx = jnp.zeros_like(v_c)
    for d, reverse in enumerate((False, True)):
        def flip(a, reverse=reverse):
            return jnp.flip(a, axis=2) if reverse else a
        k_c, lf_c = dirs_c[d]
        k_l, lf_l = dirs_l[d]
        oc, s_ctx = hgrn_chunk_scan(flip(q_c), flip(k_c), flip(v_c), flip(lf_c), s0)
        ol, _ = hgrn_chunk_scan(flip(q_l), flip(k_l), flip(v_l), flip(lf_l), s_ctx)
        o_ctx = o_ctx + flip(oc)
        o_lat = o_lat + flip(ol)
    return (hgrn_readout(o_lat, gate_l, g_norm, p_lat.dtype),
            hgrn_readout(o_ctx, gate_c, g_norm, p_ctx.dtype))


def ab_mixer(h_lat, h_ctx, w_in, w_pool, pool_scale, lb, g_norm, w_out):
    p_lat = h_lat @ w_in
    p_ctx = h_ctx @ w_in
    a_lat = pool_mixer(p_lat[..., :D_POOL], w_pool, pool_scale, True)
    a_ctx = pool_mixer(p_ctx[..., :D_POOL], w_pool, pool_scale, False)
    b_lat, b_ctx = hgrn_mixer(p_lat[..., D_POOL:], p_ctx[..., D_POOL:], lb, g_norm)
    out_lat = jnp.concatenate([a_lat, b_lat], axis=-1) @ w_out
    out_ctx = jnp.concatenate([a_ctx, b_ctx], axis=-1) @ w_out
    return out_lat, out_ctx


def gmlp_mixer(h, w_in, ln_g, ln_b, w_s, b_s, w_out):
    B, L, _ = h.shape
    z = jax.nn.gelu(h @ w_in)
    u, v = jnp.split(z, 2, axis=-1)
    v = layer_norm(v, ln_g, ln_b)
    n = L // GMLP_CHUNK
    vr = v.reshape(B, n, GMLP_CHUNK, N_GMLP_GROUPS, GMLP_GROUP_DIM)
    sv = jnp.einsum('gpq,bnqgc->bnpgc', w_s, vr) + b_s.T[None, None, :, :, None]
    return (u * sv.reshape(B, L, D_GMLP)) @ w_out


def setup_inputs(seed: int = 0) -> dict:
    key = jax.random.key(seed)
    ks = jax.random.split(key, 23)
    D = D_MODEL

    def nrm(k, shape, scale):
        return jax.random.normal(k, shape, jnp.float32) * scale

    return {
        'x': nrm(ks[0], (BATCH, SEQ, D), 1.0),
        'c': nrm(ks[1], (BATCH, D), 1.0),
        'ctx': nrm(ks[2], (BATCH, CTX_LEN, D), 1.0),
        'c_ctx': nrm(ks[3], (D,), 1.0),
        'mod_w': nrm(ks[4], (DEPTH, D, N_MOD * D), 0.5 * D ** -0.5),
        'mod_b': nrm(ks[5], (DEPTH, N_MOD * D), 0.02),
        'norm_g': 1.0 + nrm(ks[6], (DEPTH, 3, D), 0.05),
        'ffn_w1': nrm(ks[7], (DEPTH, 2, D, D_FF), D ** -0.5),
        'ffn_w3': nrm(ks[8], (DEPTH, 2, D, D_FF), D ** -0.5),
        'ffn_w2': nrm(ks[9], (DEPTH, 2, D_FF, D), D_FF ** -0.5),
        'ab_w_in': nrm(ks[10], (N_EVEN, D, D_IN_AB), D ** -0.5),
        'pool_w': nrm(ks[11], (N_EVEN, N_POOL_GROUPS, POOL_GROUP_DIM, POOL_GROUP_DIM), POOL_GROUP_DIM ** -0.5),
        'pool_scale': 1.0 + nrm(ks[12], (N_EVEN, D_POOL), 0.05),
        'hgrn_lb': nrm(ks[13], (N_EVEN + 1, 2, D_HGRN), 1.0),
        'hgrn_norm_g': 1.0 + nrm(ks[14], (N_EVEN, D_HGRN), 0.05),
        'ab_w_out': nrm(ks[15], (N_EVEN, D_POOL + D_HGRN, D), (D_POOL + D_HGRN) ** -0.5),
        'gmlp_w_in': nrm(ks[16], (N_ODD, D, 2 * D_GMLP), D ** -0.5),
        'gmlp_ln_g': 1.0 + nrm(ks[17], (N_ODD, D_GMLP), 0.05),
        'gmlp_ln_b': nrm(ks[18], (N_ODD, D_GMLP), 0.02),
        'gmlp_w_s': nrm(ks[19], (N_ODD, N_GMLP_GROUPS, GMLP_CHUNK, GMLP_CHUNK), GMLP_CHUNK ** -0.5),
        'gmlp_b_s': 1.0 + nrm(ks[20], (N_ODD, N_GMLP_GROUPS, GMLP_CHUNK), 0.05),
        'gmlp_w_out': nrm(ks[21], (N_ODD, D_GMLP, D), D_GMLP ** -0.5),
        'final_g': 1.0 + nrm(ks[22], (D,), 0.05),
    }


def reference(x, c, ctx, c_ctx, mod_w, mod_b, norm_g, ffn_w1, ffn_w3, ffn_w2,
              ab_w_in, pool_w, pool_scale, hgrn_lb, hgrn_norm_g, ab_w_out,
              gmlp_w_in, gmlp_ln_g, gmlp_ln_b, gmlp_w_s, gmlp_b_s, gmlp_w_out, final_g):
    B = x.shape[0]
    lb_all = jnp.cumsum(jax.nn.softmax(hgrn_lb.astype(jnp.float32), axis=0), axis=0)
    xl, xc = x, ctx
    for i in range(DEPTH):
        ctx_live = any(j % 2 == 0 for j in range(i, DEPTH))
        mod_l = (jax.nn.silu(c) @ mod_w[i] + mod_b[i]).reshape(B, 1, N_MOD, D_MODEL)
        mod_c = (jax.nn.silu(c_ctx) @ mod_w[i] + mod_b[i]).reshape(1, 1, N_MOD, D_MODEL)

        xl = xl + 0.5 * gate_of(mod_l, 0) * swiglu(adaln(xl, norm_g[i, 0], mod_l, 0), ffn_w1[i, 0], ffn_w3[i, 0], ffn_w2[i, 0])
        if ctx_live:
            xc = xc + 0.5 * gate_of(mod_c, 0) * swiglu(adaln(xc, norm_g[i, 0], mod_c, 0), ffn_w1[i, 0], ffn_w3[i, 0], ffn_w2[i, 0])

        h_l = adaln(xl, norm_g[i, 1], mod_l, 1)
        if i % 2 == 0:
            e = i // 2
            h_c = adaln(xc, norm_g[i, 1], mod_c, 1)
            o_l, o_c = ab_mixer(h_l, h_c, ab_w_in[e], pool_w[e], pool_scale[e], lb_all[e], hgrn_norm_g[e], ab_w_out[e])
            xc = xc + gate_of(mod_c, 1) * o_c
        else:
            o = i // 2
            o_l = gmlp_mixer(h_l, gmlp_w_in[o], gmlp_ln_g[o], gmlp_ln_b[o], gmlp_w_s[o], gmlp_b_s[o], gmlp_w_out[o])
            if ctx_live:
                h_c = adaln(xc, norm_g[i, 1], mod_c, 1)
                o_c = gmlp_mixer(h_c, gmlp_w_in[o], gmlp_ln_g[o], gmlp_ln_b[o], gmlp_w_s[o], gmlp_b_s[o], gmlp_w_out[o])
                xc = xc + gate_of(mod_c, 1) * o_c
        xl = xl + gate_of(mod_l, 1) * o_l

        xl = xl + 0.5 * gate_of(mod_l, 2) * swiglu(adaln(xl, norm_g[i, 2], mod_l, 2), ffn_w1[i, 1], ffn_w3[i, 1], ffn_w2[i, 1])
        if ctx_live:
            xc = xc + 0.5 * gate_of(mod_c, 2) * swiglu(adaln(xc, norm_g[i, 2], mod_c, 2), ffn_w1[i, 1], ffn_w3[i, 1], ffn_w2[i, 1])
    return rms_norm(xl, final_g)
```

```python
import functools

import numpy as np
import jax
import jax.numpy as jnp
from jax import lax
from jax.experimental import pallas as pl
from jax.experimental.pallas import tpu as pltpu

F32 = jnp.float32
BF16 = jnp.bfloat16

EPS = 1e-6
N_MOD = 9
GRID_W = 64
POOL_WINDOWS = (2, 4, 8, 16)
HGRN_HEAD_DIM = 128
GMLP_CHUNK = 128
GMLP_GROUP_DIM = 128

MOD_ROWS = 8
VMEM_LIMIT_BYTES = 56 * 1024 * 1024


def _cparams(n_grid_dims):
    return pltpu.CompilerParams(
        dimension_semantics=("arbitrary",) * n_grid_dims,
        vmem_limit_bytes=VMEM_LIMIT_BYTES)


def _sigmoid(x):
    return 1.0 / (1.0 + jnp.exp(-x))


def _silu(x):
    return x * _sigmoid(x)


def _rms(x, g):
    return x * lax.rsqrt(jnp.mean(x * x, axis=-1, keepdims=True) + EPS) * g


def _adaln(x, g, shift, scale):
    return _rms(x, g) * (1.0 + scale) + shift


def _dot(a, b):
    return jnp.dot(a, b, preferred_element_type=F32)


def _dot_nt(a, b):
    return lax.dot_general(a, b, (((1,), (1,)), ((), ())), preferred_element_type=F32)


def _dot_tn(a, b):
    return lax.dot_general(a, b, (((0,), (0,)), ((), ())), preferred_element_type=F32)


def _const_spec(shape):
    nd = len(shape)
    return pl.BlockSpec(shape, lambda *_: (0,) * nd)


def _mod_kernel(c_ref, w_ref, b_ref, o_ref):
    a = _silu(c_ref[...])
    o_ref[...] = jnp.dot(a, w_ref[...], preferred_element_type=F32,
                         precision=lax.Precision.HIGHEST) + b_ref[...]


def _modulation(c_rows, mod_w, mod_b, tn=1152):
    depth, d, n = mod_w.shape
    return pl.pallas_call(
        _mod_kernel,
        out_shape=jax.ShapeDtypeStruct((depth, MOD_ROWS, n), F32),
        grid=(depth, n // tn),
        in_specs=[
            pl.BlockSpec((MOD_ROWS, d), lambda l, j: (0, 0)),
            pl.BlockSpec((None, d, tn), lambda l, j: (l, 0, j)),
            pl.BlockSpec((None, 1, tn), lambda l, j: (l, 0, j)),
        ],
        out_specs=pl.BlockSpec((None, MOD_ROWS, tn), lambda l, j: (l, 0, j)),
        compiler_params=_cparams(2),
        name="modulation",
    )(c_rows, mod_w, mod_b.reshape(depth, 1, n))


def _mod_index(layer, col, row_of_batch, b, t):
    return (layer, row_of_batch(b), col, 0, 0)


def _mod_block_specs(d_model, layer, sub, row_of_batch):
    return [pl.BlockSpec((None, None, None, 1, d_model),
                         functools.partial(_mod_index, layer, 3 * sub + k, row_of_batch))
            for k in range(3)]


def _ffn_kernel(x_ref, shift_ref, scale_ref, gate_ref, g_ref, w1_ref, w3_ref, w2_ref, fg_ref,
                o_ref, a_ref, *, n_chunk, final_norm):
    x = x_ref[...]
    h = _adaln(x, g_ref[...], shift_ref[...], scale_ref[...]).astype(BF16)
    d_ff = w1_ref.shape[1]
    for c in range(d_ff // n_chunk):
        sl = slice(c * n_chunk, (c + 1) * n_chunk)
        u = _dot(h, w1_ref[:, sl])
        v = _dot(h, w3_ref[:, sl])
        a_ref[:, sl] = (_silu(u) * v).astype(BF16)
    y = _dot(a_ref[...], w2_ref[...])
    out = x + (0.5 * gate_ref[...]) * y
    if final_norm:
        out = _rms(out, fg_ref[...])
    o_ref[...] = out


def _ffn(x, mod5, layer, sub, row_of_batch, norm_g, w1, w3, w2, final_g, *, final_norm=False, tm=512,
         n_chunk=256):
    bsz, length, d = x.shape
    d_ff = w1.shape[1]
    tm = min(tm, length)
    kern = functools.partial(_ffn_kernel, n_chunk=n_chunk, final_norm=final_norm)
    return pl.pallas_call(
        kern,
        out_shape=jax.ShapeDtypeStruct(x.shape, F32),
        grid=(bsz, length // tm),
        in_specs=[pl.BlockSpec((None, tm, d), lambda b, t: (b, t, 0))]
        + _mod_block_specs(d, layer, sub, row_of_batch)
        + [_const_spec((1, d)), _const_spec((d, d_ff)), _const_spec((d, d_ff)), _const_spec((d_ff, d)),
           _const_spec((1, d))],
        out_specs=pl.BlockSpec((None, tm, d), lambda b, t: (b, t, 0)),
        scratch_shapes=[pltpu.VMEM((tm, d_ff), BF16)],
        compiler_params=_cparams(2),
        name="swiglu_ffn",
    )(x, mod5, mod5, mod5, norm_g.reshape(1, d), w1, w3, w2, final_g.reshape(1, d))


def _abin_kernel(x_ref, shift_ref, scale_ref, g_ref, w_ref, *o_refs):
    h = _adaln(x_ref[...], g_ref[...], shift_ref[...], scale_ref[...]).astype(BF16)
    off = 0
    for o_ref in o_refs:
        n = o_ref.shape[-1]
        o_ref[...] = _dot(h, w_ref[:, off:off + n])
        off += n


def _ab_in(x, mod5, layer, row_of_batch, norm_g, w_in_perm, widths, *, tm=512):
    bsz, length, d = x.shape
    tm = min(tm, length)
    n_all = w_in_perm.shape[1]
    shift_spec, scale_spec, _ = _mod_block_specs(d, layer, 1, row_of_batch)
    return pl.pallas_call(
        _abin_kernel,
        out_shape=[jax.ShapeDtypeStruct((bsz, length, n), F32) for n in widths],
        grid=(bsz, length // tm),
        in_specs=[pl.BlockSpec((None, tm, d), lambda b, t: (b, t, 0)), shift_spec, scale_spec,
                  _const_spec((1, d)), _const_spec((d, n_all))],
        out_specs=[pl.BlockSpec((None, tm, n), lambda b, t: (b, t, 0)) for n in widths],
        compiler_params=_cparams(2),
        name="ab_in_proj",
    )(x, mod5, mod5, norm_g.reshape(1, d), w_in_perm)


def _level_table(c):
    j = np.arange(c)[:, None]
    i = np.arange(c)[None, :]
    x = j ^ i
    bits = np.zeros((c, c), np.int32)
    for k in range(int(np.log2(c))):
        bits = np.where(x >= (1 << k), k + 1, bits)
    return np.where(i < j, bits, 0).astype(np.int32)


def _fine_decay(forget, rev):
    c, w = forget.shape
    f3 = forget.reshape(c // 8, 8, w)
    pos = lax.broadcasted_iota(jnp.int32, (c // 8, 8, w), 1)

    def at(offset):
        return f3 if offset == 0 else pltpu.roll(f3, (-offset) % 8, 1)

    sgn = 1 if rev else -1
    cq = [at(0)]
    ck = [None]
    for t in range(1, 4):
        cq.append(cq[-1] * at(sgn * t))
        ck.append(at(-sgn * t) if ck[-1] is None else ck[-1] * at(-sgn * t))
    out = {}
    for s in (2, 4, 8):
        half = s // 2
        p = pos % s
        e = jnp.ones_like(f3)
        for t in range(half):
            q_pos = (half - 1 - t) if rev else (half + t)
            k_pos = (half + t) if rev else (half - 1 - t)
            e = jnp.where(p == q_pos, cq[t], e)
            if t > 0:
                e = jnp.where(p == k_pos, ck[t], e)
        out[s] = e.reshape(c, w)
    return out


def _hgrn_head(qq, kk, v, forget, b, st, lvl, rev):
    c = qq.shape[0]
    b_end = b[0:1] if rev else b[c - 1:c]
    vb = v.astype(BF16)
    o = _dot_nt((qq * jnp.exp(b)).astype(BF16), st.astype(BF16))
    k_out = (kk * jnp.exp(b_end - b)).astype(BF16)
    st_new = st * jnp.exp(b_end) + _dot_tn(vb, k_out)

    row = lax.broadcasted_iota(jnp.int32, (c, 1), 0)
    fine = _fine_decay(forget, rev)
    a = jnp.zeros((c, c), F32)
    s = c
    while s >= 2:
        half = s // 2
        q_side = ((row % s) >= half) != rev
        if s >= 16:
            ridx = half if rev else half - 1
            r = b.reshape(c // s, s, b.shape[1])[:, ridx:ridx + 1, :]
            r = jnp.broadcast_to(r, (c // s, s, b.shape[1])).reshape(c, b.shape[1])
            e = jnp.exp(-jnp.abs(b - r))
        else:
            e = fine[s]
        x = (jnp.where(q_side, qq, kk) * e).astype(BF16)
        p = _dot_nt(x, x)
        a = jnp.where(lvl == int(np.log2(s)), p, a)
        s //= 2
    o = o + _dot(a.astype(BF16), vb) + jnp.sum(qq * kk, axis=-1, keepdims=True) * v
    return o, st_new


def _hgrn_kernel(qf_ref, vf_ref, ff_ref, qb_ref, vb_ref, fb_ref, lb_ref, s0_ref, tri_ref, lvl_ref,
                 of_ref, ob_ref, sout_ref, s_scr, *, slot):
    t = pl.program_id(1)

    @pl.when(t == 0)
    def _():
        s_scr[...] = s0_ref[...]

    raw = lb_ref[...]
    ex = jnp.exp(raw - jnp.max(raw, axis=0, keepdims=True))
    sm = ex / jnp.sum(ex, axis=0, keepdims=True)
    lb_all = sm[0]
    for k in range(1, slot + 1):
        lb_all = lb_all + sm[k]

    n_heads = qf_ref.shape[-1] // HGRN_HEAD_DIM
    q_scale = HGRN_HEAD_DIM ** -0.5
    dirs = ((qf_ref, vf_ref, ff_ref, of_ref), (qb_ref, vb_ref, fb_ref, ob_ref))
    for d, (q_ref, v_ref, f_ref, o_ref) in enumerate(dirs):
        rev = d == 1
        lb = lb_all[d:d + 1]
        forget = lb + (1.0 - lb) * _sigmoid(f_ref[...])
        logf = jnp.log(forget)
        kk = 1.0 - forget
        qq = _silu(q_ref[...]) * q_scale
        v = v_ref[...]
        hi = logf.astype(BF16)
        lo = (logf - hi.astype(F32)).astype(BF16)
        tri = tri_ref[d]
        b = _dot(tri, hi) + _dot(tri, lo)
        lvl = lvl_ref[d]
        for h in range(n_heads):
            sl = slice(h * HGRN_HEAD_DIM, (h + 1) * HGRN_HEAD_DIM)
            o_h, st_new = _hgrn_head(qq[:, sl], kk[:, sl], v[:, sl], forget[:, sl], b[:, sl],
                                     s_scr[d, h], lvl, rev)
            o_ref[:, sl] = o_h
            s_scr[d, h] = st_new

    @pl.when(t == pl.num_programs(1) - 1)
    def _():
        sout_ref[...] = s_scr[...]


def _hgrn_scan(q, v, f_fwd, f_bwd, hgrn_lb, s0, slot, *, chunk=128):
    bsz, length, dh = q.shape
    c = min(chunk, length)
    n = length // c
    n_heads = dh // HGRN_HEAD_DIM
    lvl_f = _level_table(c)
    lvl = jnp.asarray(np.stack([lvl_f, lvl_f.T]))
    tri_f = np.tril(np.ones((c, c), np.float32))
    tri = jnp.asarray(np.stack([tri_f, tri_f.T]), BF16)
    fwd = lambda b, t: (b, t, 0)
    bwd = lambda b, t: (b, n - 1 - t, 0)
    blk = (None, c, dh)
    s_shape = (2, n_heads, HGRN_HEAD_DIM, HGRN_HEAD_DIM)
    s_spec = pl.BlockSpec((None,) + s_shape, lambda b, t: (b, 0, 0, 0, 0))
    kern = functools.partial(_hgrn_kernel, slot=slot)
    return pl.pallas_call(
        kern,
        out_shape=[jax.ShapeDtypeStruct(q.shape, F32), jax.ShapeDtypeStruct(q.shape, F32),
                   jax.ShapeDtypeStruct((bsz,) + s_shape, F32)],
        grid=(bsz, n),
        in_specs=[pl.BlockSpec(blk, fwd), pl.BlockSpec(blk, fwd), pl.BlockSpec(blk, fwd),
                  pl.BlockSpec(blk, bwd), pl.BlockSpec(blk, bwd), pl.BlockSpec(blk, bwd),
                  _const_spec(hgrn_lb.shape), s_spec, _const_spec((2, c, c)), _const_spec((2, c, c))],
        out_specs=[pl.BlockSpec(blk, fwd), pl.BlockSpec(blk, bwd), s_spec],
        scratch_shapes=[pltpu.VMEM(s_shape, F32)],
        compiler_params=_cparams(2),
        name="hgrn2_scan",
    )(q, v, f_fwd, q, v, f_bwd, hgrn_lb, s0, tri, lvl)


def _pool_tables(tm, row_len):
    t = np.arange(tm)
    same_row = (t[:, None] // row_len) == (t[None, :] // row_len)
    pos = t % row_len
    bands, inv = [], []
    for w in POOL_WINDOWS:
        lo = np.clip(pos - w // 2, 0, row_len)
        hi = np.clip(pos - w // 2 + w, 0, row_len)
        u = pos[None, :]
        bands.append(same_row & (u >= lo[:, None]) & (u < hi[:, None]))
        inv.append(1.0 / (hi - lo).astype(np.float64))
    return np.stack(bands).astype(np.float32), np.stack(inv, axis=1).astype(np.float32)


def _about_kernel(x_ref, of_ref, ob_ref, g_ref, a_ref, gate_ref, gn_ref, band_ref, icnt_ref, wp_ref,
                  ps_ref, wo_ref, o_ref):
    o = of_ref[...] + ob_ref[...]
    dh = o.shape[-1]
    heads = []
    for h in range(dh // HGRN_HEAD_DIM):
        oh = o[:, h * HGRN_HEAD_DIM:(h + 1) * HGRN_HEAD_DIM]
        heads.append(oh * lax.rsqrt(jnp.mean(oh * oh, axis=-1, keepdims=True) + EPS))
    b_mix = jnp.concatenate(heads, axis=-1) * gn_ref[...] * _silu(g_ref[...])

    xa = a_ref[...]
    dp = xa.shape[-1]
    xab = xa.astype(BF16)
    n_groups = band_ref.shape[0]
    lane_group = lax.broadcasted_iota(jnp.int32, xa.shape, 1) // (dp // n_groups)
    total = jnp.zeros_like(xa)
    for gi in range(n_groups):
        total = jnp.where(lane_group == gi, _dot(band_ref[gi], xab), total)
    y = total * icnt_ref[...] - xa
    a_mix = _dot(y.astype(BF16), wp_ref[...]) * ps_ref[...]

    mix = _dot(a_mix.astype(BF16), wo_ref[0:dp, :]) + _dot(b_mix.astype(BF16), wo_ref[dp:, :])
    o_ref[...] = x_ref[...] + gate_ref[...] * mix


def _ab_out(x, o_f, o_b, g, xa, mod5, layer, row_of_batch, g_norm, w_pool_bd, pool_scale, w_out, row_len,
            *, tm=256):
    bsz, length, d = x.shape
    dh = o_f.shape[-1]
    dp = xa.shape[-1]
    tm = min(tm, length)
    n_groups = len(POOL_WINDOWS)
    bands, inv = _pool_tables(tm, row_len)
    icnt = jnp.asarray(np.repeat(inv, dp // n_groups, axis=1))
    _, _, gate_spec = _mod_block_specs(d, layer, 1, row_of_batch)
    tok = lambda n: pl.BlockSpec((None, tm, n), lambda b, t: (b, t, 0))
    return pl.pallas_call(
        _about_kernel,
        out_shape=jax.ShapeDtypeStruct(x.shape, F32),
        grid=(bsz, length // tm),
        in_specs=[tok(d), tok(dh), tok(dh), tok(dh), tok(dp), gate_spec, _const_spec((1, dh)),
                  _const_spec((n_groups, tm, tm)), _const_spec((tm, dp)), _const_spec((dp, dp)),
                  _const_spec((1, dp)), _const_spec((dp + dh, d))],
        out_specs=tok(d),
        compiler_params=_cparams(2),
        name="ab_out_proj",
    )(x, o_f, o_b, g, xa, mod5, g_norm.reshape(1, dh), jnp.asarray(bands, BF16), icnt, w_pool_bd,
      pool_scale.reshape(1, dp), w_out)


def _gelu_tanh(x):
    return 0.5 * x * (1.0 + jnp.tanh(np.sqrt(2.0 / np.pi).astype(np.float32) * (x + 0.044715 * (x * x * x))))


def _gmlp_kernel(x_ref, shift_ref, scale_ref, gate_ref, g_ref, win_ref, lng_ref, lnb_ref, ws_ref, bs_ref,
                 wout_ref, o_ref, s_ref):
    x = x_ref[...]
    tm = x.shape[0]
    h = _adaln(x, g_ref[...], shift_ref[...], scale_ref[...]).astype(BF16)
    dg = win_ref.shape[1] // 2
    u = _gelu_tanh(_dot(h, win_ref[:, :dg]))
    v = _gelu_tanh(_dot(h, win_ref[:, dg:]))
    vc = v - jnp.mean(v, axis=-1, keepdims=True)
    v = vc * lax.rsqrt(jnp.mean(vc * vc, axis=-1, keepdims=True) + EPS) * lng_ref[...] + lnb_ref[...]
    vb = v.astype(BF16)
    for c in range(tm // GMLP_CHUNK):
        rows = slice(c * GMLP_CHUNK, (c + 1) * GMLP_CHUNK)
        for gi in range(dg // GMLP_GROUP_DIM):
            cols = slice(gi * GMLP_GROUP_DIM, (gi + 1) * GMLP_GROUP_DIM)
            sv = _dot(ws_ref[gi], vb[rows, cols]) + bs_ref[gi]
            s_ref[rows, cols] = (u[rows, cols] * sv).astype(BF16)
    o_ref[...] = x + gate_ref[...] * _dot(s_ref[...], wout_ref[...])


def _gmlp(x, mod5, layer, row_of_batch, norm_g, w_in, ln_g, ln_b, w_s, b_s, w_out, *, tm=512):
    bsz, length, d = x.shape
    dg = w_in.shape[1] // 2
    n_groups = w_s.shape[0]
    tm = min(tm, length)
    b_s_wide = jnp.broadcast_to(b_s[:, :, None], (n_groups, GMLP_CHUNK, GMLP_GROUP_DIM))
    return pl.pallas_call(
        _gmlp_kernel,
        out_shape=jax.ShapeDtypeStruct(x.shape, F32),
        grid=(bsz, length // tm),
        in_specs=[pl.BlockSpec((None, tm, d), lambda b, t: (b, t, 0))]
        + _mod_block_specs(d, layer, 1, row_of_batch)
        + [_const_spec((1, d)), _const_spec((d, 2 * dg)), _const_spec((1, dg)), _const_spec((1, dg)),
           _const_spec((n_groups, GMLP_CHUNK, GMLP_CHUNK)),
           _const_spec((n_groups, GMLP_CHUNK, GMLP_GROUP_DIM)), _const_spec((dg, d))],
        out_specs=pl.BlockSpec((None, tm, d), lambda b, t: (b, t, 0)),
        scratch_shapes=[pltpu.VMEM((tm, dg), BF16)],
        compiler_params=_cparams(2),
        name="gmlp_mixer",
    )(x, mod5, mod5, mod5, norm_g.reshape(1, d), w_in, ln_g.reshape(1, dg), ln_b.reshape(1, dg), w_s,
      b_s_wide, w_out)


def _block_diag(w):
    g, a, b = w.shape
    out = jnp.zeros((g * a, g * b), w.dtype)
    for i in range(g):
        out = out.at[i * a:(i + 1) * a, i * b:(i + 1) * b].set(w[i])
    return out


def kernel(x, c, ctx, c_ctx, mod_w, mod_b, norm_g, ffn_w1, ffn_w3, ffn_w2, ab_w_in, pool_w, pool_scale,
           hgrn_lb, hgrn_norm_g, ab_w_out, gmlp_w_in, gmlp_ln_g, gmlp_ln_b, gmlp_w_s, gmlp_b_s, gmlp_w_out,
           final_g):
    bsz, _, d = x.shape
    depth = mod_w.shape[0]
    d_pool = pool_scale.shape[-1]
    d_hgrn = hgrn_norm_g.shape[-1]
    n_heads = d_hgrn // HGRN_HEAD_DIM

    c_rows = jnp.zeros((MOD_ROWS, d), F32).at[:bsz].set(c).at[bsz].set(c_ctx)
    mod = _modulation(c_rows, mod_w, mod_b)
    mod5 = mod.reshape(depth, MOD_ROWS, N_MOD, 1, d)
    lat_row = lambda b: b
    ctx_row = lambda b: bsz

    w1 = ffn_w1.astype(BF16)
    w3 = ffn_w3.astype(BF16)
    w2 = ffn_w2.astype(BF16)

    xl, xc = x, ctx
    for i in range(depth):
        ctx_live = any(j % 2 == 0 for j in range(i, depth))
        last = i == depth - 1

        xl = _ffn(xl, mod5, i, 0, lat_row, norm_g[i, 0], w1[i, 0], w3[i, 0], w2[i, 0], final_g)
        if ctx_live:
            xc = _ffn(xc, mod5, i, 0, ctx_row, norm_g[i, 0], w1[i, 0], w3[i, 0], w2[i, 0], final_g)

        if i % 2 == 0:
            e = i // 2
            w_in = jnp.concatenate([ab_w_in[e][:, d_pool:], ab_w_in[e][:, :d_pool]], axis=1).astype(BF16)
            widths = (d_hgrn,) * 5 + (d_pool,)
            w_pool_bd = _block_diag(pool_w[e]).astype(BF16)
            w_out = ab_w_out[e].astype(BF16)
            q_c, v_c, ff_c, fb_c, g_c, a_c = _ab_in(xc, mod5, i, ctx_row, norm_g[i, 1], w_in, widths)
            q_l, v_l, ff_l, fb_l, g_l, a_l = _ab_in(xl, mod5, i, lat_row, norm_g[i, 1], w_in, widths)
            s0 = jnp.zeros((bsz, 2, n_heads, HGRN_HEAD_DIM, HGRN_HEAD_DIM), F32)
            of_c, ob_c, s_ctx = _hgrn_scan(q_c, v_c, ff_c, fb_c, hgrn_lb, s0, e)
            of_l, ob_l, _ = _hgrn_scan(q_l, v_l, ff_l, fb_l, hgrn_lb, s_ctx, e)
            xc = _ab_out(xc, of_c, ob_c, g_c, a_c, mod5, i, ctx_row, hgrn_norm_g[e], w_pool_bd,
                         pool_scale[e], w_out, xc.shape[1])
            xl = _ab_out(xl, of_l, ob_l, g_l, a_l, mod5, i, lat_row, hgrn_norm_g[e], w_pool_bd,
                         pool_scale[e], w_out, GRID_W)
        else:
            o = i // 2
            gm = (gmlp_w_in[o].astype(BF16), gmlp_ln_g[o], gmlp_ln_b[o], gmlp_w_s[o].astype(BF16),
                  gmlp_b_s[o], gmlp_w_out[o].astype(BF16))
            xl = _gmlp(xl, mod5, i, lat_row, norm_g[i, 1], *gm)
            if ctx_live:
                xc = _gmlp(xc, mod5, i, ctx_row, norm_g[i, 1], *gm)

        xl = _ffn(xl, mod5, i, 2, lat_row, norm_g[i, 2], w1[i, 1], w3[i, 1], w2[i, 1], final_g,
                  final_norm=last)
        if ctx_live:
            xc = _ffn(xc, mod5, i, 2, ctx_row, norm_g[i, 2], w1[i, 1], w3[i, 1], w2[i, 1], final_g)
    return xl
```

```python
import functools

import numpy as np
import jax
import jax.numpy as jnp
from jax import lax
from jax.experimental import pallas as pl
from jax.experimental.pallas import tpu as pltpu

F32 = jnp.float32
BF16 = jnp.bfloat16

EPS = 1e-6
N_MOD = 9
GRID_W = 64
POOL_WINDOWS = (2, 4, 8, 16)
HGRN_HEAD_DIM = 128
GMLP_CHUNK = 128
GMLP_GROUP_DIM = 128

LOG2_E = float(np.log2(np.e))
HGRN_DIAG_BLOCK = 32
HGRN_SAFE_LOG2_SPAN = 80.0

MOD_ROWS = 8
VMEM_LIMIT_BYTES = 56 * 1024 * 1024


def _cparams(n_grid_dims):
    return pltpu.CompilerParams(
        dimension_semantics=("arbitrary",) * n_grid_dims,
        vmem_limit_bytes=VMEM_LIMIT_BYTES)


def _sigmoid(x):
    return 1.0 / (1.0 + jnp.exp(-x))


def _silu(x):
    return x * _sigmoid(x)


def _rms(x, g):
    return x * lax.rsqrt(jnp.mean(x * x, axis=-1, keepdims=True) + EPS) * g


def _adaln(x, g, shift, scale):
    return _rms(x, g) * (1.0 + scale) + shift


def _dot(a, b):
    return jnp.dot(a, b, preferred_element_type=F32)


def _dot_nt(a, b):
    return lax.dot_general(a, b, (((1,), (1,)), ((), ())), preferred_element_type=F32)


def _dot_tn(a, b):
    return lax.dot_general(a, b, (((0,), (0,)), ((), ())), preferred_element_type=F32)


def _const_spec(shape):
    nd = len(shape)
    return pl.BlockSpec(shape, lambda *_: (0,) * nd)


def _mod_kernel(c_ref, w_ref, b_ref, o_ref):
    a = _silu(c_ref[...])
    o_ref[...] = jnp.dot(a, w_ref[...], preferred_element_type=F32,
                         precision=lax.Precision.HIGHEST) + b_ref[...]


def _modulation(c_rows, mod_w, mod_b, tn=1152):
    depth, d, n = mod_w.shape
    return pl.pallas_call(
        _mod_kernel,
        out_shape=jax.ShapeDtypeStruct((depth, MOD_ROWS, n), F32),
        grid=(depth, n // tn),
        in_specs=[
            pl.BlockSpec((MOD_ROWS, d), lambda l, j: (0, 0)),
            pl.BlockSpec((None, d, tn), lambda l, j: (l, 0, j)),
            pl.BlockSpec((None, 1, tn), lambda l, j: (l, 0, j)),
        ],
        out_specs=pl.BlockSpec((None, MOD_ROWS, tn), lambda l, j: (l, 0, j)),
        compiler_params=_cparams(2),
        name="modulation",
    )(c_rows, mod_w, mod_b.reshape(depth, 1, n))


def _mod_index(layer, col, row_of_batch, b, t):
    return (layer, row_of_batch(b), col, 0, 0)


def _mod_block_specs(d_model, layer, sub, row_of_batch):
    return [pl.BlockSpec((None, None, None, 1, d_model),
                         functools.partial(_mod_index, layer, 3 * sub + k, row_of_batch))
            for k in range(3)]


def _ffn_kernel(x_ref, shift_ref, scale_ref, gate_ref, g_ref, w1_ref, w3_ref, w2_ref, fg_ref,
                o_ref, a_ref, *, n_chunk, final_norm):
    x = x_ref[...]
    h = _adaln(x, g_ref[...], shift_ref[...], scale_ref[...]).astype(BF16)
    d_ff = w1_ref.shape[1]
    for c in range(d_ff // n_chunk):
        sl = slice(c * n_chunk, (c + 1) * n_chunk)
        u = _dot(h, w1_ref[:, sl])
        v = _dot(h, w3_ref[:, sl])
        a_ref[:, sl] = (_silu(u) * v).astype(BF16)
    y = _dot(a_ref[...], w2_ref[...])
    out = x + (0.5 * gate_ref[...]) * y
    if final_norm:
        out = _rms(out, fg_ref[...])
    o_ref[...] = out


def _ffn(x, mod5, layer, sub, row_of_batch, norm_g, w1, w3, w2, final_g, *, final_norm=False, tm=512,
         n_chunk=256):
    bsz, length, d = x.shape
    d_ff = w1.shape[1]
    tm = min(tm, length)
    kern = functools.partial(_ffn_kernel, n_chunk=n_chunk, final_norm=final_norm)
    return pl.pallas_call(
        kern,
        out_shape=jax.ShapeDtypeStruct(x.shape, F32),
        grid=(bsz, length // tm),
        in_specs=[pl.BlockSpec((None, tm, d), lambda b, t: (b, t, 0))]
        + _mod_block_specs(d, layer, sub, row_of_batch)
        + [_const_spec((1, d)), _const_spec((d, d_ff)), _const_spec((d, d_ff)), _const_spec((d_ff, d)),
           _const_spec((1, d))],
        out_specs=pl.BlockSpec((None, tm, d), lambda b, t: (b, t, 0)),
        scratch_shapes=[pltpu.VMEM((tm, d_ff), BF16)],
        compiler_params=_cparams(2),
        name="swiglu_ffn",
    )(x, mod5, mod5, mod5, norm_g.reshape(1, d), w1, w3, w2, final_g.reshape(1, d))


def _abin_kernel(x_ref, shift_ref, scale_ref, g_ref, w_ref, *o_refs):
    h = _adaln(x_ref[...], g_ref[...], shift_ref[...], scale_ref[...]).astype(BF16)
    off = 0
    for o_ref in o_refs:
        n = o_ref.shape[-1]
        o_ref[...] = _dot(h, w_ref[:, off:off + n])
        off += n


def _ab_in(x, mod5, layer, row_of_batch, norm_g, w_in_perm, widths, *, tm=512):
    bsz, length, d = x.shape
    tm = min(tm, length)
    n_all = w_in_perm.shape[1]
    shift_spec, scale_spec, _ = _mod_block_specs(d, layer, 1, row_of_batch)
    return pl.pallas_call(
        _abin_kernel,
        out_shape=[jax.ShapeDtypeStruct((bsz, length, n), F32) for n in widths],
        grid=(bsz, length // tm),
        in_specs=[pl.BlockSpec((None, tm, d), lambda b, t: (b, t, 0)), shift_spec, scale_spec,
                  _const_spec((1, d)), _const_spec((d, n_all))],
        out_specs=[pl.BlockSpec((None, tm, n), lambda b, t: (b, t, 0)) for n in widths],
        compiler_params=_cparams(2),
        name="ab_in_proj",
    )(x, mod5, mod5, norm_g.reshape(1, d), w_in_perm)


def _level_table(c, diag_block):
    j = np.arange(c)[:, None]
    i = np.arange(c)[None, :]
    x = j ^ i
    bits = np.zeros((c, c), np.int32)
    for k in range(int(np.log2(c))):
        bits = np.where(x >= (1 << k), k + 1, bits)
    lvl = np.where(i < j, bits, 0)
    if diag_block > 1:
        lvl = np.where((i <= j) & (bits <= int(np.log2(diag_block))), int(np.log2(diag_block)), lvl)
    return lvl.astype(np.int32)


def _fine_decay(forget, rev):
    c, w = forget.shape
    f3 = forget.reshape(c // 8, 8, w)
    pos = lax.broadcasted_iota(jnp.int32, (c // 8, 8, w), 1)

    def at(offset):
        return f3 if offset == 0 else pltpu.roll(f3, (-offset) % 8, 1)

    sgn = 1 if rev else -1
    cq = [at(0)]
    ck = [None]
    for t in range(1, 4):
        cq.append(cq[-1] * at(sgn * t))
        ck.append(at(-sgn * t) if ck[-1] is None else ck[-1] * at(-sgn * t))
    out = {}
    for s in (2, 4, 8):
        half = s // 2
        p = pos % s
        e = jnp.ones_like(f3)
        for t in range(half):
            q_pos = (half - 1 - t) if rev else (half + t)
            k_pos = (half + t) if rev else (half - 1 - t)
            e = jnp.where(p == q_pos, cq[t], e)
            if t > 0:
                e = jnp.where(p == k_pos, ck[t], e)
        out[s] = e.reshape(c, w)
    return out


def _block_row(b, s, ridx):
    c, w = b.shape
    r = b.reshape(c // s, s, w)[:, ridx:ridx + 1, :]
    return jnp.broadcast_to(r, (c // s, s, w)).reshape(c, w)


def _hgrn_head(qq, kk, v, forget, b, st, lvl, rev, diag_block):
    c = qq.shape[0]
    b_end = b[0:1] if rev else b[c - 1:c]
    vb = v.astype(BF16)
    o = _dot_nt((qq * jnp.exp2(b)).astype(BF16), st.astype(BF16))
    k_out = (kk * jnp.exp2(b_end - b)).astype(BF16)
    st_new = st * jnp.exp2(b_end) + _dot_tn(vb, k_out)

    row = lax.broadcasted_iota(jnp.int32, (c, 1), 0)
    fine = _fine_decay(forget, rev) if diag_block < 8 else None
    a = jnp.zeros((c, c), F32)
    s = c
    while s > diag_block:
        half = s // 2
        q_side = ((row % s) >= half) != rev
        if s >= 16:
            e = jnp.exp2(-jnp.abs(b - _block_row(b, s, half if rev else half - 1)))
        else:
            e = fine[s]
        x = (jnp.where(q_side, qq, kk) * e).astype(BF16)
        a = jnp.where(lvl == int(np.log2(s)), _dot_nt(x, x), a)
        s //= 2
    if diag_block > 1:
        d = b - _block_row(b, diag_block, diag_block - 1 if rev else 0)
        p = _dot_nt((qq * jnp.exp2(d)).astype(BF16), (kk * jnp.exp2(-d)).astype(BF16))
        a = jnp.where(lvl == int(np.log2(diag_block)), p, a)
        o = o + _dot(a.astype(BF16), vb)
    else:
        o = o + _dot(a.astype(BF16), vb) + jnp.sum(qq * kk, axis=-1, keepdims=True) * v
    return o, st_new


def _hgrn_kernel(qf_ref, vf_ref, ff_ref, qb_ref, vb_ref, fb_ref, lb_ref, s0_ref, tri_ref, lvl_ref,
                 of_ref, ob_ref, sout_ref, s_scr, qq_s, kk_s, fg_s, b_s, *, slot):
    t = pl.program_id(1)

    @pl.when(t == 0)
    def _():
        s_scr[...] = s0_ref[...]

    raw = lb_ref[...]
    ex = jnp.exp(raw - jnp.max(raw, axis=0, keepdims=True))
    sm = ex / jnp.sum(ex, axis=0, keepdims=True)
    lb_all = sm[0]
    for k in range(1, slot + 1):
        lb_all = lb_all + sm[k]

    c = qf_ref.shape[0]
    n_heads = qf_ref.shape[-1] // HGRN_HEAD_DIM
    q_scale = HGRN_HEAD_DIM ** -0.5
    dirs = ((qf_ref, vf_ref, ff_ref, of_ref), (qb_ref, vb_ref, fb_ref, ob_ref))
    weakest = None
    for d, (q_ref, v_ref, f_ref, o_ref) in enumerate(dirs):
        lb = lb_all[d:d + 1]
        forget = lb + (1.0 - lb) * _sigmoid(f_ref[...])
        logf = jnp.log(forget) * LOG2_E
        hi = logf.astype(BF16)
        lo = (logf - hi.astype(F32)).astype(BF16)
        tri = tri_ref[d]
        b = _dot(tri, hi) + _dot(tri, lo)
        fg_s[d] = forget
        kk_s[d] = 1.0 - forget
        qq_s[d] = _silu(q_ref[...]) * q_scale
        b_s[d] = b
        for m in range(c // HGRN_DIAG_BLOCK):
            lo_row, hi_row = m * HGRN_DIAG_BLOCK, (m + 1) * HGRN_DIAG_BLOCK - 1
            if d == 0:
                span = b[hi_row:hi_row + 1] - b[lo_row:lo_row + 1]
            else:
                span = b[lo_row:lo_row + 1] - b[hi_row:hi_row + 1]
            weakest = span if weakest is None else jnp.minimum(weakest, span)
    in_range = jnp.min(weakest) > -HGRN_SAFE_LOG2_SPAN

    def run(diag_block):
        for d, (q_ref, v_ref, f_ref, o_ref) in enumerate(dirs):
            lvl = lvl_ref[(0 if diag_block > 1 else 2) + d]
            for h in range(n_heads):
                sl = slice(h * HGRN_HEAD_DIM, (h + 1) * HGRN_HEAD_DIM)
                o_h, st_new = _hgrn_head(qq_s[d, :, sl], kk_s[d, :, sl], v_ref[:, sl], fg_s[d, :, sl],
                                         b_s[d, :, sl], s_scr[d, h], lvl, d == 1, diag_block)
                o_ref[:, sl] = o_h
                s_scr[d, h] = st_new

    @pl.when(in_range)
    def _():
        run(HGRN_DIAG_BLOCK)

    @pl.when(jnp.logical_not(in_range))
    def _():
        run(1)

    @pl.when(t == pl.num_programs(1) - 1)
    def _():
        sout_ref[...] = s_scr[...]


def _hgrn_scan(q, v, f_fwd, f_bwd, hgrn_lb, s0, slot, *, chunk=128):
    bsz, length, dh = q.shape
    c = min(chunk, length)
    n = length // c
    n_heads = dh // HGRN_HEAD_DIM
    lvl_fast = _level_table(c, HGRN_DIAG_BLOCK)
    lvl_full = _level_table(c, 1)
    lvl = jnp.asarray(np.stack([lvl_fast, lvl_fast.T, lvl_full, lvl_full.T]))
    tri_f = np.tril(np.ones((c, c), np.float32))
    tri = jnp.asarray(np.stack([tri_f, tri_f.T]), BF16)
    fwd = lambda b, t: (b, t, 0)
    bwd = lambda b, t: (b, n - 1 - t, 0)
    blk = (None, c, dh)
    s_shape = (2, n_heads, HGRN_HEAD_DIM, HGRN_HEAD_DIM)
    s_spec = pl.BlockSpec((None,) + s_shape, lambda b, t: (b, 0, 0, 0, 0))
    kern = functools.partial(_hgrn_kernel, slot=slot)
    return pl.pallas_call(
        kern,
        out_shape=[jax.ShapeDtypeStruct(q.shape, F32), jax.ShapeDtypeStruct(q.shape, F32),
                   jax.ShapeDtypeStruct((bsz,) + s_shape, F32)],
        grid=(bsz, n),
        in_specs=[pl.BlockSpec(blk, fwd), pl.BlockSpec(blk, fwd), pl.BlockSpec(blk, fwd),
                  pl.BlockSpec(blk, bwd), pl.BlockSpec(blk, bwd), pl.BlockSpec(blk, bwd),
                  _const_spec(hgrn_lb.shape), s_spec, _const_spec((2, c, c)), _const_spec((4, c, c))],
        out_specs=[pl.BlockSpec(blk, fwd), pl.BlockSpec(blk, bwd), s_spec],
        scratch_shapes=[pltpu.VMEM(s_shape, F32)] + [pltpu.VMEM((2, c, dh), F32)] * 4,
        compiler_params=_cparams(2),
        name="hgrn2_scan",
    )(q, v, f_fwd, q, v, f_bwd, hgrn_lb, s0, tri, lvl)


def _pool_tables(tm, row_len):
    t = np.arange(tm)
    same_row = (t[:, None] // row_len) == (t[None, :] // row_len)
    pos = t % row_len
    bands, inv = [], []
    for w in POOL_WINDOWS:
        lo = np.clip(pos - w // 2, 0, row_len)
        hi = np.clip(pos - w // 2 + w, 0, row_len)
        u = pos[None, :]
        bands.append(same_row & (u >= lo[:, None]) & (u < hi[:, None]))
        inv.append(1.0 / (hi - lo).astype(np.float64))
    return np.stack(bands).astype(np.float32), np.stack(inv, axis=1).astype(np.float32)


def _about_kernel(x_ref, of_ref, ob_ref, g_ref, a_ref, gate_ref, gn_ref, band_ref, icnt_ref, wp_ref,
                  ps_ref, wo_ref, o_ref):
    o = of_ref[...] + ob_ref[...]
    dh = o.shape[-1]
    heads = []
    for h in range(dh // HGRN_HEAD_DIM):
        oh = o[:, h * HGRN_HEAD_DIM:(h + 1) * HGRN_HEAD_DIM]
        heads.append(oh * lax.rsqrt(jnp.mean(oh * oh, axis=-1, keepdims=True) + EPS))
    b_mix = jnp.concatenate(heads, axis=-1) * gn_ref[...] * _silu(g_ref[...])

    xa = a_ref[...]
    dp = xa.shape[-1]
    xab = xa.astype(BF16)
    n_groups = band_ref.shape[0]
    lane_group = lax.broadcasted_iota(jnp.int32, xa.shape, 1) // (dp // n_groups)
    total = jnp.zeros_like(xa)
    for gi in range(n_groups):
        total = jnp.where(lane_group == gi, _dot(band_ref[gi], xab), total)
    y = total * icnt_ref[...] - xa
    a_mix = _dot(y.astype(BF16), wp_ref[...]) * ps_ref[...]

    mix = _dot(a_mix.astype(BF16), wo_ref[0:dp, :]) + _dot(b_mix.astype(BF16), wo_ref[dp:, :])
    o_ref[...] = x_ref[...] + gate_ref[...] * mix


def _ab_out(x, o_f, o_b, g, xa, mod5, layer, row_of_batch, g_norm, w_pool_bd, pool_scale, w_out, row_len,
            *, tm=256):
    bsz, length, d = x.shape
    dh = o_f.shape[-1]
    dp = xa.shape[-1]
    tm = min(tm, length)
    n_groups = len(POOL_WINDOWS)
    bands, inv = _pool_tables(tm, row_len)
    icnt = jnp.asarray(np.repeat(inv, dp // n_groups, axis=1))
    _, _, gate_spec = _mod_block_specs(d, layer, 1, row_of_batch)
    tok = lambda n: pl.BlockSpec((None, tm, n), lambda b, t: (b, t, 0))
    return pl.pallas_call(
        _about_kernel,
        out_shape=jax.ShapeDtypeStruct(x.shape, F32),
        grid=(bsz, length // tm),
        in_specs=[tok(d), tok(dh), tok(dh), tok(dh), tok(dp), gate_spec, _const_spec((1, dh)),
                  _const_spec((n_groups, tm, tm)), _const_spec((tm, dp)), _const_spec((dp, dp)),
                  _const_spec((1, dp)), _const_spec((dp + dh, d))],
        out_specs=tok(d),
        compiler_params=_cparams(2),
        name="ab_out_proj",
    )(x, o_f, o_b, g, xa, mod5, g_norm.reshape(1, dh), jnp.asarray(bands, BF16), icnt, w_pool_bd,
      pool_scale.reshape(1, dp), w_out)


def _gelu_tanh(x):
    return 0.5 * x * (1.0 + jnp.tanh(np.sqrt(2.0 / np.pi).astype(np.float32) * (x + 0.044715 * (x * x * x))))


def _gmlp_kernel(x_ref, shift_ref, scale_ref, gate_ref, g_ref, win_ref, lng_ref, lnb_ref, ws_ref, bs_ref,
                 wout_ref, o_ref, s_ref):
    x = x_ref[...]
    tm = x.shape[0]
    h = _adaln(x, g_ref[...], shift_ref[...], scale_ref[...]).astype(BF16)
    dg = win_ref.shape[1] // 2
    u = _gelu_tanh(_dot(h, win_ref[:, :dg]))
    v = _gelu_tanh(_dot(h, win_ref[:, dg:]))
    vc = v - jnp.mean(v, axis=-1, keepdims=True)
    v = vc * lax.rsqrt(jnp.mean(vc * vc, axis=-1, keepdims=True) + EPS) * lng_ref[...] + lnb_ref[...]
    vb = v.astype(BF16)
    for c in range(tm // GMLP_CHUNK):
        rows = slice(c * GMLP_CHUNK, (c + 1) * GMLP_CHUNK)
        for gi in range(dg // GMLP_GROUP_DIM):
            cols = slice(gi * GMLP_GROUP_DIM, (gi + 1) * GMLP_GROUP_DIM)
            sv = _dot(ws_ref[gi], vb[rows, cols]) + bs_ref[gi]
            s_ref[rows, cols] = (u[rows, cols] * sv).astype(BF16)
    o_ref[...] = x + gate_ref[...] * _dot(s_ref[...], wout_ref[...])


def _gmlp(x, mod5, layer, row_of_batch, norm_g, w_in, ln_g, ln_b, w_s, b_s, w_out, *, tm=512):
    bsz, length, d = x.shape
    dg = w_in.shape[1] // 2
    n_groups = w_s.shape[0]
    tm = min(tm, length)
    b_s_wide = jnp.broadcast_to(b_s[:, :, None], (n_groups, GMLP_CHUNK, GMLP_GROUP_DIM))
    return pl.pallas_call(
        _gmlp_kernel,
        out_shape=jax.ShapeDtypeStruct(x.shape, F32),
        grid=(bsz, length // tm),
        in_specs=[pl.BlockSpec((None, tm, d), lambda b, t: (b, t, 0))]
        + _mod_block_specs(d, layer, 1, row_of_batch)
        + [_const_spec((1, d)), _const_spec((d, 2 * dg)), _const_spec((1, dg)), _const_spec((1, dg)),
           _const_spec((n_groups, GMLP_CHUNK, GMLP_CHUNK)),
           _const_spec((n_groups, GMLP_CHUNK, GMLP_GROUP_DIM)), _const_spec((dg, d))],
        out_specs=pl.BlockSpec((None, tm, d), lambda b, t: (b, t, 0)),
        scratch_shapes=[pltpu.VMEM((tm, dg), BF16)],
        compiler_params=_cparams(2),
        name="gmlp_mixer",
    )(x, mod5, mod5, mod5, norm_g.reshape(1, d), w_in, ln_g.reshape(1, dg), ln_b.reshape(1, dg), w_s,
      b_s_wide, w_out)


def _block_diag(w):
    g, a, b = w.shape
    out = jnp.zeros((g * a, g * b), w.dtype)
    for i in range(g):
        out = out.at[i * a:(i + 1) * a, i * b:(i + 1) * b].set(w[i])
    return out


def kernel(x, c, ctx, c_ctx, mod_w, mod_b, norm_g, ffn_w1, ffn_w3, ffn_w2, ab_w_in, pool_w, pool_scale,
           hgrn_lb, hgrn_norm_g, ab_w_out, gmlp_w_in, gmlp_ln_g, gmlp_ln_b, gmlp_w_s, gmlp_b_s, gmlp_w_out,
           final_g):
    bsz, _, d = x.shape
    depth = mod_w.shape[0]
    d_pool = pool_scale.shape[-1]
    d_hgrn = hgrn_norm_g.shape[-1]
    n_heads = d_hgrn // HGRN_HEAD_DIM

    c_rows = jnp.zeros((MOD_ROWS, d), F32).at[:bsz].set(c).at[bsz].set(c_ctx)
    mod = _modulation(c_rows, mod_w, mod_b)
    mod5 = mod.reshape(depth, MOD_ROWS, N_MOD, 1, d)
    lat_row = lambda b: b
    ctx_row = lambda b: bsz

    w1 = ffn_w1.astype(BF16)
    w3 = ffn_w3.astype(BF16)
    w2 = ffn_w2.astype(BF16)

    xl, xc = x, ctx
    for i in range(depth):
        ctx_live = any(j % 2 == 0 for j in range(i, depth))
        last = i == depth - 1

        xl = _ffn(xl, mod5, i, 0, lat_row, norm_g[i, 0], w1[i, 0], w3[i, 0], w2[i, 0], final_g)
        if ctx_live:
            xc = _ffn(xc, mod5, i, 0, ctx_row, norm_g[i, 0], w1[i, 0], w3[i, 0], w2[i, 0], final_g)

        if i % 2 == 0:
            e = i // 2
            w_in = jnp.concatenate([ab_w_in[e][:, d_pool:], ab_w_in[e][:, :d_pool]], axis=1).astype(BF16)
            widths = (d_hgrn,) * 5 + (d_pool,)
            w_pool_bd = _block_diag(pool_w[e]).astype(BF16)
            w_out = ab_w_out[e].astype(BF16)
            q_c, v_c, ff_c, fb_c, g_c, a_c = _ab_in(xc, mod5, i, ctx_row, norm_g[i, 1], w_in, widths)
            q_l, v_l, ff_l, fb_l, g_l, a_l = _ab_in(xl, mod5, i, lat_row, norm_g[i, 1], w_in, widths)
            s0 = jnp.zeros((bsz, 2, n_heads, HGRN_HEAD_DIM, HGRN_HEAD_DIM), F32)
            of_c, ob_c, s_ctx = _hgrn_scan(q_c, v_c, ff_c, fb_c, hgrn_lb, s0, e)
            of_l, ob_l, _ = _hgrn_scan(q_l, v_l, ff_l, fb_l, hgrn_lb, s_ctx, e)
            xc = _ab_out(xc, of_c, ob_c, g_c, a_c, mod5, i, ctx_row, hgrn_norm_g[e], w_pool_bd,
                         pool_scale[e], w_out, xc.shape[1])
            xl = _ab_out(xl, of_l, ob_l, g_l, a_l, mod5, i, lat_row, hgrn_norm_g[e], w_pool_bd,
                         pool_scale[e], w_out, GRID_W)
        else:
            o = i // 2
            gm = (gmlp_w_in[o].astype(BF16), gmlp_ln_g[o], gmlp_ln_b[o], gmlp_w_s[o].astype(BF16),
                  gmlp_b_s[o], gmlp_w_out[o].astype(BF16))
            xl = _gmlp(xl, mod5, i, lat_row, norm_g[i, 1], *gm)
            if ctx_live:
                xc = _gmlp(xc, mod5, i, ctx_row, norm_g[i, 1], *gm)

        xl = _ffn(xl, mod5, i, 2, lat_row, norm_g[i, 2], w1[i, 1], w3[i, 1], w2[i, 1], final_g,
                  final_norm=last)
        if ctx_live:
            xc = _ffn(xc, mod5, i, 2, ctx_row, norm_g[i, 2], w1[i, 1], w3[i, 1], w2[i, 1], final_g)
    return xl
```

```python
import functools

import numpy as np
import jax
import jax.numpy as jnp
from jax import lax
from jax.experimental import pallas as pl
from jax.experimental.pallas import tpu as pltpu

F32 = jnp.float32
BF16 = jnp.bfloat16

EPS = 1e-6
N_MOD = 9
GRID_W = 64
POOL_WINDOWS = (2, 4, 8, 16)
HGRN_HEAD_DIM = 128
GMLP_CHUNK = 128
GMLP_GROUP_DIM = 128

LOG2_E = float(np.log2(np.e))
HGRN_DIAG_BLOCK = 32
HGRN_SAFE_LOG2_SPAN = 80.0

VMEM_LIMIT_BYTES = 56 * 1024 * 1024


def _cparams(n_grid_dims):
    return pltpu.CompilerParams(
        dimension_semantics=("arbitrary",) * n_grid_dims,
        vmem_limit_bytes=VMEM_LIMIT_BYTES)


def _sigmoid(x):
    return 1.0 / (1.0 + jnp.exp(-x))


def _silu(x):
    return x * _sigmoid(x)


def _rms(x, g):
    return x * lax.rsqrt(jnp.mean(x * x, axis=-1, keepdims=True) + EPS) * g


def _adaln(x, g, shift, scale):
    return _rms(x, g) * (1.0 + scale) + shift


def _dot(a, b):
    return jnp.dot(a, b, preferred_element_type=F32)


def _dot_nt(a, b):
    return lax.dot_general(a, b, (((1,), (1,)), ((), ())), preferred_element_type=F32)


def _dot_tn(a, b):
    return lax.dot_general(a, b, (((0,), (0,)), ((), ())), preferred_element_type=F32)


def _const_spec(shape):
    nd = len(shape)
    return pl.BlockSpec(shape, lambda *_: (0,) * nd, pipeline_mode=pl.Buffered(1))


def _mod_kernel(c_ref, w_ref, b_ref, o_ref, a_scr):
    @pl.when((pl.program_id(0) == 0) & (pl.program_id(1) == 0))
    def _():
        a_scr[...] = _silu(c_ref[...])

    rows, d, lanes = a_scr.shape
    tn = w_ref.shape[1]
    n_blk = tn // lanes
    sub = 8

    def body(kc, accs):
        k0 = pl.multiple_of(kc * sub, sub)
        w8 = w_ref[pl.ds(k0, sub), :]
        out = []
        for r in range(rows):
            a8 = a_scr[r, pl.ds(k0, sub), :]
            out.append(tuple(accs[r][j] + a8 * w8[:, j * lanes:(j + 1) * lanes] for j in range(n_blk)))
        return tuple(out)

    zero = jnp.zeros((sub, lanes), F32)
    accs = lax.fori_loop(0, d // sub, body, tuple((zero,) * n_blk for _ in range(rows)), unroll=4)
    for r in range(rows):
        row = jnp.concatenate([jnp.sum(a, axis=0, keepdims=True) for a in accs[r]], axis=1)
        o_ref[r:r + 1, :] = row + b_ref[...]


def _modulation(c_rows, mod_w, mod_b, tn=1152):
    depth, d, n = mod_w.shape
    rows = c_rows.shape[0]
    lanes = 128
    c_wide = jnp.broadcast_to(c_rows[:, :, None], (rows, d, lanes))
    return pl.pallas_call(
        _mod_kernel,
        out_shape=jax.ShapeDtypeStruct((depth, rows, n), F32),
        grid=(depth, n // tn),
        in_specs=[
            _const_spec((rows, d, lanes)),
            pl.BlockSpec((None, d, tn), lambda l, j: (l, 0, j)),
            pl.BlockSpec((None, 1, tn), lambda l, j: (l, 0, j)),
        ],
        out_specs=pl.BlockSpec((None, rows, tn), lambda l, j: (l, 0, j)),
        scratch_shapes=[pltpu.VMEM((rows, d, lanes), F32)],
        compiler_params=_cparams(2),
        name="modulation",
    )(c_wide, mod_w, mod_b.reshape(depth, 1, n))


def _mod_index(layer, col, row_of_batch, b, t):
    return (layer, row_of_batch(b), col, 0, 0)


def _mod_block_specs(d_model, layer, sub, row_of_batch):
    return [pl.BlockSpec((None, None, None, 1, d_model),
                         functools.partial(_mod_index, layer, 3 * sub + k, row_of_batch))
            for k in range(3)]


def _ffn_kernel(x_ref, shift_ref, scale_ref, gate_ref, g_ref, w1_ref, w3_ref, w2_ref, fg_ref,
                o_ref, a_ref, *, n_chunk, sub_rows, final_norm):
    d_ff = w1_ref.shape[1]
    for r in range(x_ref.shape[0] // sub_rows):
        rows = slice(r * sub_rows, (r + 1) * sub_rows)
        x = x_ref[rows, :]
        h = _adaln(x, g_ref[...], shift_ref[...], scale_ref[...]).astype(BF16)
        for c in range(d_ff // n_chunk):
            sl = slice(c * n_chunk, (c + 1) * n_chunk)
            u = _dot(h, w1_ref[:, sl])
            v = _dot(h, w3_ref[:, sl])
            a_ref[rows, sl] = (_silu(u) * v).astype(BF16)
        y = _dot(a_ref[rows, :], w2_ref[...])
        out = x + (0.5 * gate_ref[...]) * y
        if final_norm:
            out = _rms(out, fg_ref[...])
        o_ref[rows, :] = out


def _ffn(x, mod5, layer, sub, row_of_batch, norm_g, w1, w3, w2, final_g, *, final_norm=False, tm=1024,
         n_chunk=256, sub_rows=512):
    bsz, length, d = x.shape
    d_ff = w1.shape[1]
    tm = min(tm, length)
    kern = functools.partial(_ffn_kernel, n_chunk=n_chunk, sub_rows=min(sub_rows, tm), final_norm=final_norm)
    return pl.pallas_call(
        kern,
        out_shape=jax.ShapeDtypeStruct(x.shape, F32),
        grid=(bsz, length // tm),
        in_specs=[pl.BlockSpec((None, tm, d), lambda b, t: (b, t, 0))]
        + _mod_block_specs(d, layer, sub, row_of_batch)
        + [_const_spec((1, d)), _const_spec((d, d_ff)), _const_spec((d, d_ff)), _const_spec((d_ff, d)),
           _const_spec((1, d))],
        out_specs=pl.BlockSpec((None, tm, d), lambda b, t: (b, t, 0)),
        scratch_shapes=[pltpu.VMEM((tm, d_ff), BF16)],
        compiler_params=_cparams(2),
        name="swiglu_ffn",
    )(x, mod5, mod5, mod5, norm_g.reshape(1, d), w1, w3, w2, final_g.reshape(1, d))


def _abin_kernel(x_ref, shift_ref, scale_ref, g_ref, w_ref, *o_refs):
    h = _adaln(x_ref[...], g_ref[...], shift_ref[...], scale_ref[...]).astype(BF16)
    off = 0
    for o_ref in o_refs:
        n = o_ref.shape[-1]
        o_ref[...] = _dot(h, w_ref[:, off:off + n]).astype(o_ref.dtype)
        off += n


def _ab_in(x, mod5, layer, row_of_batch, norm_g, w_in_perm, widths, dtypes, *, tm=512):
    bsz, length, d = x.shape
    tm = min(tm, length)
    n_all = w_in_perm.shape[1]
    shift_spec, scale_spec, _ = _mod_block_specs(d, layer, 1, row_of_batch)
    return pl.pallas_call(
        _abin_kernel,
        out_shape=[jax.ShapeDtypeStruct((bsz, length, n), dt) for n, dt in zip(widths, dtypes)],
        grid=(bsz, length // tm),
        in_specs=[pl.BlockSpec((None, tm, d), lambda b, t: (b, t, 0)), shift_spec, scale_spec,
                  _const_spec((1, d)), _const_spec((d, n_all))],
        out_specs=[pl.BlockSpec((None, tm, n), lambda b, t: (b, t, 0)) for n in widths],
        compiler_params=_cparams(2),
        name="ab_in_proj",
    )(x, mod5, mod5, norm_g.reshape(1, d), w_in_perm)


def _level_table(c, diag_block):
    j = np.arange(c)[:, None]
    i = np.arange(c)[None, :]
    x = j ^ i
    bits = np.zeros((c, c), np.int32)
    for k in range(int(np.log2(c))):
        bits = np.where(x >= (1 << k), k + 1, bits)
    lvl = np.where(i < j, bits, 0)
    if diag_block > 1:
        lvl = np.where((i <= j) & (bits <= int(np.log2(diag_block))), int(np.log2(diag_block)), lvl)
    return lvl.astype(np.int32)


def _fine_decay(forget, rev):
    c, w = forget.shape
    f3 = forget.reshape(c // 8, 8, w)
    pos = lax.broadcasted_iota(jnp.int32, (c // 8, 8, w), 1)

    def at(offset):
        return f3 if offset == 0 else pltpu.roll(f3, (-offset) % 8, 1)

    sgn = 1 if rev else -1
    cq = [at(0)]
    ck = [None]
    for t in range(1, 4):
        cq.append(cq[-1] * at(sgn * t))
        ck.append(at(-sgn * t) if ck[-1] is None else ck[-1] * at(-sgn * t))
    out = {}
    for s in (2, 4, 8):
        half = s // 2
        p = pos % s
        e = jnp.ones_like(f3)
        for t in range(half):
            q_pos = (half - 1 - t) if rev else (half + t)
            k_pos = (half + t) if rev else (half - 1 - t)
            e = jnp.where(p == q_pos, cq[t], e)
            if t > 0:
                e = jnp.where(p == k_pos, ck[t], e)
        out[s] = e.reshape(c, w)
    return out


def _block_row(b, s, ridx):
    c, w = b.shape
    r = b.reshape(c // s, s, w)[:, ridx:ridx + 1, :]
    return jnp.broadcast_to(r, (c // s, s, w)).reshape(c, w)


def _hgrn_head(qq, kk, v, forget, b, st, lvl, rev, diag_block):
    c = qq.shape[0]
    b_end = b[0:1] if rev else b[c - 1:c]
    vb = v.astype(BF16)
    o = _dot_nt((qq * jnp.exp2(b)).astype(BF16), st.astype(BF16))
    k_out = (kk * jnp.exp2(b_end - b)).astype(BF16)
    st_new = st * jnp.exp2(b_end) + _dot_tn(vb, k_out)

    row = lax.broadcasted_iota(jnp.int32, (c, 1), 0)
    fine = _fine_decay(forget, rev) if diag_block < 8 else None
    a = jnp.zeros((c, c), F32)
    s = c
    while s > diag_block:
        half = s // 2
        q_side = ((row % s) >= half) != rev
        if s >= 16:
            e = jnp.exp2(-jnp.abs(b - _block_row(b, s, half if rev else half - 1)))
        else:
            e = fine[s]
        x = (jnp.where(q_side, qq, kk) * e).astype(BF16)
        a = jnp.where(lvl == int(np.log2(s)), _dot_nt(x, x), a)
        s //= 2
    if diag_block > 1:
        d = b - _block_row(b, diag_block, diag_block - 1 if rev else 0)
        p = _dot_nt((qq * jnp.exp2(d)).astype(BF16), (kk * jnp.exp2(-d)).astype(BF16))
        a = jnp.where(lvl == int(np.log2(diag_block)), p, a)
        o = o + _dot(a.astype(BF16), vb)
    else:
        o = o + _dot(a.astype(BF16), vb) + jnp.sum(qq * kk, axis=-1, keepdims=True) * v.astype(F32)
    return o, st_new


def _hgrn_kernel(qf_ref, vf_ref, ff_ref, qb_ref, vb_ref, fb_ref, lb_ref, s0_ref, tri_ref, lvl_ref,
                 of_ref, ob_ref, sout_ref, s_scr, qq_s, kk_s, fg_s, b_s, *, slot):
    t = pl.program_id(1)

    @pl.when(t == 0)
    def _():
        s_scr[...] = s0_ref[...]

    raw = lb_ref[...]
    ex = jnp.exp(raw - jnp.max(raw, axis=0, keepdims=True))
    sm = ex / jnp.sum(ex, axis=0, keepdims=True)
    lb_all = sm[0]
    for k in range(1, slot + 1):
        lb_all = lb_all + sm[k]

    c = qf_ref.shape[0]
    n_heads = qf_ref.shape[-1] // HGRN_HEAD_DIM
    q_scale = HGRN_HEAD_DIM ** -0.5
    dirs = ((qf_ref, vf_ref, ff_ref, of_ref), (qb_ref, vb_ref, fb_ref, ob_ref))
    weakest = None
    for d, (q_ref, v_ref, f_ref, o_ref) in enumerate(dirs):
        lb = lb_all[d:d + 1]
        forget = lb + (1.0 - lb) * _sigmoid(f_ref[...])
        logf = jnp.log(forget) * LOG2_E
        hi = logf.astype(BF16)
        lo = (logf - hi.astype(F32)).astype(BF16)
        tri = tri_ref[d]
        b = _dot(tri, hi) + _dot(tri, lo)
        fg_s[d] = forget
        kk_s[d] = 1.0 - forget
        qq_s[d] = _silu(q_ref[...].astype(F32)) * q_scale
        b_s[d] = b
        for m in range(c // HGRN_DIAG_BLOCK):
            lo_row, hi_row = m * HGRN_DIAG_BLOCK, (m + 1) * HGRN_DIAG_BLOCK - 1
            if d == 0:
                span = b[hi_row:hi_row + 1] - b[lo_row:lo_row + 1]
            else:
                span = b[lo_row:lo_row + 1] - b[hi_row:hi_row + 1]
            weakest = span if weakest is None else jnp.minimum(weakest, span)
    in_range = jnp.min(weakest) > -HGRN_SAFE_LOG2_SPAN

    def run(diag_block):
        for d, (q_ref, v_ref, f_ref, o_ref) in enumerate(dirs):
            lvl = lvl_ref[(0 if diag_block > 1 else 2) + d]
            for h in range(n_heads):
                sl = slice(h * HGRN_HEAD_DIM, (h + 1) * HGRN_HEAD_DIM)
                o_h, st_new = _hgrn_head(qq_s[d, :, sl], kk_s[d, :, sl], v_ref[:, sl], fg_s[d, :, sl],
                                         b_s[d, :, sl], s_scr[d, h], lvl, d == 1, diag_block)
                o_ref[:, sl] = o_h.astype(o_ref.dtype)
                s_scr[d, h] = st_new

    @pl.when(in_range)
    def _():
        run(HGRN_DIAG_BLOCK)

    @pl.when(jnp.logical_not(in_range))
    def _():
        run(1)

    @pl.when(t == pl.num_programs(1) - 1)
    def _():
        sout_ref[...] = s_scr[...]


def _hgrn_scan(q, v, f_fwd, f_bwd, hgrn_lb, s0, slot, *, chunk=128):
    bsz, length, dh = q.shape
    c = min(chunk, length)
    n = length // c
    n_heads = dh // HGRN_HEAD_DIM
    lvl_fast = _level_table(c, HGRN_DIAG_BLOCK)
    lvl_full = _level_table(c, 1)
    lvl = jnp.asarray(np.stack([lvl_fast, lvl_fast.T, lvl_full, lvl_full.T]))
    tri_f = np.tril(np.ones((c, c), np.float32))
    tri = jnp.asarray(np.stack([tri_f, tri_f.T]), BF16)
    fwd = lambda b, t: (b, t, 0)
    bwd = lambda b, t: (b, n - 1 - t, 0)
    blk = (None, c, dh)
    s_shape = (2, n_heads, HGRN_HEAD_DIM, HGRN_HEAD_DIM)
    s_spec = pl.BlockSpec((None,) + s_shape, lambda b, t: (b, 0, 0, 0, 0))
    kern = functools.partial(_hgrn_kernel, slot=slot)
    return pl.pallas_call(
        kern,
        out_shape=[jax.ShapeDtypeStruct(q.shape, BF16), jax.ShapeDtypeStruct(q.shape, BF16),
                   jax.ShapeDtypeStruct((bsz,) + s_shape, F32)],
        grid=(bsz, n),
        in_specs=[pl.BlockSpec(blk, fwd), pl.BlockSpec(blk, fwd), pl.BlockSpec(blk, fwd),
                  pl.BlockSpec(blk, bwd), pl.BlockSpec(blk, bwd), pl.BlockSpec(blk, bwd),
                  _const_spec(hgrn_lb.shape), s_spec, _const_spec((2, c, c)), _const_spec((4, c, c))],
        out_specs=[pl.BlockSpec(blk, fwd), pl.BlockSpec(blk, bwd), s_spec],
        scratch_shapes=[pltpu.VMEM(s_shape, F32)] + [pltpu.VMEM((2, c, dh), F32)] * 4,
        compiler_params=_cparams(2),
        name="hgrn2_scan",
    )(q, v, f_fwd, q, v, f_bwd, hgrn_lb, s0, tri, lvl)


def _pool_tables(tm, row_len):
    t = np.arange(tm)
    same_row = (t[:, None] // row_len) == (t[None, :] // row_len)
    pos = t % row_len
    bands, inv = [], []
    for w in POOL_WINDOWS:
        lo = np.clip(pos - w // 2, 0, row_len)
        hi = np.clip(pos - w // 2 + w, 0, row_len)
        u = pos[None, :]
        bands.append(same_row & (u >= lo[:, None]) & (u < hi[:, None]))
        inv.append(1.0 / (hi - lo).astype(np.float64))
    return np.stack(bands).astype(np.float32), np.stack(inv, axis=1).astype(np.float32)


def _about_kernel(x_ref, of_ref, ob_ref, g_ref, a_ref, gate_ref, gn_ref, band_ref, icnt_ref, wp_ref,
                  ps_ref, wo_ref, o_ref):
    o = of_ref[...].astype(F32) + ob_ref[...].astype(F32)
    dh = o.shape[-1]
    heads = []
    for h in range(dh // HGRN_HEAD_DIM):
        oh = o[:, h * HGRN_HEAD_DIM:(h + 1) * HGRN_HEAD_DIM]
        heads.append(oh * lax.rsqrt(jnp.mean(oh * oh, axis=-1, keepdims=True) + EPS))
    b_mix = jnp.concatenate(heads, axis=-1) * gn_ref[...] * _silu(g_ref[...].astype(F32))

    xab = a_ref[...].astype(BF16)
    xa = xab.astype(F32)
    dp = xa.shape[-1]
    n_groups = band_ref.shape[0]
    lane_group = lax.broadcasted_iota(jnp.int32, xa.shape, 1) // (dp // n_groups)
    total = jnp.zeros_like(xa)
    for gi in range(n_groups):
        total = jnp.where(lane_group == gi, _dot(band_ref[gi], xab), total)
    y = total * icnt_ref[...] - xa
    a_mix = _dot(y.astype(BF16), wp_ref[...]) * ps_ref[...]

    mix = _dot(a_mix.astype(BF16), wo_ref[0:dp, :]) + _dot(b_mix.astype(BF16), wo_ref[dp:, :])
    o_ref[...] = x_ref[...] + gate_ref[...] * mix


def _ab_out(x, o_f, o_b, g, xa, mod5, layer, row_of_batch, g_norm, w_pool_bd, pool_scale, w_out, row_len,
            *, tm=256):
    bsz, length, d = x.shape
    dh = o_f.shape[-1]
    dp = xa.shape[-1]
    tm = min(tm, length)
    n_groups = len(POOL_WINDOWS)
    bands, inv = _pool_tables(tm, row_len)
    icnt = jnp.asarray(np.repeat(inv, dp // n_groups, axis=1))
    _, _, gate_spec = _mod_block_specs(d, layer, 1, row_of_batch)
    tok = lambda n: pl.BlockSpec((None, tm, n), lambda b, t: (b, t, 0))
    return pl.pallas_call(
        _about_kernel,
        out_shape=jax.ShapeDtypeStruct(x.shape, F32),
        grid=(bsz, length // tm),
        in_specs=[tok(d), tok(dh), tok(dh), tok(dh), tok(dp), gate_spec, _const_spec((1, dh)),
                  _const_spec((n_groups, tm, tm)), _const_spec((tm, dp)), _const_spec((dp, dp)),
                  _const_spec((1, dp)), _const_spec((dp + dh, d))],
        out_specs=tok(d),
        compiler_params=_cparams(2),
        name="ab_out_proj",
    )(x, o_f, o_b, g, xa, mod5, g_norm.reshape(1, dh), jnp.asarray(bands, BF16), icnt, w_pool_bd,
      pool_scale.reshape(1, dp), w_out)


def _gelu_tanh(x):
    k1 = float(-2.0 * np.sqrt(2.0 / np.pi) * np.log2(np.e))
    k2 = 0.044715 * k1
    return x / (1.0 + jnp.exp2(x * (k1 + k2 * (x * x))))


def _gmlp_kernel(x_ref, shift_ref, scale_ref, gate_ref, g_ref, win_ref, lng_ref, lnb_ref, ws_ref, bs_ref,
                 wout_ref, o_ref, s_ref):
    x = x_ref[...]
    tm = x.shape[0]
    h = _adaln(x, g_ref[...], shift_ref[...], scale_ref[...]).astype(BF16)
    dg = win_ref.shape[1] // 2
    u = _gelu_tanh(_dot(h, win_ref[:, :dg]))
    v = _gelu_tanh(_dot(h, win_ref[:, dg:]))
    vc = v - jnp.mean(v, axis=-1, keepdims=True)
    v = vc * lax.rsqrt(jnp.mean(vc * vc, axis=-1, keepdims=True) + EPS) * lng_ref[...] + lnb_ref[...]
    vb = v.astype(BF16)
    for c in range(tm // GMLP_CHUNK):
        rows = slice(c * GMLP_CHUNK, (c + 1) * GMLP_CHUNK)
        for gi in range(dg // GMLP_GROUP_DIM):
            cols = slice(gi * GMLP_GROUP_DIM, (gi + 1) * GMLP_GROUP_DIM)
            sv = _dot(ws_ref[gi], vb[rows, cols]) + bs_ref[gi]
            s_ref[rows, cols] = (u[rows, cols] * sv).astype(BF16)
    o_ref[...] = x + gate_ref[...] * _dot(s_ref[...], wout_ref[...])


def _gmlp(x, mod5, layer, row_of_batch, norm_g, w_in, ln_g, ln_b, w_s, b_s, w_out, *, tm=512):
    bsz, length, d = x.shape
    dg = w_in.shape[1] // 2
    n_groups = w_s.shape[0]
    tm = min(tm, length)
    b_s_wide = jnp.broadcast_to(b_s[:, :, None], (n_groups, GMLP_CHUNK, GMLP_GROUP_DIM))
    return pl.pallas_call(
        _gmlp_kernel,
        out_shape=jax.ShapeDtypeStruct(x.shape, F32),
        grid=(bsz, length // tm),
        in_specs=[pl.BlockSpec((None, tm, d), lambda b, t: (b, t, 0))]
        + _mod_block_specs(d, layer, 1, row_of_batch)
        + [_const_spec((1, d)), _const_spec((d, 2 * dg)), _const_spec((1, dg)), _const_spec((1, dg)),
           _const_spec((n_groups, GMLP_CHUNK, GMLP_CHUNK)),
           _const_spec((n_groups, GMLP_CHUNK, GMLP_GROUP_DIM)), _const_spec((dg, d))],
        out_specs=pl.BlockSpec((None, tm, d), lambda b, t: (b, t, 0)),
        scratch_shapes=[pltpu.VMEM((tm, dg), BF16)],
        compiler_params=_cparams(2),
        name="gmlp_mixer",
    )(x, mod5, mod5, mod5, norm_g.reshape(1, d), w_in, ln_g.reshape(1, dg), ln_b.reshape(1, dg), w_s,
      b_s_wide, w_out)


def _block_diag(w):
    g, a, b = w.shape
    out = jnp.zeros((g * a, g * b), w.dtype)
    for i in range(g):
        out = out.at[i * a:(i + 1) * a, i * b:(i + 1) * b].set(w[i])
    return out


def kernel(x, c, ctx, c_ctx, mod_w, mod_b, norm_g, ffn_w1, ffn_w3, ffn_w2, ab_w_in, pool_w, pool_scale,
           hgrn_lb, hgrn_norm_g, ab_w_out, gmlp_w_in, gmlp_ln_g, gmlp_ln_b, gmlp_w_s, gmlp_b_s, gmlp_w_out,
           final_g):
    bsz, _, d = x.shape
    depth = mod_w.shape[0]
    d_pool = pool_scale.shape[-1]
    d_hgrn = hgrn_norm_g.shape[-1]
    n_heads = d_hgrn // HGRN_HEAD_DIM

    c_rows = jnp.concatenate([c, c_ctx[None, :]], axis=0)
    mod = _modulation(c_rows, mod_w, mod_b)
    mod5 = mod.reshape(depth, bsz + 1, N_MOD, 1, d)
    lat_row = lambda b: b
    ctx_row = lambda b: bsz

    w1 = ffn_w1.astype(BF16)
    w3 = ffn_w3.astype(BF16)
    w2 = ffn_w2.astype(BF16)

    xl, xc = x, ctx
    for i in range(depth):
        ctx_live = any(j % 2 == 0 for j in range(i, depth))
        last = i == depth - 1

        xl = _ffn(xl, mod5, i, 0, lat_row, norm_g[i, 0], w1[i, 0], w3[i, 0], w2[i, 0], final_g)
        if ctx_live:
            xc = _ffn(xc, mod5, i, 0, ctx_row, norm_g[i, 0], w1[i, 0], w3[i, 0], w2[i, 0], final_g)

        if i % 2 == 0:
            e = i // 2
            w_in = jnp.concatenate([ab_w_in[e][:, d_pool:], ab_w_in[e][:, :d_pool]], axis=1).astype(BF16)
            widths = (d_hgrn,) * 5 + (d_pool,)
            dtypes = (BF16, BF16, F32, F32, BF16, BF16)
            w_pool_bd = _block_diag(pool_w[e]).astype(BF16)
            w_out = ab_w_out[e].astype(BF16)
            q_c, v_c, ff_c, fb_c, g_c, a_c = _ab_in(xc, mod5, i, ctx_row, norm_g[i, 1], w_in, widths, dtypes)
            q_l, v_l, ff_l, fb_l, g_l, a_l = _ab_in(xl, mod5, i, lat_row, norm_g[i, 1], w_in, widths, dtypes)
            s0 = jnp.zeros((bsz, 2, n_heads, HGRN_HEAD_DIM, HGRN_HEAD_DIM), F32)
            of_c, ob_c, s_ctx = _hgrn_scan(q_c, v_c, ff_c, fb_c, hgrn_lb, s0, e)
            of_l, ob_l, _ = _hgrn_scan(q_l, v_l, ff_l, fb_l, hgrn_lb, s_ctx, e)
            xc = _ab_out(xc, of_c, ob_c, g_c, a_c, mod5, i, ctx_row, hgrn_norm_g[e], w_pool_bd,
                         pool_scale[e], w_out, xc.shape[1])
            xl = _ab_out(xl, of_l, ob_l, g_l, a_l, mod5, i, lat_row, hgrn_norm_g[e], w_pool_bd,
                         pool_scale[e], w_out, GRID_W)
        else:
            o = i // 2
            gm = (gmlp_w_in[o].astype(BF16), gmlp_ln_g[o], gmlp_ln_b[o], gmlp_w_s[o].astype(BF16),
                  gmlp_b_s[o], gmlp_w_out[o].astype(BF16))
            xl = _gmlp(xl, mod5, i, lat_row, norm_g[i, 1], *gm)
            if ctx_live:
                xc = _gmlp(xc, mod5, i, ctx_row, norm_g[i, 1], *gm)

        xl = _ffn(xl, mod5, i, 2, lat_row, norm_g[i, 2], w1[i, 1], w3[i, 1], w2[i, 1], final_g,
                  final_norm=last)
        if ctx_live:
            xc = _ffn(xc, mod5, i, 2, ctx_row, norm_g[i, 2], w1[i, 1], w3[i, 1], w2[i, 1], final_g)
    return xl
```

```python
import functools

import numpy as np
import jax
import jax.numpy as jnp
from jax import lax
from jax.experimental import pallas as pl
from jax.experimental.pallas import tpu as pltpu

F32 = jnp.float32
BF16 = jnp.bfloat16

EPS = 1e-6
N_MOD = 9
GRID_W = 64
POOL_WINDOWS = (2, 4, 8, 16)
HGRN_HEAD_DIM = 128
GMLP_CHUNK = 128
GMLP_GROUP_DIM = 128

LOG2_E = float(np.log2(np.e))
HGRN_DIAG_BLOCK = 32
HGRN_SAFE_LOG2_SPAN = 80.0

VMEM_LIMIT_BYTES = 56 * 1024 * 1024


def _cparams(n_grid_dims):
    return pltpu.CompilerParams(
        dimension_semantics=("arbitrary",) * n_grid_dims,
        vmem_limit_bytes=VMEM_LIMIT_BYTES)


def _sigmoid(x):
    return 1.0 / (1.0 + jnp.exp(-x))


def _silu(x):
    return x * _sigmoid(x)


def _rms(x, g):
    return x * lax.rsqrt(jnp.mean(x * x, axis=-1, keepdims=True) + EPS) * g


def _adaln(x, g, shift, scale):
    return _rms(x, g) * (1.0 + scale) + shift


def _dot(a, b):
    return jnp.dot(a, b, preferred_element_type=F32)


def _dot_nt(a, b):
    return lax.dot_general(a, b, (((1,), (1,)), ((), ())), preferred_element_type=F32)


def _dot_tn(a, b):
    return lax.dot_general(a, b, (((0,), (0,)), ((), ())), preferred_element_type=F32)


def _const_spec(shape):
    nd = len(shape)
    return pl.BlockSpec(shape, lambda *_: (0,) * nd, pipeline_mode=pl.Buffered(1))


def _mod_kernel(c_ref, w_ref, b_ref, o_ref, a_scr):
    @pl.when((pl.program_id(0) == 0) & (pl.program_id(1) == 0))
    def _():
        a_scr[...] = _silu(c_ref[...])

    rows, d, lanes = a_scr.shape
    tn = w_ref.shape[1]
    n_blk = tn // lanes
    sub = 8

    def body(kc, accs):
        k0 = pl.multiple_of(kc * sub, sub)
        w8 = w_ref[pl.ds(k0, sub), :]
        out = []
        for r in range(rows):
            a8 = a_scr[r, pl.ds(k0, sub), :]
            out.append(tuple(accs[r][j] + a8 * w8[:, j * lanes:(j + 1) * lanes] for j in range(n_blk)))
        return tuple(out)

    zero = jnp.zeros((sub, lanes), F32)
    accs = lax.fori_loop(0, d // sub, body, tuple((zero,) * n_blk for _ in range(rows)), unroll=4)
    for r in range(rows):
        row = jnp.concatenate([jnp.sum(a, axis=0, keepdims=True) for a in accs[r]], axis=1)
        o_ref[r:r + 1, :] = row + b_ref[...]


def _modulation(c_rows, mod_w, mod_b, tn=1152):
    depth, d, n = mod_w.shape
    rows = c_rows.shape[0]
    lanes = 128
    c_wide = jnp.broadcast_to(c_rows[:, :, None], (rows, d, lanes))
    return pl.pallas_call(
        _mod_kernel,
        out_shape=jax.ShapeDtypeStruct((depth, rows, n), F32),
        grid=(depth, n // tn),
        in_specs=[
            _const_spec((rows, d, lanes)),
            pl.BlockSpec((None, d, tn), lambda l, j: (l, 0, j)),
            pl.BlockSpec((None, 1, tn), lambda l, j: (l, 0, j)),
        ],
        out_specs=pl.BlockSpec((None, rows, tn), lambda l, j: (l, 0, j)),
        scratch_shapes=[pltpu.VMEM((rows, d, lanes), F32)],
        compiler_params=_cparams(2),
        name="modulation",
    )(c_wide, mod_w, mod_b.reshape(depth, 1, n))


def _mod_index(layer, col, row_of_batch, b, t):
    return (layer, row_of_batch(b), col, 0, 0)


def _mod_block_specs(d_model, layer, sub, row_of_batch):
    return [pl.BlockSpec((None, None, None, 1, d_model),
                         functools.partial(_mod_index, layer, 3 * sub + k, row_of_batch))
            for k in range(3)]


def _ffn_kernel(x_ref, shift_ref, scale_ref, gate_ref, g_ref, w1_ref, w3_ref, w2_ref, fg_ref,
                o_ref, a_ref, *, n_chunk, sub_rows, final_norm):
    d_ff = w1_ref.shape[1]
    for r in range(x_ref.shape[0] // sub_rows):
        rows = slice(r * sub_rows, (r + 1) * sub_rows)
        x = x_ref[rows, :]
        h = _adaln(x, g_ref[...], shift_ref[...], scale_ref[...]).astype(BF16)
        for c in range(d_ff // n_chunk):
            sl = slice(c * n_chunk, (c + 1) * n_chunk)
            u = _dot(h, w1_ref[:, sl])
            v = _dot(h, w3_ref[:, sl])
            a_ref[rows, sl] = (_silu(u) * v).astype(BF16)
        y = _dot(a_ref[rows, :], w2_ref[...])
        out = x + (0.5 * gate_ref[...]) * y
        if final_norm:
            out = _rms(out, fg_ref[...])
        o_ref[rows, :] = out


def _ffn(x, mod5, layer, sub, row_of_batch, norm_g, w1, w3, w2, final_g, *, final_norm=False, tm=1024,
         n_chunk=256, sub_rows=512):
    bsz, length, d = x.shape
    d_ff = w1.shape[1]
    tm = min(tm, length)
    kern = functools.partial(_ffn_kernel, n_chunk=n_chunk, sub_rows=min(sub_rows, tm), final_norm=final_norm)
    return pl.pallas_call(
        kern,
        out_shape=jax.ShapeDtypeStruct(x.shape, F32),
        grid=(bsz, length // tm),
        in_specs=[pl.BlockSpec((None, tm, d), lambda b, t: (b, t, 0))]
        + _mod_block_specs(d, layer, sub, row_of_batch)
        + [_const_spec((1, d)), _const_spec((d, d_ff)), _const_spec((d, d_ff)), _const_spec((d_ff, d)),
           _const_spec((1, d))],
        out_specs=pl.BlockSpec((None, tm, d), lambda b, t: (b, t, 0)),
        scratch_shapes=[pltpu.VMEM((tm, d_ff), BF16)],
        compiler_params=_cparams(2),
        name="swiglu_ffn",
    )(x, mod5, mod5, mod5, norm_g.reshape(1, d), w1, w3, w2, final_g.reshape(1, d))


def _abin_kernel(x_ref, shift_ref, scale_ref, g_ref, w_ref, *o_refs):
    h = _adaln(x_ref[...], g_ref[...], shift_ref[...], scale_ref[...]).astype(BF16)
    off = 0
    for o_ref in o_refs:
        n = o_ref.shape[-1]
        o_ref[...] = _dot(h, w_ref[:, off:off + n]).astype(o_ref.dtype)
        off += n


def _ab_in(x, mod5, layer, row_of_batch, norm_g, w_in_perm, widths, dtypes, *, tm=512):
    bsz, length, d = x.shape
    tm = min(tm, length)
    n_all = w_in_perm.shape[1]
    shift_spec, scale_spec, _ = _mod_block_specs(d, layer, 1, row_of_batch)
    return pl.pallas_call(
        _abin_kernel,
        out_shape=[jax.ShapeDtypeStruct((bsz, length, n), dt) for n, dt in zip(widths, dtypes)],
        grid=(bsz, length // tm),
        in_specs=[pl.BlockSpec((None, tm, d), lambda b, t: (b, t, 0)), shift_spec, scale_spec,
                  _const_spec((1, d)), _const_spec((d, n_all))],
        out_specs=[pl.BlockSpec((None, tm, n), lambda b, t: (b, t, 0)) for n in widths],
        compiler_params=_cparams(2),
        name="ab_in_proj",
    )(x, mod5, mod5, norm_g.reshape(1, d), w_in_perm)


def _level_table(c, diag_block):
    j = np.arange(c)[:, None]
    i = np.arange(c)[None, :]
    x = j ^ i
    bits = np.zeros((c, c), np.int32)
    for k in range(int(np.log2(c))):
        bits = np.where(x >= (1 << k), k + 1, bits)
    lvl = np.where(i < j, bits, 0)
    if diag_block > 1:
        lvl = np.where((i <= j) & (bits <= int(np.log2(diag_block))), int(np.log2(diag_block)), lvl)
    return lvl.astype(np.int32)


def _fine_decay(forget, rev):
    c, w = forget.shape
    f3 = forget.reshape(c // 8, 8, w)
    pos = lax.broadcasted_iota(jnp.int32, (c // 8, 8, w), 1)

    def at(offset):
        return f3 if offset == 0 else pltpu.roll(f3, (-offset) % 8, 1)

    sgn = 1 if rev else -1
    cq = [at(0)]
    ck = [None]
    for t in range(1, 4):
        cq.append(cq[-1] * at(sgn * t))
        ck.append(at(-sgn * t) if ck[-1] is None else ck[-1] * at(-sgn * t))
    out = {}
    for s in (2, 4, 8):
        half = s // 2
        p = pos % s
        e = jnp.ones_like(f3)
        for t in range(half):
            q_pos = (half - 1 - t) if rev else (half + t)
            k_pos = (half + t) if rev else (half - 1 - t)
            e = jnp.where(p == q_pos, cq[t], e)
            if t > 0:
                e = jnp.where(p == k_pos, ck[t], e)
        out[s] = e.reshape(c, w)
    return out


def _block_row(b, s, ridx):
    c, w = b.shape
    r = b.reshape(c // s, s, w)[:, ridx:ridx + 1, :]
    return jnp.broadcast_to(r, (c // s, s, w)).reshape(c, w)


def _hgrn_head(qq, kk, v, forget, b, st, lvl, rev, diag_block):
    c = qq.shape[0]
    b_end = b[0:1] if rev else b[c - 1:c]
    vb = v.astype(BF16)
    o = _dot((qq * jnp.exp2(b)).astype(BF16), st.astype(BF16))
    k_out = (kk * jnp.exp2(b_end - b)).astype(BF16)
    decay = jnp.broadcast_to(jnp.exp2(b_end), st.shape).T
    st_new = st * decay + _dot_tn(k_out, vb)

    row = lax.broadcasted_iota(jnp.int32, (c, 1), 0)
    fine = _fine_decay(forget, rev) if diag_block < 8 else None
    a = jnp.zeros((c, c), F32)
    s = c
    while s > diag_block:
        half = s // 2
        q_side = ((row % s) >= half) != rev
        if s >= 16:
            e = jnp.exp2(-jnp.abs(b - _block_row(b, s, half if rev else half - 1)))
        else:
            e = fine[s]
        x = jnp.where(q_side, qq, kk) * e
        a = jnp.where(lvl == int(np.log2(s)), _dot(x.astype(BF16), x.T.astype(BF16)), a)
        s //= 2
    if diag_block > 1:
        d = b - _block_row(b, diag_block, diag_block - 1 if rev else 0)
        p = _dot((qq * jnp.exp2(d)).astype(BF16), (kk * jnp.exp2(-d)).T.astype(BF16))
        a = jnp.where(lvl == int(np.log2(diag_block)), p, a)
        o = o + _dot(a.astype(BF16), vb)
    else:
        o = o + _dot(a.astype(BF16), vb) + jnp.sum(qq * kk, axis=-1, keepdims=True) * v.astype(F32)
    return o, st_new


def _hgrn_mild_direction(rev, qq_ref, kk_ref, b_ref, v_ref, o_ref, st_ref, lvl, ops_ref, a_ref):
    c = qq_ref.shape[0]
    n_heads = st_ref.shape[0]
    row = lax.broadcasted_iota(jnp.int32, (c, 1), 0)
    sizes = []
    s = c
    while s > HGRN_DIAG_BLOCK:
        sizes.append(s)
        s //= 2

    for h in range(n_heads):
        sl = slice(h * HGRN_HEAD_DIM, (h + 1) * HGRN_HEAD_DIM)
        qq, kk, b = qq_ref[:, sl], kk_ref[:, sl], b_ref[:, sl]
        b_end = b[0:1] if rev else b[c - 1:c]
        st = st_ref[h]
        ops_ref[h, 0] = (qq * jnp.exp2(b)).astype(BF16)
        ops_ref[h, 1] = (kk * jnp.exp2(b_end - b)).astype(BF16)
        ops_ref[h, 2] = st.astype(BF16)
        st_ref[h] = st * jnp.broadcast_to(jnp.exp2(b_end), st.shape).T
        for n, s in enumerate(sizes):
            half = s // 2
            q_side = ((row % s) >= half) != rev
            e = jnp.exp2(-jnp.abs(b - _block_row(b, s, half if rev else half - 1)))
            x = jnp.where(q_side, qq, kk) * e
            ops_ref[h, 3 + 2 * n] = x.astype(BF16)
            ops_ref[h, 4 + 2 * n] = x.T.astype(BF16)
        d = b - _block_row(b, HGRN_DIAG_BLOCK, HGRN_DIAG_BLOCK - 1 if rev else 0)
        ops_ref[h, 3 + 2 * len(sizes)] = (qq * jnp.exp2(d)).astype(BF16)
        ops_ref[h, 4 + 2 * len(sizes)] = (kk * jnp.exp2(-d)).T.astype(BF16)

    for h in range(n_heads):
        a = jnp.zeros((c, c), F32)
        for n, s in enumerate(sizes + [HGRN_DIAG_BLOCK]):
            a = jnp.where(lvl == int(np.log2(s)), _dot(ops_ref[h, 3 + 2 * n], ops_ref[h, 4 + 2 * n]), a)
        a_ref[h] = a.astype(BF16)

    for h in range(n_heads):
        sl = slice(h * HGRN_HEAD_DIM, (h + 1) * HGRN_HEAD_DIM)
        vb = v_ref[:, sl].astype(BF16)
        o = _dot(ops_ref[h, 0], ops_ref[h, 2]) + _dot(a_ref[h], vb)
        o_ref[:, sl] = o.astype(o_ref.dtype)
        st_ref[h] = st_ref[h] + _dot_tn(ops_ref[h, 1], vb)


def _hgrn_kernel(qf_ref, vf_ref, ff_ref, qb_ref, vb_ref, fb_ref, lb_ref, s0_ref, tri_ref, lvl_ref,
                 of_ref, ob_ref, sout_ref, s_scr, qq_s, kk_s, fg_s, b_s, ops_s, a_s, *, slot):
    t = pl.program_id(1)

    @pl.when(t == 0)
    def _():
        s_scr[...] = s0_ref[...]

    raw = lb_ref[...]
    ex = jnp.exp(raw - jnp.max(raw, axis=0, keepdims=True))
    sm = ex / jnp.sum(ex, axis=0, keepdims=True)
    lb_all = sm[0]
    for k in range(1, slot + 1):
        lb_all = lb_all + sm[k]

    c = qf_ref.shape[0]
    n_heads = qf_ref.shape[-1] // HGRN_HEAD_DIM
    q_scale = HGRN_HEAD_DIM ** -0.5
    dirs = ((qf_ref, vf_ref, ff_ref, of_ref), (qb_ref, vb_ref, fb_ref, ob_ref))
    weakest = None
    for d, (q_ref, v_ref, f_ref, o_ref) in enumerate(dirs):
        lb = lb_all[d:d + 1]
        forget = lb + (1.0 - lb) * _sigmoid(f_ref[...])
        logf = jnp.log(forget) * LOG2_E
        hi = logf.astype(BF16)
        lo = (logf - hi.astype(F32)).astype(BF16)
        tri = tri_ref[d]
        b = _dot(tri, hi) + _dot(tri, lo)
        fg_s[d] = forget
        kk_s[d] = 1.0 - forget
        qq_s[d] = _silu(q_ref[...].astype(F32)) * q_scale
        b_s[d] = b
        for m in range(c // HGRN_DIAG_BLOCK):
            lo_row, hi_row = m * HGRN_DIAG_BLOCK, (m + 1) * HGRN_DIAG_BLOCK - 1
            if d == 0:
                span = b[hi_row:hi_row + 1] - b[lo_row:lo_row + 1]
            else:
                span = b[lo_row:lo_row + 1] - b[hi_row:hi_row + 1]
            weakest = span if weakest is None else jnp.minimum(weakest, span)
    in_range = jnp.min(weakest) > -HGRN_SAFE_LOG2_SPAN

    def run(diag_block):
        for d, (q_ref, v_ref, f_ref, o_ref) in enumerate(dirs):
            lvl = lvl_ref[(0 if diag_block > 1 else 2) + d]
            for h in range(n_heads):
                sl = slice(h * HGRN_HEAD_DIM, (h + 1) * HGRN_HEAD_DIM)
                o_h, st_new = _hgrn_head(qq_s[d, :, sl], kk_s[d, :, sl], v_ref[:, sl], fg_s[d, :, sl],
                                         b_s[d, :, sl], s_scr[d, h], lvl, d == 1, diag_block)
                o_ref[:, sl] = o_h.astype(o_ref.dtype)
                s_scr[d, h] = st_new

    @pl.when(in_range)
    def _():
        for d, (q_ref, v_ref, f_ref, o_ref) in enumerate(dirs):
            _hgrn_mild_direction(d == 1, qq_s.at[d], kk_s.at[d], b_s.at[d], v_ref, o_ref, s_scr.at[d],
                                 lvl_ref[d], ops_s.at[d], a_s.at[d])

    @pl.when(jnp.logical_not(in_range))
    def _():
        run(1)

    @pl.when(t == pl.num_programs(1) - 1)
    def _():
        sout_ref[...] = s_scr[...]


def _hgrn_scan(q, v, f_fwd, f_bwd, hgrn_lb, s0, slot, *, chunk=128):
    bsz, length, dh = q.shape
    c = min(chunk, length)
    n = length // c
    n_heads = dh // HGRN_HEAD_DIM
    assert c == HGRN_HEAD_DIM and c > HGRN_DIAG_BLOCK
    n_mild = int(np.log2(c // HGRN_DIAG_BLOCK)) + 1
    lvl_fast = _level_table(c, HGRN_DIAG_BLOCK)
    lvl_full = _level_table(c, 1)
    lvl = jnp.asarray(np.stack([lvl_fast, lvl_fast.T, lvl_full, lvl_full.T]))
    tri_f = np.tril(np.ones((c, c), np.float32))
    tri = jnp.asarray(np.stack([tri_f, tri_f.T]), BF16)
    fwd = lambda b, t: (b, t, 0)
    bwd = lambda b, t: (b, n - 1 - t, 0)
    blk = (None, c, dh)
    s_shape = (2, n_heads, HGRN_HEAD_DIM, HGRN_HEAD_DIM)
    s_spec = pl.BlockSpec((None,) + s_shape, lambda b, t: (b, 0, 0, 0, 0))
    kern = functools.partial(_hgrn_kernel, slot=slot)
    return pl.pallas_call(
        kern,
        out_shape=[jax.ShapeDtypeStruct(q.shape, BF16), jax.ShapeDtypeStruct(q.shape, BF16),
                   jax.ShapeDtypeStruct((bsz,) + s_shape, F32)],
        grid=(bsz, n),
        in_specs=[pl.BlockSpec(blk, fwd), pl.BlockSpec(blk, fwd), pl.BlockSpec(blk, fwd),
                  pl.BlockSpec(blk, bwd), pl.BlockSpec(blk, bwd), pl.BlockSpec(blk, bwd),
                  _const_spec(hgrn_lb.shape), s_spec, _const_spec((2, c, c)), _const_spec((4, c, c))],
        out_specs=[pl.BlockSpec(blk, fwd), pl.BlockSpec(blk, bwd), s_spec],
        scratch_shapes=[pltpu.VMEM(s_shape, F32)] + [pltpu.VMEM((2, c, dh), F32)] * 4
        + [pltpu.VMEM((2, n_heads, 3 + 2 * n_mild, c, HGRN_HEAD_DIM), BF16),
           pltpu.VMEM((2, n_heads, c, c), BF16)],
        compiler_params=_cparams(2),
        name="hgrn2_scan",
    )(q, v, f_fwd, q, v, f_bwd, hgrn_lb, s0, tri, lvl)


def _pool_tables(tm, row_len):
    t = np.arange(tm)
    same_row = (t[:, None] // row_len) == (t[None, :] // row_len)
    pos = t % row_len
    bands, inv = [], []
    for w in POOL_WINDOWS:
        lo = np.clip(pos - w // 2, 0, row_len)
        hi = np.clip(pos - w // 2 + w, 0, row_len)
        u = pos[None, :]
        bands.append(same_row & (u >= lo[:, None]) & (u < hi[:, None]))
        inv.append(1.0 / (hi - lo).astype(np.float64))
    return np.stack(bands).astype(np.float32), np.stack(inv, axis=1).astype(np.float32)


def _about_kernel(x_ref, of_ref, ob_ref, g_ref, a_ref, gate_ref, gn_ref, band_ref, icnt_ref, wp_ref,
                  ps_ref, wo_ref, o_ref):
    o = of_ref[...].astype(F32) + ob_ref[...].astype(F32)
    dh = o.shape[-1]
    heads = []
    for h in range(dh // HGRN_HEAD_DIM):
        oh = o[:, h * HGRN_HEAD_DIM:(h + 1) * HGRN_HEAD_DIM]
        heads.append(oh * lax.rsqrt(jnp.mean(oh * oh, axis=-1, keepdims=True) + EPS))
    b_mix = jnp.concatenate(heads, axis=-1) * gn_ref[...] * _silu(g_ref[...].astype(F32))

    xab = a_ref[...].astype(BF16)
    xa = xab.astype(F32)
    dp = xa.shape[-1]
    n_groups = band_ref.shape[0]
    lane_group = lax.broadcasted_iota(jnp.int32, xa.shape, 1) // (dp // n_groups)
    total = jnp.zeros_like(xa)
    for gi in range(n_groups):
        total = jnp.where(lane_group == gi, _dot(band_ref[gi], xab), total)
    y = total * icnt_ref[...] - xa
    a_mix = _dot(y.astype(BF16), wp_ref[...]) * ps_ref[...]

    mix = _dot(a_mix.astype(BF16), wo_ref[0:dp, :]) + _dot(b_mix.astype(BF16), wo_ref[dp:, :])
    o_ref[...] = x_ref[...] + gate_ref[...] * mix


def _ab_out(x, o_f, o_b, g, xa, mod5, layer, row_of_batch, g_norm, w_pool_bd, pool_scale, w_out, row_len,
            *, tm=256):
    bsz, length, d = x.shape
    dh = o_f.shape[-1]
    dp = xa.shape[-1]
    tm = min(tm, length)
    n_groups = len(POOL_WINDOWS)
    bands, inv = _pool_tables(tm, row_len)
    icnt = jnp.asarray(np.repeat(inv, dp // n_groups, axis=1))
    _, _, gate_spec = _mod_block_specs(d, layer, 1, row_of_batch)
    tok = lambda n: pl.BlockSpec((None, tm, n), lambda b, t: (b, t, 0))
    return pl.pallas_call(
        _about_kernel,
        out_shape=jax.ShapeDtypeStruct(x.shape, F32),
        grid=(bsz, length // tm),
        in_specs=[tok(d), tok(dh), tok(dh), tok(dh), tok(dp), gate_spec, _const_spec((1, dh)),
                  _const_spec((n_groups, tm, tm)), _const_spec((tm, dp)), _const_spec((dp, dp)),
                  _const_spec((1, dp)), _const_spec((dp + dh, d))],
        out_specs=tok(d),
        compiler_params=_cparams(2),
        name="ab_out_proj",
    )(x, o_f, o_b, g, xa, mod5, g_norm.reshape(1, dh), jnp.asarray(bands, BF16), icnt, w_pool_bd,
      pool_scale.reshape(1, dp), w_out)


def _gelu_tanh(x):
    k1 = float(-2.0 * np.sqrt(2.0 / np.pi) * np.log2(np.e))
    k2 = 0.044715 * k1
    return x / (1.0 + jnp.exp2(x * (k1 + k2 * (x * x))))


def _gmlp_kernel(x_ref, shift_ref, scale_ref, gate_ref, g_ref, win_ref, lng_ref, lnb_ref, ws_ref, bs_ref,
                 wout_ref, o_ref, s_ref):
    x = x_ref[...]
    tm = x.shape[0]
    h = _adaln(x, g_ref[...], shift_ref[...], scale_ref[...]).astype(BF16)
    dg = win_ref.shape[1] // 2
    u = _gelu_tanh(_dot(h, win_ref[:, :dg]))
    v = _gelu_tanh(_dot(h, win_ref[:, dg:]))
    vc = v - jnp.mean(v, axis=-1, keepdims=True)
    v = vc * lax.rsqrt(jnp.mean(vc * vc, axis=-1, keepdims=True) + EPS) * lng_ref[...] + lnb_ref[...]
    vb = v.astype(BF16)
    for c in range(tm // GMLP_CHUNK):
        rows = slice(c * GMLP_CHUNK, (c + 1) * GMLP_CHUNK)
        for gi in range(dg // GMLP_GROUP_DIM):
            cols = slice(gi * GMLP_GROUP_DIM, (gi + 1) * GMLP_GROUP_DIM)
            sv = _dot(ws_ref[gi], vb[rows, cols]) + bs_ref[gi]
            s_ref[rows, cols] = (u[rows, cols] * sv).astype(BF16)
    o_ref[...] = x + gate_ref[...] * _dot(s_ref[...], wout_ref[...])


def _gmlp(x, mod5, layer, row_of_batch, norm_g, w_in, ln_g, ln_b, w_s, b_s, w_out, *, tm=512):
    bsz, length, d = x.shape
    dg = w_in.shape[1] // 2
    n_groups = w_s.shape[0]
    tm = min(tm, length)
    b_s_wide = jnp.broadcast_to(b_s[:, :, None], (n_groups, GMLP_CHUNK, GMLP_GROUP_DIM))
    return pl.pallas_call(
        _gmlp_kernel,
        out_shape=jax.ShapeDtypeStruct(x.shape, F32),
        grid=(bsz, length // tm),
        in_specs=[pl.BlockSpec((None, tm, d), lambda b, t: (b, t, 0))]
        + _mod_block_specs(d, layer, 1, row_of_batch)
        + [_const_spec((1, d)), _const_spec((d, 2 * dg)), _const_spec((1, dg)), _const_spec((1, dg)),
           _const_spec((n_groups, GMLP_CHUNK, GMLP_CHUNK)),
           _const_spec((n_groups, GMLP_CHUNK, GMLP_GROUP_DIM)), _const_spec((dg, d))],
        out_specs=pl.BlockSpec((None, tm, d), lambda b, t: (b, t, 0)),
        scratch_shapes=[pltpu.VMEM((tm, dg), BF16)],
        compiler_params=_cparams(2),
        name="gmlp_mixer",
    )(x, mod5, mod5, mod5, norm_g.reshape(1, d), w_in, ln_g.reshape(1, dg), ln_b.reshape(1, dg), w_s,
      b_s_wide, w_out)


def _block_diag(w):
    g, a, b = w.shape
    out = jnp.zeros((g * a, g * b), w.dtype)
    for i in range(g):
        out = out.at[i * a:(i + 1) * a, i * b:(i + 1) * b].set(w[i])
    return out


def kernel(x, c, ctx, c_ctx, mod_w, mod_b, norm_g, ffn_w1, ffn_w3, ffn_w2, ab_w_in, pool_w, pool_scale,
           hgrn_lb, hgrn_norm_g, ab_w_out, gmlp_w_in, gmlp_ln_g, gmlp_ln_b, gmlp_w_s, gmlp_b_s, gmlp_w_out,
           final_g):
    bsz, _, d = x.shape
    depth = mod_w.shape[0]
    d_pool = pool_scale.shape[-1]
    d_hgrn = hgrn_norm_g.shape[-1]
    n_heads = d_hgrn // HGRN_HEAD_DIM

    c_rows = jnp.concatenate([c, c_ctx[None, :]], axis=0)
    mod = _modulation(c_rows, mod_w, mod_b)
    mod5 = mod.reshape(depth, bsz + 1, N_MOD, 1, d)
    lat_row = lambda b: b
    ctx_row = lambda b: bsz

    w1 = ffn_w1.astype(BF16)
    w3 = ffn_w3.astype(BF16)
    w2 = ffn_w2.astype(BF16)

    xl, xc = x, ctx
    for i in range(depth):
        ctx_live = any(j % 2 == 0 for j in range(i, depth))
        last = i == depth - 1

        xl = _ffn(xl, mod5, i, 0, lat_row, norm_g[i, 0], w1[i, 0], w3[i, 0], w2[i, 0], final_g)
        if ctx_live:
            xc = _ffn(xc, mod5, i, 0, ctx_row, norm_g[i, 0], w1[i, 0], w3[i, 0], w2[i, 0], final_g)

        if i % 2 == 0:
            e = i // 2
            w_in = jnp.concatenate([ab_w_in[e][:, d_pool:], ab_w_in[e][:, :d_pool]], axis=1).astype(BF16)
            widths = (d_hgrn,) * 5 + (d_pool,)
            dtypes = (BF16, BF16, F32, F32, BF16, BF16)
            w_pool_bd = _block_diag(pool_w[e]).astype(BF16)
            w_out = ab_w_out[e].astype(BF16)
            q_c, v_c, ff_c, fb_c, g_c, a_c = _ab_in(xc, mod5, i, ctx_row, norm_g[i, 1], w_in, widths, dtypes)
            q_l, v_l, ff_l, fb_l, g_l, a_l = _ab_in(xl, mod5, i, lat_row, norm_g[i, 1], w_in, widths, dtypes)
            s0 = jnp.zeros((bsz, 2, n_heads, HGRN_HEAD_DIM, HGRN_HEAD_DIM), F32)
            of_c, ob_c, s_ctx = _hgrn_scan(q_c, v_c, ff_c, fb_c, hgrn_lb, s0, e)
            of_l, ob_l, _ = _hgrn_scan(q_l, v_l, ff_l, fb_l, hgrn_lb, s_ctx, e)
            xc = _ab_out(xc, of_c, ob_c, g_c, a_c, mod5, i, ctx_row, hgrn_norm_g[e], w_pool_bd,
                         pool_scale[e], w_out, xc.shape[1])
            xl = _ab_out(xl, of_l, ob_l, g_l, a_l, mod5, i, lat_row, hgrn_norm_g[e], w_pool_bd,
                         pool_scale[e], w_out, GRID_W)
        else:
            o = i // 2
            gm = (gmlp_w_in[o].astype(BF16), gmlp_ln_g[o], gmlp_ln_b[o], gmlp_w_s[o].astype(BF16),
                  gmlp_b_s[o], gmlp_w_out[o].astype(BF16))
            xl = _gmlp(xl, mod5, i, lat_row, norm_g[i, 1], *gm)
            if ctx_live:
                xc = _gmlp(xc, mod5, i, ctx_row, norm_g[i, 1], *gm)

        xl = _ffn(xl, mod5, i, 2, lat_row, norm_g[i, 2], w1[i, 1], w3[i, 1], w2[i, 1], final_g,
                  final_norm=last)
        if ctx_live:
            xc = _ffn(xc, mod5, i, 2, ctx_row, norm_g[i, 2], w1[i, 1], w3[i, 1], w2[i, 1], final_g)
    return xl
```

```python
import functools

import numpy as np
import jax
import jax.numpy as jnp
from jax import lax
from jax.experimental import pallas as pl
from jax.experimental.pallas import tpu as pltpu

F32 = jnp.float32
BF16 = jnp.bfloat16

EPS = 1e-6
N_MOD = 9
GRID_W = 64
POOL_WINDOWS = (2, 4, 8, 16)
HGRN_HEAD_DIM = 128
GMLP_CHUNK = 128
GMLP_GROUP_DIM = 128

LOG2_E = float(np.log2(np.e))
HGRN_CHUNK = 128
HGRN_DIAG_BLOCK = 32
HGRN_SAFE_LOG2_SPAN = 80.0

VMEM_LIMIT_BYTES = 56 * 1024 * 1024


def _cparams(n_grid_dims):
    return pltpu.CompilerParams(
        dimension_semantics=("arbitrary",) * n_grid_dims,
        vmem_limit_bytes=VMEM_LIMIT_BYTES)


def _sigmoid(x):
    return 1.0 / (1.0 + jnp.exp2(x * (-LOG2_E)))


def _silu(x):
    return x * _sigmoid(x)


def _rms(x, g):
    return x * lax.rsqrt(jnp.mean(x * x, axis=-1, keepdims=True) + EPS) * g


def _adaln(x, g, shift, scale):
    return _rms(x, g) * (1.0 + scale) + shift


def _dot(a, b):
    return jnp.dot(a, b, preferred_element_type=F32)


def _dot_tn(a, b):
    return lax.dot_general(a, b, (((0,), (0,)), ((), ())), preferred_element_type=F32)


def _const_spec(shape):
    nd = len(shape)
    return pl.BlockSpec(shape, lambda *_: (0,) * nd, pipeline_mode=pl.Buffered(1))


def _pick_spec(full_shape, lead):
    tail = tuple(full_shape[len(lead):])
    idx = tuple(lead) + (0,) * len(tail)
    return pl.BlockSpec((None,) * len(lead) + tail, lambda *_: idx, pipeline_mode=pl.Buffered(1))


def _tok_spec(tm, n):
    return pl.BlockSpec((None, tm, n), lambda b, t: (b, t, 0))


def _mod_kernel(c_ref, w_ref, b_ref, o_ref, a_scr):
    @pl.when((pl.program_id(0) == 0) & (pl.program_id(1) == 0))
    def _():
        a_scr[...] = _silu(c_ref[...])

    rows, d, lanes = a_scr.shape
    tn = w_ref.shape[1]
    n_blk = tn // lanes
    sub = 8

    def body(kc, accs):
        k0 = pl.multiple_of(kc * sub, sub)
        w8 = w_ref[pl.ds(k0, sub), :]
        out = []
        for r in range(rows):
            a8 = a_scr[r, pl.ds(k0, sub), :]
            out.append(tuple(accs[r][j] + a8 * w8[:, j * lanes:(j + 1) * lanes] for j in range(n_blk)))
        return tuple(out)

    zero = jnp.zeros((sub, lanes), F32)
    accs = lax.fori_loop(0, d // sub, body, tuple((zero,) * n_blk for _ in range(rows)), unroll=4)
    for r in range(rows):
        row = jnp.concatenate([jnp.sum(a, axis=0, keepdims=True) for a in accs[r]], axis=1)
        o_ref[r:r + 1, :] = row + b_ref[...]


def _modulation(c_rows, mod_w, mod_b, tn=1152):
    depth, d, n = mod_w.shape
    rows = c_rows.shape[0]
    lanes = 128
    c_wide = jnp.broadcast_to(c_rows[:, :, None], (rows, d, lanes))
    return pl.pallas_call(
        _mod_kernel,
        out_shape=jax.ShapeDtypeStruct((depth, rows, n), F32),
        grid=(depth, n // tn),
        in_specs=[
            _const_spec((rows, d, lanes)),
            pl.BlockSpec((None, d, tn), lambda l, j: (l, 0, j)),
            pl.BlockSpec((None, 1, tn), lambda l, j: (l, 0, j)),
        ],
        out_specs=pl.BlockSpec((None, rows, tn), lambda l, j: (l, 0, j)),
        scratch_shapes=[pltpu.VMEM((rows, d, lanes), F32)],
        compiler_params=_cparams(2),
        name="modulation",
    )(c_wide, mod_w, mod_b.reshape(depth, 1, n))


def _mod_index(layer, col, row_of_batch, b, t):
    return (layer, row_of_batch(b), col, 0, 0)


def _mod_block_specs(d_model, layer, sub, row_of_batch):
    return [pl.BlockSpec((None, None, None, 1, d_model),
                         functools.partial(_mod_index, layer, 3 * sub + k, row_of_batch))
            for k in range(3)]


def _ffn_kernel(x_ref, shift_ref, scale_ref, gate_ref, g_ref, w1_ref, w3_ref, w2_ref, fg_ref,
                o_ref, a_ref, *, n_chunk, sub_rows, final_norm):
    d_ff = w1_ref.shape[1]
    for r in range(x_ref.shape[0] // sub_rows):
        rows = slice(r * sub_rows, (r + 1) * sub_rows)
        x = x_ref[rows, :]
        h = _adaln(x, g_ref[...], shift_ref[...], scale_ref[...]).astype(BF16)
        for c in range(d_ff // n_chunk):
            sl = slice(c * n_chunk, (c + 1) * n_chunk)
            u = _dot(h, w1_ref[:, sl])
            v = _dot(h, w3_ref[:, sl])
            a_ref[rows, sl] = (_silu(u) * v).astype(BF16)
        y = _dot(a_ref[rows, :], w2_ref[...])
        out = x + (0.5 * gate_ref[...]) * y
        if final_norm:
            out = _rms(out, fg_ref[...])
        o_ref[rows, :] = out


def _ffn(x, mod5, layer, sub, which, row_of_batch, norm_g4, w1, w3, w2, final_g, *, final_norm=False,
         tm=1024, n_chunk=256, sub_rows=512):
    bsz, length, d = x.shape
    d_ff = w1.shape[-1]
    tm = min(tm, length)
    kern = functools.partial(_ffn_kernel, n_chunk=n_chunk, sub_rows=min(sub_rows, tm), final_norm=final_norm)
    return pl.pallas_call(
        kern,
        out_shape=jax.ShapeDtypeStruct(x.shape, F32),
        grid=(bsz, length // tm),
        in_specs=[_tok_spec(tm, d)]
        + _mod_block_specs(d, layer, sub, row_of_batch)
        + [_pick_spec(norm_g4.shape, (layer, sub)), _pick_spec(w1.shape, (layer, which)),
           _pick_spec(w3.shape, (layer, which)), _pick_spec(w2.shape, (layer, which)), _const_spec((1, d))],
        out_specs=_tok_spec(tm, d),
        scratch_shapes=[pltpu.VMEM((tm, d_ff), BF16)],
        compiler_params=_cparams(2),
        name="swiglu_ffn",
    )(x, mod5, mod5, mod5, norm_g4, w1, w3, w2, final_g.reshape(1, d))


def _abin_kernel(x_ref, shift_ref, scale_ref, g_ref, w_ref, *o_refs, col_offsets, sub_rows):
    for r in range(x_ref.shape[0] // sub_rows):
        rows = slice(r * sub_rows, (r + 1) * sub_rows)
        h = _adaln(x_ref[rows, :], g_ref[...], shift_ref[...], scale_ref[...]).astype(BF16)
        for o_ref, off in zip(o_refs, col_offsets):
            o_ref[rows, :] = _dot(h, w_ref[:, off:off + o_ref.shape[-1]]).astype(o_ref.dtype)


def _ab_in(x, mod5, layer, row_of_batch, norm_g4, w_in, e, col_offsets, widths, dtypes, *, tm=1024,
           sub_rows=512):
    bsz, length, d = x.shape
    tm = min(tm, length)
    shift_spec, scale_spec, _ = _mod_block_specs(d, layer, 1, row_of_batch)
    kern = functools.partial(_abin_kernel, col_offsets=col_offsets, sub_rows=min(sub_rows, tm))
    return pl.pallas_call(
        kern,
        out_shape=[jax.ShapeDtypeStruct((bsz, length, n), dt) for n, dt in zip(widths, dtypes)],
        grid=(bsz, length // tm),
        in_specs=[_tok_spec(tm, d), shift_spec, scale_spec, _pick_spec(norm_g4.shape, (layer, 1)),
                  _pick_spec(w_in.shape, (e,))],
        out_specs=[_tok_spec(tm, n) for n in widths],
        compiler_params=_cparams(2),
        name="ab_in_proj",
    )(x, mod5, mod5, norm_g4, w_in)


def _level_table(c, diag_block):
    j = np.arange(c)[:, None]
    i = np.arange(c)[None, :]
    x = j ^ i
    bits = np.zeros((c, c), np.int32)
    for k in range(int(np.log2(c))):
        bits = np.where(x >= (1 << k), k + 1, bits)
    lvl = np.where(i < j, bits, 0)
    if diag_block > 1:
        lvl = np.where((i <= j) & (bits <= int(np.log2(diag_block))), int(np.log2(diag_block)), lvl)
    return lvl.astype(np.int32)


def _fine_decay(forget, rev):
    c, w = forget.shape
    f3 = forget.reshape(c // 8, 8, w)
    pos = lax.broadcasted_iota(jnp.int32, (c // 8, 8, w), 1)

    def at(offset):
        return f3 if offset == 0 else pltpu.roll(f3, (-offset) % 8, 1)

    sgn = 1 if rev else -1
    cq = [at(0)]
    ck = [None]
    for t in range(1, 4):
        cq.append(cq[-1] * at(sgn * t))
        ck.append(at(-sgn * t) if ck[-1] is None else ck[-1] * at(-sgn * t))
    out = {}
    for s in (2, 4, 8):
        half = s // 2
        p = pos % s
        e = jnp.ones_like(f3)
        for t in range(half):
            q_pos = (half - 1 - t) if rev else (half + t)
            k_pos = (half + t) if rev else (half - 1 - t)
            e = jnp.where(p == q_pos, cq[t], e)
            if t > 0:
                e = jnp.where(p == k_pos, ck[t], e)
        out[s] = e.reshape(c, w)
    return out


def _block_row(b, s, ridx):
    c, w = b.shape
    r = b.reshape(c // s, s, w)[:, ridx:ridx + 1, :]
    return jnp.broadcast_to(r, (c // s, s, w)).reshape(c, w)


def _key_decay_column(b_end, shape):
    return jnp.broadcast_to(jnp.exp2(b_end), shape).T


def _hgrn_head(qq, kk, v, forget, b, st, lvl, rev):
    c = qq.shape[0]
    b_end = b[0:1] if rev else b[c - 1:c]
    vb = v.astype(BF16)
    o = _dot((qq * jnp.exp2(b)).astype(BF16), st.astype(BF16))
    k_out = (kk * jnp.exp2(b_end - b)).astype(BF16)
    st_new = st * _key_decay_column(b_end, st.shape) + _dot_tn(k_out, vb)

    row = lax.broadcasted_iota(jnp.int32, (c, 1), 0)
    fine = _fine_decay(forget, rev)
    a = jnp.zeros((c, c), F32)
    s = c
    while s > 1:
        half = s // 2
        q_side = ((row % s) >= half) != rev
        if s >= 16:
            e = jnp.exp2(-jnp.abs(b - _block_row(b, s, half if rev else half - 1)))
        else:
            e = fine[s]
        x = jnp.where(q_side, qq, kk) * e
        a = jnp.where(lvl == int(np.log2(s)), _dot(x.astype(BF16), x.T.astype(BF16)), a)
        s //= 2
    o = o + _dot(a.astype(BF16), vb) + jnp.sum(qq * kk, axis=-1, keepdims=True) * v.astype(F32)
    return o, st_new


def _hgrn_mild_direction(rev, qq_ref, kk_ref, b_ref, v_ref, o_ref, st_ref, lvl, ops_ref, a_ref):
    c = qq_ref.shape[0]
    n_heads = st_ref.shape[0]
    row = lax.broadcasted_iota(jnp.int32, (c, 1), 0)
    sizes = []
    s = c
    while s > HGRN_DIAG_BLOCK:
        sizes.append(s)
        s //= 2

    for h in range(n_heads):
        sl = slice(h * HGRN_HEAD_DIM, (h + 1) * HGRN_HEAD_DIM)
        qq, kk, b = qq_ref[:, sl], kk_ref[:, sl], b_ref[:, sl]
        b_end = b[0:1] if rev else b[c - 1:c]
        st = st_ref[h]
        ops_ref[h, 0] = (qq * jnp.exp2(b)).astype(BF16)
        ops_ref[h, 1] = (kk * jnp.exp2(b_end - b)).astype(BF16)
        ops_ref[h, 2] = st.astype(BF16)
        st_ref[h] = st * _key_decay_column(b_end, st.shape)
        for n, s in enumerate(sizes):
            half = s // 2
            q_side = ((row % s) >= half) != rev
            e = jnp.exp2(-jnp.abs(b - _block_row(b, s, half if rev else half - 1)))
            x = jnp.where(q_side, qq, kk) * e
            ops_ref[h, 3 + 2 * n] = x.astype(BF16)
            ops_ref[h, 4 + 2 * n] = x.T.astype(BF16)
        d = b - _block_row(b, HGRN_DIAG_BLOCK, HGRN_DIAG_BLOCK - 1 if rev else 0)
        ops_ref[h, 3 + 2 * len(sizes)] = (qq * jnp.exp2(d)).astype(BF16)
        ops_ref[h, 4 + 2 * len(sizes)] = (kk * jnp.exp2(-d)).T.astype(BF16)

    for h in range(n_heads):
        a = jnp.zeros((c, c), F32)
        for n, s in enumerate(sizes + [HGRN_DIAG_BLOCK]):
            a = jnp.where(lvl == int(np.log2(s)), _dot(ops_ref[h, 3 + 2 * n], ops_ref[h, 4 + 2 * n]), a)
        a_ref[h] = a.astype(BF16)

    for h in range(n_heads):
        sl = slice(h * HGRN_HEAD_DIM, (h + 1) * HGRN_HEAD_DIM)
        vb = v_ref[:, sl].astype(BF16)
        o = _dot(ops_ref[h, 0], ops_ref[h, 2]) + _dot(a_ref[h], vb)
        o_ref[:, sl] = o.astype(o_ref.dtype)
        st_ref[h] = st_ref[h] + _dot_tn(ops_ref[h, 1], vb)


def _hgrn_kernel(qf_ref, vf_ref, ff_ref, qb_ref, vb_ref, fb_ref, lb_ref, s0_ref, tri_ref, lvl_ref,
                 of_ref, ob_ref, sout_ref, s_scr, qq_s, kk_s, fg_s, b_s, ops_s, a_s, *, slot):
    t = pl.program_id(1)

    @pl.when(t == 0)
    def _():
        s_scr[...] = s0_ref[...]

    raw = lb_ref[...]
    ex = jnp.exp(raw - jnp.max(raw, axis=0, keepdims=True))
    sm = ex / jnp.sum(ex, axis=0, keepdims=True)
    lb_all = sm[0]
    for k in range(1, slot + 1):
        lb_all = lb_all + sm[k]

    c = HGRN_CHUNK
    n_sub = qf_ref.shape[0] // c
    n_heads = qf_ref.shape[-1] // HGRN_HEAD_DIM
    q_scale = HGRN_HEAD_DIM ** -0.5

    def chunk(i, carry):
        offs = (pl.multiple_of(i * c, c), pl.multiple_of((n_sub - 1 - i) * c, c))
        dirs = tuple((q_ref.at[pl.ds(off, c), :], v_ref.at[pl.ds(off, c), :], f_ref.at[pl.ds(off, c), :],
                      o_ref.at[pl.ds(off, c), :])
                     for off, (q_ref, v_ref, f_ref, o_ref)
                     in zip(offs, ((qf_ref, vf_ref, ff_ref, of_ref), (qb_ref, vb_ref, fb_ref, ob_ref))))
        weakest = None
        for d, (q_ref, v_ref, f_ref, o_ref) in enumerate(dirs):
            lb = lb_all[d:d + 1]
            forget = lb + (1.0 - lb) * _sigmoid(f_ref[...])
            logf = jnp.log(forget) * LOG2_E
            hi = logf.astype(BF16)
            lo = (logf - hi.astype(F32)).astype(BF16)
            tri = tri_ref[d]
            b = _dot(tri, hi) + _dot(tri, lo)
            fg_s[d] = forget
            kk_s[d] = 1.0 - forget
            qq_s[d] = _silu(q_ref[...].astype(F32)) * q_scale
            b_s[d] = b
            for m in range(c // HGRN_DIAG_BLOCK):
                lo_row, hi_row = m * HGRN_DIAG_BLOCK, (m + 1) * HGRN_DIAG_BLOCK - 1
                if d == 0:
                    span = b[hi_row:hi_row + 1] - b[lo_row:lo_row + 1]
                else:
                    span = b[lo_row:lo_row + 1] - b[hi_row:hi_row + 1]
                weakest = span if weakest is None else jnp.minimum(weakest, span)
        in_range = jnp.min(weakest) > -HGRN_SAFE_LOG2_SPAN

        @pl.when(in_range)
        def _():
            for d, (q_ref, v_ref, f_ref, o_ref) in enumerate(dirs):
                _hgrn_mild_direction(d == 1, qq_s.at[d], kk_s.at[d], b_s.at[d], v_ref, o_ref, s_scr.at[d],
                                     lvl_ref[d], ops_s.at[d], a_s.at[d])

        @pl.when(jnp.logical_not(in_range))
        def _():
            for d, (q_ref, v_ref, f_ref, o_ref) in enumerate(dirs):
                lvl = lvl_ref[2 + d]
                for h in range(n_heads):
                    sl = slice(h * HGRN_HEAD_DIM, (h + 1) * HGRN_HEAD_DIM)
                    o_h, st_new = _hgrn_head(qq_s[d, :, sl], kk_s[d, :, sl], v_ref[:, sl], fg_s[d, :, sl],
                                             b_s[d, :, sl], s_scr[d, h], lvl, d == 1)
                    o_ref[:, sl] = o_h.astype(o_ref.dtype)
                    s_scr[d, h] = st_new

        return carry

    lax.fori_loop(0, n_sub, chunk, 0)

    @pl.when(t == pl.num_programs(1) - 1)
    def _():
        sout_ref[...] = s_scr[...]


def _hgrn_scan(q, v, f_fwd, f_bwd, hgrn_lb, s0, slot, *, rows=512):
    bsz, length, dh = q.shape
    c = HGRN_CHUNK
    rows = min(rows, length)
    n = length // rows
    n_heads = dh // HGRN_HEAD_DIM
    assert c == HGRN_HEAD_DIM and c > HGRN_DIAG_BLOCK
    n_mild = int(np.log2(c // HGRN_DIAG_BLOCK)) + 1
    lvl_fast = _level_table(c, HGRN_DIAG_BLOCK)
    lvl_full = _level_table(c, 1)
    lvl = jnp.asarray(np.stack([lvl_fast, lvl_fast.T, lvl_full, lvl_full.T]))
    tri_f = np.tril(np.ones((c, c), np.float32))
    tri = jnp.asarray(np.stack([tri_f, tri_f.T]), BF16)
    fwd = lambda b, t: (b, t, 0)
    bwd = lambda b, t: (b, n - 1 - t, 0)
    blk = (None, rows, dh)
    s_shape = (2, n_heads, HGRN_HEAD_DIM, HGRN_HEAD_DIM)
    s_spec = pl.BlockSpec((None,) + s_shape, lambda b, t: (b, 0, 0, 0, 0))
    kern = functools.partial(_hgrn_kernel, slot=slot)
    return pl.pallas_call(
        kern,
        out_shape=[jax.ShapeDtypeStruct(q.shape, BF16), jax.ShapeDtypeStruct(q.shape, BF16),
                   jax.ShapeDtypeStruct((bsz,) + s_shape, F32)],
        grid=(bsz, n),
        in_specs=[pl.BlockSpec(blk, fwd), pl.BlockSpec(blk, fwd), pl.BlockSpec(blk, fwd),
                  pl.BlockSpec(blk, bwd), pl.BlockSpec(blk, bwd), pl.BlockSpec(blk, bwd),
                  _const_spec(hgrn_lb.shape), s_spec, _const_spec((2, c, c)), _const_spec((4, c, c))],
        out_specs=[pl.BlockSpec(blk, fwd), pl.BlockSpec(blk, bwd), s_spec],
        scratch_shapes=[pltpu.VMEM(s_shape, F32)] + [pltpu.VMEM((2, c, dh), F32)] * 4
        + [pltpu.VMEM((2, n_heads, 3 + 2 * n_mild, c, HGRN_HEAD_DIM), BF16),
           pltpu.VMEM((2, n_heads, c, c), BF16)],
        compiler_params=_cparams(2),
        name="hgrn2_scan",
    )(q, v, f_fwd, q, v, f_bwd, hgrn_lb, s0, tri, lvl)


def _pool_tables(tm, row_len):
    t = np.arange(tm)
    same_row = (t[:, None] // row_len) == (t[None, :] // row_len)
    pos = t % row_len
    bands, inv = [], []
    for w in POOL_WINDOWS:
        lo = np.clip(pos - w // 2, 0, row_len)
        hi = np.clip(pos - w // 2 + w, 0, row_len)
        u = pos[None, :]
        bands.append(same_row & (u >= lo[:, None]) & (u < hi[:, None]))
        inv.append(1.0 / (hi - lo).astype(np.float64))
    return np.stack(bands).astype(np.float32), np.stack(inv, axis=1).astype(np.float32)


def _about_kernel(x_ref, of_ref, ob_ref, g_ref, a_ref, gate_ref, gn_ref, band_ref, icnt_ref, wp_ref,
                  ps_ref, wo_ref, o_ref):
    sub_rows = band_ref.shape[-1]
    dp = a_ref.shape[-1]
    n_groups = band_ref.shape[0]
    for r in range(x_ref.shape[0] // sub_rows):
        rows = slice(r * sub_rows, (r + 1) * sub_rows)
        o = of_ref[rows, :].astype(F32) + ob_ref[rows, :].astype(F32)
        heads = []
        for h in range(o.shape[-1] // HGRN_HEAD_DIM):
            oh = o[:, h * HGRN_HEAD_DIM:(h + 1) * HGRN_HEAD_DIM]
            heads.append(oh * lax.rsqrt(jnp.mean(oh * oh, axis=-1, keepdims=True) + EPS))
        b_mix = jnp.concatenate(heads, axis=-1) * gn_ref[...] * _silu(g_ref[rows, :].astype(F32))

        xab = a_ref[rows, :].astype(BF16)
        xa = xab.astype(F32)
        lane_group = lax.broadcasted_iota(jnp.int32, xa.shape, 1) // (dp // n_groups)
        total = jnp.zeros_like(xa)
        for gi in range(n_groups):
            total = jnp.where(lane_group == gi, _dot(band_ref[gi], xab), total)
        y = total * icnt_ref[...] - xa
        a_mix = _dot(y.astype(BF16), wp_ref[...]) * ps_ref[...]

        mix = _dot(a_mix.astype(BF16), wo_ref[0:dp, :]) + _dot(b_mix.astype(BF16), wo_ref[dp:, :])
        o_ref[rows, :] = x_ref[rows, :] + gate_ref[...] * mix


def _ab_out(x, o_f, o_b, g, xa, mod5, layer, row_of_batch, g_norm, e, w_pool_bd, pool_scale, w_out, row_len,
            *, tm=512, sub_rows=256):
    bsz, length, d = x.shape
    dh = o_f.shape[-1]
    dp = xa.shape[-1]
    tm = min(tm, length)
    sub_rows = min(sub_rows, tm)
    n_groups = len(POOL_WINDOWS)
    bands, inv = _pool_tables(sub_rows, row_len)
    icnt = jnp.asarray(np.repeat(inv, dp // n_groups, axis=1))
    _, _, gate_spec = _mod_block_specs(d, layer, 1, row_of_batch)
    return pl.pallas_call(
        _about_kernel,
        out_shape=jax.ShapeDtypeStruct(x.shape, F32),
        grid=(bsz, length // tm),
        in_specs=[_tok_spec(tm, d), _tok_spec(tm, dh), _tok_spec(tm, dh), _tok_spec(tm, dh), _tok_spec(tm, dp),
                  gate_spec, _pick_spec(g_norm.shape, (e,)), _const_spec((n_groups, sub_rows, sub_rows)),
                  _const_spec((sub_rows, dp)), _const_spec((dp, dp)), _pick_spec(pool_scale.shape, (e,)),
                  _pick_spec(w_out.shape, (e,))],
        out_specs=_tok_spec(tm, d),
        compiler_params=_cparams(2),
        name="ab_out_proj",
    )(x, o_f, o_b, g, xa, mod5, g_norm, jnp.asarray(bands, BF16), icnt, w_pool_bd, pool_scale, w_out)


def _gelu_tanh(x):
    k1 = float(-2.0 * np.sqrt(2.0 / np.pi) * np.log2(np.e))
    k2 = 0.044715 * k1
    return x / (1.0 + jnp.exp2(x * (k1 + k2 * (x * x))))


def _gmlp_kernel(x_ref, shift_ref, scale_ref, gate_ref, g_ref, win_ref, lng_ref, lnb_ref, ws_ref, bs_ref,
                 wout_ref, o_ref, s_ref, *, sub_rows):
    dg = win_ref.shape[1] // 2
    for r in range(x_ref.shape[0] // sub_rows):
        rows = slice(r * sub_rows, (r + 1) * sub_rows)
        x = x_ref[rows, :]
        h = _adaln(x, g_ref[...], shift_ref[...], scale_ref[...]).astype(BF16)
        u = _gelu_tanh(_dot(h, win_ref[:, :dg]))
        v = _gelu_tanh(_dot(h, win_ref[:, dg:]))
        vc = v - jnp.mean(v, axis=-1, keepdims=True)
        v = vc * lax.rsqrt(jnp.mean(vc * vc, axis=-1, keepdims=True) + EPS) * lng_ref[...] + lnb_ref[...]
        vb = v.astype(BF16)
        for c in range(sub_rows // GMLP_CHUNK):
            crows = slice(c * GMLP_CHUNK, (c + 1) * GMLP_CHUNK)
            srows = slice(r * sub_rows + c * GMLP_CHUNK, r * sub_rows + (c + 1) * GMLP_CHUNK)
            for gi in range(dg // GMLP_GROUP_DIM):
                cols = slice(gi * GMLP_GROUP_DIM, (gi + 1) * GMLP_GROUP_DIM)
                sv = _dot(ws_ref[gi], vb[crows, cols]) + bs_ref[gi]
                s_ref[srows, cols] = (u[crows, cols] * sv).astype(BF16)
        o_ref[rows, :] = x + gate_ref[...] * _dot(s_ref[rows, :], wout_ref[...])


def _gmlp(x, mod5, layer, row_of_batch, norm_g4, o, w_in, ln_g, ln_b, w_s, b_s_wide, w_out, *, tm=1024,
          sub_rows=512):
    bsz, length, d = x.shape
    dg = w_in.shape[-1] // 2
    tm = min(tm, length)
    kern = functools.partial(_gmlp_kernel, sub_rows=min(sub_rows, tm))
    return pl.pallas_call(
        kern,
        out_shape=jax.ShapeDtypeStruct(x.shape, F32),
        grid=(bsz, length // tm),
        in_specs=[_tok_spec(tm, d)]
        + _mod_block_specs(d, layer, 1, row_of_batch)
        + [_pick_spec(norm_g4.shape, (layer, 1)), _pick_spec(w_in.shape, (o,)), _pick_spec(ln_g.shape, (o,)),
           _pick_spec(ln_b.shape, (o,)), _pick_spec(w_s.shape, (o,)), _pick_spec(b_s_wide.shape, (o,)),
           _pick_spec(w_out.shape, (o,))],
        out_specs=_tok_spec(tm, d),
        scratch_shapes=[pltpu.VMEM((tm, dg), BF16)],
        compiler_params=_cparams(2),
        name="gmlp_mixer",
    )(x, mod5, mod5, mod5, norm_g4, w_in, ln_g, ln_b, w_s, b_s_wide, w_out)


def _block_diag(w):
    n, g, a, b = w.shape
    eye = jnp.eye(g, dtype=w.dtype)
    return (w[:, :, :, None, :] * eye[None, :, None, :, None]).reshape(n, g * a, g * b)


def kernel(x, c, ctx, c_ctx, mod_w, mod_b, norm_g, ffn_w1, ffn_w3, ffn_w2, ab_w_in, pool_w, pool_scale,
           hgrn_lb, hgrn_norm_g, ab_w_out, gmlp_w_in, gmlp_ln_g, gmlp_ln_b, gmlp_w_s, gmlp_b_s, gmlp_w_out,
           final_g):
    bsz, _, d = x.shape
    depth = mod_w.shape[0]
    d_pool = pool_scale.shape[-1]
    d_hgrn = hgrn_norm_g.shape[-1]
    n_heads = d_hgrn // HGRN_HEAD_DIM

    c_rows = jnp.concatenate([c, c_ctx[None, :]], axis=0)
    mod = _modulation(c_rows, mod_w, mod_b)
    mod5 = mod.reshape(depth, bsz + 1, N_MOD, 1, d)
    lat_row = lambda b: b
    ctx_row = lambda b: bsz

    norm_g4 = norm_g[:, :, None, :]
    w1, w3, w2 = ffn_w1.astype(BF16), ffn_w3.astype(BF16), ffn_w2.astype(BF16)
    ab_in_w, ab_out_w = ab_w_in.astype(BF16), ab_w_out.astype(BF16)
    pool_bd = _block_diag(pool_w).astype(BF16)
    pool_scale3, hgrn_norm_g3 = pool_scale[:, None, :], hgrn_norm_g[:, None, :]
    gm_in, gm_s, gm_out = gmlp_w_in.astype(BF16), gmlp_w_s.astype(BF16), gmlp_w_out.astype(BF16)
    gm_ln_g, gm_ln_b = gmlp_ln_g[:, None, :], gmlp_ln_b[:, None, :]
    gm_b_wide = jnp.broadcast_to(gmlp_b_s[:, :, :, None], gmlp_b_s.shape + (GMLP_GROUP_DIM,))

    ab_offsets = tuple(d_pool + k * d_hgrn for k in range(5)) + (0,)
    ab_widths = (d_hgrn,) * 5 + (d_pool,)
    ab_dtypes = (BF16, BF16, F32, F32, BF16, BF16)

    xl, xc = x, ctx
    for i in range(depth):
        ctx_live = any(j % 2 == 0 for j in range(i, depth))
        last = i == depth - 1

        xl = _ffn(xl, mod5, i, 0, 0, lat_row, norm_g4, w1, w3, w2, final_g)
        if ctx_live:
            xc = _ffn(xc, mod5, i, 0, 0, ctx_row, norm_g4, w1, w3, w2, final_g)

        if i % 2 == 0:
            e = i // 2
            ab_in = functools.partial(_ab_in, mod5=mod5, layer=i, norm_g4=norm_g4, w_in=ab_in_w, e=e,
                                      col_offsets=ab_offsets, widths=ab_widths, dtypes=ab_dtypes)
            q_c, v_c, ff_c, fb_c, g_c, a_c = ab_in(xc, row_of_batch=ctx_row)
            q_l, v_l, ff_l, fb_l, g_l, a_l = ab_in(xl, row_of_batch=lat_row)
            s0 = jnp.zeros((bsz, 2, n_heads, HGRN_HEAD_DIM, HGRN_HEAD_DIM), F32)
            of_c, ob_c, s_ctx = _hgrn_scan(q_c, v_c, ff_c, fb_c, hgrn_lb, s0, e)
            of_l, ob_l, _ = _hgrn_scan(q_l, v_l, ff_l, fb_l, hgrn_lb, s_ctx, e)
            ab_out = functools.partial(_ab_out, mod5=mod5, layer=i, g_norm=hgrn_norm_g3, e=e,
                                       w_pool_bd=pool_bd[e], pool_scale=pool_scale3, w_out=ab_out_w)
            xc = ab_out(xc, of_c, ob_c, g_c, a_c, row_of_batch=ctx_row, row_len=xc.shape[1])
            xl = ab_out(xl, of_l, ob_l, g_l, a_l, row_of_batch=lat_row, row_len=GRID_W)
        else:
            o = i // 2
            gmlp = functools.partial(_gmlp, mod5=mod5, layer=i, norm_g4=norm_g4, o=o, w_in=gm_in, ln_g=gm_ln_g,
                                     ln_b=gm_ln_b, w_s=gm_s, b_s_wide=gm_b_wide, w_out=gm_out)
            xl = gmlp(xl, row_of_batch=lat_row)
            if ctx_live:
                xc = gmlp(xc, row_of_batch=ctx_row)

        xl = _ffn(xl, mod5, i, 2, 1, lat_row, norm_g4, w1, w3, w2, final_g, final_norm=last)
        if ctx_live:
            xc = _ffn(xc, mod5, i, 2, 1, ctx_row, norm_g4, w1, w3, w2, final_g)
    return xl
```

```python
import functools

import numpy as np
import jax
import jax.numpy as jnp
from jax import lax
from jax.experimental import pallas as pl
from jax.experimental.pallas import tpu as pltpu

F32 = jnp.float32
BF16 = jnp.bfloat16

EPS = 1e-6
N_MOD = 9
GRID_W = 64
POOL_WINDOWS = (2, 4, 8, 16)
HGRN_HEAD_DIM = 128
GMLP_CHUNK = 128
GMLP_GROUP_DIM = 128

LOG2_E = float(np.log2(np.e))
HGRN_CHUNK = 128
HGRN_DIAG_BLOCK = 32
HGRN_SAFE_LOG2_SPAN = 80.0

VMEM_LIMIT_BYTES = 56 * 1024 * 1024


def _cparams(n_grid_dims):
    return pltpu.CompilerParams(
        dimension_semantics=("arbitrary",) * n_grid_dims,
        vmem_limit_bytes=VMEM_LIMIT_BYTES)


def _sigmoid(x):
    return 1.0 / (1.0 + jnp.exp2(x * (-LOG2_E)))


def _silu(x):
    return x * _sigmoid(x)


def _rms(x, g):
    return x * lax.rsqrt(jnp.mean(x * x, axis=-1, keepdims=True) + EPS) * g


def _adaln(x, g, shift, scale):
    return _rms(x, g) * (1.0 + scale) + shift


def _dot(a, b):
    return jnp.dot(a, b, preferred_element_type=F32)


def _dot_tn(a, b):
    return lax.dot_general(a, b, (((0,), (0,)), ((), ())), preferred_element_type=F32)


def _const_spec(shape):
    nd = len(shape)
    return pl.BlockSpec(shape, lambda *_: (0,) * nd, pipeline_mode=pl.Buffered(1))


def _pick_spec(full_shape, lead):
    tail = tuple(full_shape[len(lead):])
    idx = tuple(lead) + (0,) * len(tail)
    return pl.BlockSpec((None,) * len(lead) + tail, lambda *_: idx, pipeline_mode=pl.Buffered(1))


def _tok_spec(tm, n):
    return pl.BlockSpec((None, tm, n), lambda b, t: (b, t, 0))


def _mod_kernel(c_ref, w_ref, b_ref, o_ref, a_scr):
    @pl.when((pl.program_id(0) == 0) & (pl.program_id(1) == 0))
    def _():
        a_scr[...] = _silu(c_ref[...])

    rows, d, lanes = a_scr.shape
    tn = w_ref.shape[1]
    n_blk = tn // lanes
    sub = 8

    def body(kc, accs):
        k0 = pl.multiple_of(kc * sub, sub)
        w8 = w_ref[pl.ds(k0, sub), :]
        out = []
        for r in range(rows):
            a8 = a_scr[r, pl.ds(k0, sub), :]
            out.append(tuple(accs[r][j] + a8 * w8[:, j * lanes:(j + 1) * lanes] for j in range(n_blk)))
        return tuple(out)

    zero = jnp.zeros((sub, lanes), F32)
    accs = lax.fori_loop(0, d // sub, body, tuple((zero,) * n_blk for _ in range(rows)), unroll=4)
    for r in range(rows):
        row = jnp.concatenate([jnp.sum(a, axis=0, keepdims=True) for a in accs[r]], axis=1)
        o_ref[r:r + 1, :] = row + b_ref[...]


def _modulation(c_rows, mod_w, mod_b, tn=1152):
    depth, d, n = mod_w.shape
    rows = c_rows.shape[0]
    lanes = 128
    c_wide = jnp.broadcast_to(c_rows[:, :, None], (rows, d, lanes))
    return pl.pallas_call(
        _mod_kernel,
        out_shape=jax.ShapeDtypeStruct((depth, rows, n), F32),
        grid=(depth, n // tn),
        in_specs=[
            _const_spec((rows, d, lanes)),
            pl.BlockSpec((None, d, tn), lambda l, j: (l, 0, j)),
            pl.BlockSpec((None, 1, tn), lambda l, j: (l, 0, j)),
        ],
        out_specs=pl.BlockSpec((None, rows, tn), lambda l, j: (l, 0, j)),
        scratch_shapes=[pltpu.VMEM((rows, d, lanes), F32)],
        compiler_params=_cparams(2),
        name="modulation",
    )(c_wide, mod_w, mod_b.reshape(depth, 1, n))


def _mod_index(layer, col, row_of_batch, b, t):
    return (layer, row_of_batch(b), col, 0, 0)


def _mod_block_specs(d_model, layer, sub, row_of_batch):
    return [pl.BlockSpec((None, None, None, 1, d_model),
                         functools.partial(_mod_index, layer, 3 * sub + k, row_of_batch))
            for k in range(3)]


def _ffn_kernel(x_ref, shift_ref, scale_ref, gate_ref, g_ref, w1_ref, w3_ref, w2_ref, fg_ref, *rest,
                n_cast, n_chunk, sub_rows, final_norm):
    cast_in, o_ref, cast_out, a_ref = rest[:n_cast], rest[n_cast], rest[n_cast + 1:-1], rest[-1]
    for src, dst in zip(cast_in, cast_out):
        dst[...] = src[...].astype(BF16)
    d_ff = w1_ref.shape[1]
    for r in range(x_ref.shape[0] // sub_rows):
        rows = slice(r * sub_rows, (r + 1) * sub_rows)
        x = x_ref[rows, :]
        h = _adaln(x, g_ref[...], shift_ref[...], scale_ref[...]).astype(BF16)
        for c in range(d_ff // n_chunk):
            sl = slice(c * n_chunk, (c + 1) * n_chunk)
            u = _dot(h, w1_ref[:, sl])
            v = _dot(h, w3_ref[:, sl])
            a_ref[rows, sl] = (_silu(u) * v).astype(BF16)
        y = _dot(a_ref[rows, :], w2_ref[...])
        out = x + (0.5 * gate_ref[...]) * y
        if final_norm:
            out = _rms(out, fg_ref[...])
        o_ref[rows, :] = out


def _ffn(x, mod5, layer, sub, row_of_batch, norm_g4, w1, w3, w2, final_g, *, cast=(), final_norm=False,
         tm=1024, n_chunk=256, sub_rows=512):
    bsz, length, d = x.shape
    d_ff = w1.shape[-1]
    tm = min(tm, length)
    n_t = length // tm
    n_steps = bsz * n_t
    bf16_rows = 16
    cast_in_specs, cast_out_specs, cast_shapes = [], [], []
    for arr, lead in cast:
        rows, cols = arr.shape[len(lead):]
        band = rows // n_steps
        assert band * n_steps == rows and band % bf16_rows == 0, (arr.shape, n_steps)
        cast_in_specs.append(pl.BlockSpec(
            (None,) * len(lead) + (band, cols),
            functools.partial(lambda lead, b, t: tuple(lead) + (b * n_t + t, 0), lead)))
        cast_out_specs.append(pl.BlockSpec((band, cols), lambda b, t: (b * n_t + t, 0)))
        cast_shapes.append(jax.ShapeDtypeStruct((rows, cols), BF16))
    kern = functools.partial(_ffn_kernel, n_cast=len(cast), n_chunk=n_chunk, sub_rows=min(sub_rows, tm),
                             final_norm=final_norm)
    out = pl.pallas_call(
        kern,
        out_shape=[jax.ShapeDtypeStruct(x.shape, F32)] + cast_shapes,
        grid=(bsz, n_t),
        in_specs=[_tok_spec(tm, d)]
        + _mod_block_specs(d, layer, sub, row_of_batch)
        + [_pick_spec(norm_g4.shape, (layer, sub)), _const_spec(w1.shape), _const_spec(w3.shape),
           _const_spec(w2.shape), _const_spec((1, d))]
        + cast_in_specs,
        out_specs=[_tok_spec(tm, d)] + cast_out_specs,
        scratch_shapes=[pltpu.VMEM((tm, d_ff), BF16)],
        compiler_params=_cparams(2),
        name="swiglu_ffn",
    )(x, mod5, mod5, mod5, norm_g4, w1, w3, w2, final_g.reshape(1, d), *[arr for arr, _ in cast])
    return out[0], list(out[1:])


def _hgrn_lower_bounds(raw, slot):
    ex = jnp.exp(raw - jnp.max(raw, axis=0, keepdims=True))
    sm = ex / jnp.sum(ex, axis=0, keepdims=True)
    lb = sm[0]
    for k in range(1, slot + 1):
        lb = lb + sm[k]
    return lb


def _abin_kernel(x_ref, shift_ref, scale_ref, g_ref, w_ref, lb_ref, q_ref, v_ref, lff_ref, kf_ref, lfb_ref,
                 kb_ref, sg_ref, p_ref, *, slot, sub_rows):
    dp, dh = p_ref.shape[-1], q_ref.shape[-1]
    lb_all = _hgrn_lower_bounds(lb_ref[...], slot)
    for r in range(x_ref.shape[0] // sub_rows):
        rows = slice(r * sub_rows, (r + 1) * sub_rows)
        h = _adaln(x_ref[rows, :], g_ref[...], shift_ref[...], scale_ref[...]).astype(BF16)

        def slab(k):
            return _dot(h, w_ref[:, dp + k * dh:dp + (k + 1) * dh])

        q_ref[rows, :] = (_silu(slab(0)) * HGRN_HEAD_DIM ** -0.5).astype(BF16)
        v_ref[rows, :] = slab(1).astype(BF16)
        for d, (lf_ref, kk_ref) in enumerate(((lff_ref, kf_ref), (lfb_ref, kb_ref))):
            lb = lb_all[d:d + 1]
            forget = lb + (1.0 - lb) * _sigmoid(slab(2 + d))
            lf_ref[rows, :] = jnp.log(forget) * LOG2_E
            kk_ref[rows, :] = (1.0 - forget).astype(BF16)
        sg_ref[rows, :] = _silu(slab(4)).astype(BF16)
        p_ref[rows, :] = _dot(h, w_ref[:, 0:dp]).astype(BF16)


def _ab_in(x, mod5, layer, row_of_batch, norm_g4, w_in, hgrn_lb, slot, d_pool, *, tm=1024, sub_rows=512):
    bsz, length, d = x.shape
    tm = min(tm, length)
    dh = (w_in.shape[-1] - d_pool) // 5
    shift_spec, scale_spec, _ = _mod_block_specs(d, layer, 1, row_of_batch)
    kern = functools.partial(_abin_kernel, slot=slot, sub_rows=min(sub_rows, tm))
    widths = (dh, dh, dh, dh, dh, dh, dh, d_pool)
    dtypes = (BF16, BF16, F32, BF16, F32, BF16, BF16, BF16)
    return pl.pallas_call(
        kern,
        out_shape=[jax.ShapeDtypeStruct((bsz, length, n), dt) for n, dt in zip(widths, dtypes)],
        grid=(bsz, length // tm),
        in_specs=[_tok_spec(tm, d), shift_spec, scale_spec, _pick_spec(norm_g4.shape, (layer, 1)),
                  _const_spec(w_in.shape), _const_spec(hgrn_lb.shape)],
        out_specs=[_tok_spec(tm, n) for n in widths],
        compiler_params=_cparams(2),
        name="ab_in_proj",
    )(x, mod5, mod5, norm_g4, w_in, hgrn_lb)


def _level_table(c, diag_block):
    j = np.arange(c)[:, None]
    i = np.arange(c)[None, :]
    x = j ^ i
    bits = np.zeros((c, c), np.int32)
    for k in range(int(np.log2(c))):
        bits = np.where(x >= (1 << k), k + 1, bits)
    lvl = np.where(i < j, bits, 0)
    if diag_block > 1:
        lvl = np.where((i <= j) & (bits <= int(np.log2(diag_block))), int(np.log2(diag_block)), lvl)
    return lvl.astype(np.int32)


def _fine_decay(forget, rev):
    c, w = forget.shape
    f3 = forget.reshape(c // 8, 8, w)
    pos = lax.broadcasted_iota(jnp.int32, (c // 8, 8, w), 1)

    def at(offset):
        return f3 if offset == 0 else pltpu.roll(f3, (-offset) % 8, 1)

    sgn = 1 if rev else -1
    cq = [at(0)]
    ck = [None]
    for t in range(1, 4):
        cq.append(cq[-1] * at(sgn * t))
        ck.append(at(-sgn * t) if ck[-1] is None else ck[-1] * at(-sgn * t))
    out = {}
    for s in (2, 4, 8):
        half = s // 2
        p = pos % s
        e = jnp.ones_like(f3)
        for t in range(half):
            q_pos = (half - 1 - t) if rev else (half + t)
            k_pos = (half + t) if rev else (half - 1 - t)
            e = jnp.where(p == q_pos, cq[t], e)
            if t > 0:
                e = jnp.where(p == k_pos, ck[t], e)
        out[s] = e.reshape(c, w)
    return out


def _block_row(b, s, ridx):
    c, w = b.shape
    r = b.reshape(c // s, s, w)[:, ridx:ridx + 1, :]
    return jnp.broadcast_to(r, (c // s, s, w)).reshape(c, w)


def _key_decay_column(b_end, shape):
    return jnp.broadcast_to(jnp.exp2(b_end), shape).T


def _hgrn_head(qq, kk, v, forget, b, st, lvl, rev):
    c = qq.shape[0]
    b_end = b[0:1] if rev else b[c - 1:c]
    vb = v.astype(BF16)
    o = _dot((qq * jnp.exp2(b)).astype(BF16), st.astype(BF16))
    k_out = (kk * jnp.exp2(b_end - b)).astype(BF16)
    st_new = st * _key_decay_column(b_end, st.shape) + _dot_tn(k_out, vb)

    row = lax.broadcasted_iota(jnp.int32, (c, 1), 0)
    fine = _fine_decay(forget, rev)
    a = jnp.zeros((c, c), F32)
    s = c
    while s > 1:
        half = s // 2
        q_side = ((row % s) >= half) != rev
        if s >= 16:
            e = jnp.exp2(-jnp.abs(b - _block_row(b, s, half if rev else half - 1)))
        else:
            e = fine[s]
        x = jnp.where(q_side, qq, kk) * e
        a = jnp.where(lvl == int(np.log2(s)), _dot(x.astype(BF16), x.T.astype(BF16)), a)
        s //= 2
    o = o + _dot(a.astype(BF16), vb) + jnp.sum(qq * kk, axis=-1, keepdims=True) * v.astype(F32)
    return o, st_new


def _hgrn_mild_direction(rev, qq_ref, kk_ref, b_ref, v_ref, o_ref, st_ref, lvl, ops_ref, a_ref):
    c = qq_ref.shape[0]
    n_heads = st_ref.shape[0]
    row = lax.broadcasted_iota(jnp.int32, (c, 1), 0)
    sizes = []
    s = c
    while s > HGRN_DIAG_BLOCK:
        sizes.append(s)
        s //= 2

    for h in range(n_heads):
        sl = slice(h * HGRN_HEAD_DIM, (h + 1) * HGRN_HEAD_DIM)
        qq, kk, b = qq_ref[:, sl].astype(F32), kk_ref[:, sl].astype(F32), b_ref[:, sl]
        b_end = b[0:1] if rev else b[c - 1:c]
        st = st_ref[h]
        ops_ref[h, 0] = (qq * jnp.exp2(b)).astype(BF16)
        ops_ref[h, 1] = (kk * jnp.exp2(b_end - b)).astype(BF16)
        ops_ref[h, 2] = st.astype(BF16)
        st_ref[h] = st * _key_decay_column(b_end, st.shape)
        for n, s in enumerate(sizes):
            half = s // 2
            q_side = ((row % s) >= half) != rev
            e = jnp.exp2(-jnp.abs(b - _block_row(b, s, half if rev else half - 1)))
            x = jnp.where(q_side, qq, kk) * e
            ops_ref[h, 3 + 2 * n] = x.astype(BF16)
            ops_ref[h, 4 + 2 * n] = x.T.astype(BF16)
        d = b - _block_row(b, HGRN_DIAG_BLOCK, HGRN_DIAG_BLOCK - 1 if rev else 0)
        ops_ref[h, 3 + 2 * len(sizes)] = (qq * jnp.exp2(d)).astype(BF16)
        ops_ref[h, 4 + 2 * len(sizes)] = (kk * jnp.exp2(-d)).T.astype(BF16)

    for h in range(n_heads):
        a = jnp.zeros((c, c), F32)
        for n, s in enumerate(sizes + [HGRN_DIAG_BLOCK]):
            a = jnp.where(lvl == int(np.log2(s)), _dot(ops_ref[h, 3 + 2 * n], ops_ref[h, 4 + 2 * n]), a)
        a_ref[h] = a.astype(BF16)

    for h in range(n_heads):
        sl = slice(h * HGRN_HEAD_DIM, (h + 1) * HGRN_HEAD_DIM)
        vb = v_ref[:, sl].astype(BF16)
        o = _dot(ops_ref[h, 0], ops_ref[h, 2]) + _dot(a_ref[h], vb)
        o_ref[:, sl] = o.astype(o_ref.dtype)
        st_ref[h] = st_ref[h] + _dot_tn(ops_ref[h, 1], vb)


def _hgrn_kernel(qf_ref, vf_ref, lf_ref, kf_ref, qb_ref, vb_ref, lb_ref, kb_ref, s0_ref, tri_ref, lvl_ref,
                 of_ref, ob_ref, sout_ref, s_scr, b_s, ops_s, a_s):
    t = pl.program_id(1)

    @pl.when(t == 0)
    def _():
        s_scr[...] = s0_ref[...]

    c = HGRN_CHUNK
    n_sub = qf_ref.shape[0] // c
    n_heads = qf_ref.shape[-1] // HGRN_HEAD_DIM

    def chunk(i, carry):
        offs = (pl.multiple_of(i * c, c), pl.multiple_of((n_sub - 1 - i) * c, c))
        dirs = tuple(tuple(ref.at[pl.ds(off, c), :] for ref in refs)
                     for off, refs in zip(offs, ((qf_ref, vf_ref, lf_ref, kf_ref, of_ref),
                                                 (qb_ref, vb_ref, lb_ref, kb_ref, ob_ref))))
        weakest = None
        for d, (q_ref, v_ref, logf_ref, kk_ref, o_ref) in enumerate(dirs):
            logf = logf_ref[...]
            hi = logf.astype(BF16)
            lo = (logf - hi.astype(F32)).astype(BF16)
            tri = tri_ref[d]
            b = _dot(tri, hi) + _dot(tri, lo)
            b_s[d] = b
            for m in range(c // HGRN_DIAG_BLOCK):
                lo_row, hi_row = m * HGRN_DIAG_BLOCK, (m + 1) * HGRN_DIAG_BLOCK - 1
                if d == 0:
                    span = b[hi_row:hi_row + 1] - b[lo_row:lo_row + 1]
                else:
                    span = b[lo_row:lo_row + 1] - b[hi_row:hi_row + 1]
                weakest = span if weakest is None else jnp.minimum(weakest, span)
        in_range = jnp.min(weakest) > -HGRN_SAFE_LOG2_SPAN

        @pl.when(in_range)
        def _():
            for d, (q_ref, v_ref, logf_ref, kk_ref, o_ref) in enumerate(dirs):
                _hgrn_mild_direction(d == 1, q_ref, kk_ref, b_s.at[d], v_ref, o_ref, s_scr.at[d],
                                     lvl_ref[d], ops_s.at[d], a_s.at[d])

        @pl.when(jnp.logical_not(in_range))
        def _():
            for d, (q_ref, v_ref, logf_ref, kk_ref, o_ref) in enumerate(dirs):
                lvl = lvl_ref[2 + d]
                for h in range(n_heads):
                    sl = slice(h * HGRN_HEAD_DIM, (h + 1) * HGRN_HEAD_DIM)
                    o_h, st_new = _hgrn_head(q_ref[:, sl].astype(F32), kk_ref[:, sl].astype(F32), v_ref[:, sl],
                                             jnp.exp2(logf_ref[:, sl]), b_s[d, :, sl], s_scr[d, h], lvl, d == 1)
                    o_ref[:, sl] = o_h.astype(o_ref.dtype)
                    s_scr[d, h] = st_new

        return carry

    lax.fori_loop(0, n_sub, chunk, 0)

    @pl.when(t == pl.num_programs(1) - 1)
    def _():
        sout_ref[...] = s_scr[...]


def _hgrn_scan(q, v, logf_fwd, kk_fwd, logf_bwd, kk_bwd, s0, *, rows=512):
    bsz, length, dh = q.shape
    c = HGRN_CHUNK
    rows = min(rows, length)
    n = length // rows
    n_heads = dh // HGRN_HEAD_DIM
    assert c == HGRN_HEAD_DIM and c > HGRN_DIAG_BLOCK
    n_mild = int(np.log2(c // HGRN_DIAG_BLOCK)) + 1
    lvl_fast = _level_table(c, HGRN_DIAG_BLOCK)
    lvl_full = _level_table(c, 1)
    lvl = jnp.asarray(np.stack([lvl_fast, lvl_fast.T, lvl_full, lvl_full.T]))
    tri_f = np.tril(np.ones((c, c), np.float32))
    tri = jnp.asarray(np.stack([tri_f, tri_f.T]), BF16)
    fwd = pl.BlockSpec((None, rows, dh), lambda b, t: (b, t, 0))
    bwd = pl.BlockSpec((None, rows, dh), lambda b, t: (b, n - 1 - t, 0))
    s_shape = (2, n_heads, HGRN_HEAD_DIM, HGRN_HEAD_DIM)
    s_spec = pl.BlockSpec((None,) + s_shape, lambda b, t: (b, 0, 0, 0, 0))
    return pl.pallas_call(
        _hgrn_kernel,
        out_shape=[jax.ShapeDtypeStruct(q.shape, BF16), jax.ShapeDtypeStruct(q.shape, BF16),
                   jax.ShapeDtypeStruct((bsz,) + s_shape, F32)],
        grid=(bsz, n),
        in_specs=[fwd, fwd, fwd, fwd, bwd, bwd, bwd, bwd, s_spec, _const_spec((2, c, c)),
                  _const_spec((4, c, c))],
        out_specs=[fwd, bwd, s_spec],
        scratch_shapes=[pltpu.VMEM(s_shape, F32), pltpu.VMEM((2, c, dh), F32),
                        pltpu.VMEM((2, n_heads, 3 + 2 * n_mild, c, HGRN_HEAD_DIM), BF16),
                        pltpu.VMEM((2, n_heads, c, c), BF16)],
        compiler_params=_cparams(2),
        name="hgrn2_scan",
    )(q, v, logf_fwd, kk_fwd, q, v, logf_bwd, kk_bwd, s0, tri, lvl)


def _pool_tables(tm, row_len):
    t = np.arange(tm)
    same_row = (t[:, None] // row_len) == (t[None, :] // row_len)
    pos = t % row_len
    bands, inv = [], []
    for w in POOL_WINDOWS:
        lo = np.clip(pos - w // 2, 0, row_len)
        hi = np.clip(pos - w // 2 + w, 0, row_len)
        u = pos[None, :]
        bands.append(same_row & (u >= lo[:, None]) & (u < hi[:, None]))
        inv.append(1.0 / (hi - lo).astype(np.float64))
    return np.stack(bands).astype(np.float32), np.stack(inv, axis=1).astype(np.float32)


def _about_kernel(x_ref, of_ref, ob_ref, sg_ref, a_ref, gate_ref, gn_ref, band_ref, icnt_ref, wp_ref,
                  ps_ref, wo_ref, o_ref):
    sub_rows = band_ref.shape[-1]
    dp = a_ref.shape[-1]
    n_groups = band_ref.shape[0]
    for r in range(x_ref.shape[0] // sub_rows):
        rows = slice(r * sub_rows, (r + 1) * sub_rows)
        o = of_ref[rows, :].astype(F32) + ob_ref[rows, :].astype(F32)
        heads = []
        for h in range(o.shape[-1] // HGRN_HEAD_DIM):
            oh = o[:, h * HGRN_HEAD_DIM:(h + 1) * HGRN_HEAD_DIM]
            heads.append(oh * lax.rsqrt(jnp.mean(oh * oh, axis=-1, keepdims=True) + EPS))
        b_mix = jnp.concatenate(heads, axis=-1) * gn_ref[...] * sg_ref[rows, :].astype(F32)

        xab = a_ref[rows, :].astype(BF16)
        xa = xab.astype(F32)
        lane_group = lax.broadcasted_iota(jnp.int32, xa.shape, 1) // (dp // n_groups)
        total = jnp.zeros_like(xa)
        for gi in range(n_groups):
            total = jnp.where(lane_group == gi, _dot(band_ref[gi], xab), total)
        y = total * icnt_ref[...] - xa
        a_mix = _dot(y.astype(BF16), wp_ref[...]) * ps_ref[...]

        mix = _dot(a_mix.astype(BF16), wo_ref[0:dp, :]) + _dot(b_mix.astype(BF16), wo_ref[dp:, :])
        o_ref[rows, :] = x_ref[rows, :] + gate_ref[...] * mix


def _ab_out(x, o_f, o_b, g, xa, mod5, layer, row_of_batch, g_norm, e, w_pool_bd, pool_scale, w_out, row_len,
            *, tm=512, sub_rows=256):
    bsz, length, d = x.shape
    dh = o_f.shape[-1]
    dp = xa.shape[-1]
    tm = min(tm, length)
    sub_rows = min(sub_rows, tm)
    n_groups = len(POOL_WINDOWS)
    bands, inv = _pool_tables(sub_rows, row_len)
    icnt = jnp.asarray(np.repeat(inv, dp // n_groups, axis=1))
    _, _, gate_spec = _mod_block_specs(d, layer, 1, row_of_batch)
    return pl.pallas_call(
        _about_kernel,
        out_shape=jax.ShapeDtypeStruct(x.shape, F32),
        grid=(bsz, length // tm),
        in_specs=[_tok_spec(tm, d), _tok_spec(tm, dh), _tok_spec(tm, dh), _tok_spec(tm, dh), _tok_spec(tm, dp),
                  gate_spec, _pick_spec(g_norm.shape, (e,)), _const_spec((n_groups, sub_rows, sub_rows)),
                  _const_spec((sub_rows, dp)), _const_spec((dp, dp)), _pick_spec(pool_scale.shape, (e,)),
                  _const_spec(w_out.shape)],
        out_specs=_tok_spec(tm, d),
        compiler_params=_cparams(2),
        name="ab_out_proj",
    )(x, o_f, o_b, g, xa, mod5, g_norm, jnp.asarray(bands, BF16), icnt, w_pool_bd, pool_scale, w_out)


def _gelu_tanh(x):
    k1 = float(-2.0 * np.sqrt(2.0 / np.pi) * np.log2(np.e))
    k2 = 0.044715 * k1
    return x / (1.0 + jnp.exp2(x * (k1 + k2 * (x * x))))


def _gmlp_kernel(x_ref, shift_ref, scale_ref, gate_ref, g_ref, win_ref, lng_ref, lnb_ref, ws_ref, bs_ref,
                 wout_ref, o_ref, s_ref, *, sub_rows):
    dg = win_ref.shape[1] // 2
    for r in range(x_ref.shape[0] // sub_rows):
        rows = slice(r * sub_rows, (r + 1) * sub_rows)
        x = x_ref[rows, :]
        h = _adaln(x, g_ref[...], shift_ref[...], scale_ref[...]).astype(BF16)
        u = _gelu_tanh(_dot(h, win_ref[:, :dg]))
        v = _gelu_tanh(_dot(h, win_ref[:, dg:]))
        vc = v - jnp.mean(v, axis=-1, keepdims=True)
        v = vc * lax.rsqrt(jnp.mean(vc * vc, axis=-1, keepdims=True) + EPS) * lng_ref[...] + lnb_ref[...]
        vb = v.astype(BF16)
        for c in range(sub_rows // GMLP_CHUNK):
            crows = slice(c * GMLP_CHUNK, (c + 1) * GMLP_CHUNK)
            srows = slice(r * sub_rows + c * GMLP_CHUNK, r * sub_rows + (c + 1) * GMLP_CHUNK)
            for gi in range(dg // GMLP_GROUP_DIM):
                cols = slice(gi * GMLP_GROUP_DIM, (gi + 1) * GMLP_GROUP_DIM)
                sv = _dot(ws_ref[gi], vb[crows, cols]) + bs_ref[gi]
                s_ref[srows, cols] = (u[crows, cols] * sv).astype(BF16)
        o_ref[rows, :] = x + gate_ref[...] * _dot(s_ref[rows, :], wout_ref[...])


def _gmlp(x, mod5, layer, row_of_batch, norm_g4, o, w_in, ln_g, ln_b, w_s, b_s_wide, w_out, *, tm=1024,
          sub_rows=512):
    bsz, length, d = x.shape
    dg = w_in.shape[-1] // 2
    tm = min(tm, length)
    kern = functools.partial(_gmlp_kernel, sub_rows=min(sub_rows, tm))
    return pl.pallas_call(
        kern,
        out_shape=jax.ShapeDtypeStruct(x.shape, F32),
        grid=(bsz, length // tm),
        in_specs=[_tok_spec(tm, d)]
        + _mod_block_specs(d, layer, 1, row_of_batch)
        + [_pick_spec(norm_g4.shape, (layer, 1)), _const_spec(w_in.shape), _pick_spec(ln_g.shape, (o,)),
           _pick_spec(ln_b.shape, (o,)), _const_spec(w_s.shape), _pick_spec(b_s_wide.shape, (o,)),
           _const_spec(w_out.shape)],
        out_specs=_tok_spec(tm, d),
        scratch_shapes=[pltpu.VMEM((tm, dg), BF16)],
        compiler_params=_cparams(2),
        name="gmlp_mixer",
    )(x, mod5, mod5, mod5, norm_g4, w_in, ln_g, ln_b, w_s, b_s_wide, w_out)


def _block_diag(w):
    n, g, a, b = w.shape
    eye = jnp.eye(g, dtype=w.dtype)
    return (w[:, :, :, None, :] * eye[None, :, None, :, None]).reshape(n, g * a, g * b)


def kernel(x, c, ctx, c_ctx, mod_w, mod_b, norm_g, ffn_w1, ffn_w3, ffn_w2, ab_w_in, pool_w, pool_scale,
           hgrn_lb, hgrn_norm_g, ab_w_out, gmlp_w_in, gmlp_ln_g, gmlp_ln_b, gmlp_w_s, gmlp_b_s, gmlp_w_out,
           final_g):
    bsz, _, d = x.shape
    depth = mod_w.shape[0]
    d_pool = pool_scale.shape[-1]
    d_hgrn = hgrn_norm_g.shape[-1]
    n_heads = d_hgrn // HGRN_HEAD_DIM

    c_rows = jnp.concatenate([c, c_ctx[None, :]], axis=0)
    mod = _modulation(c_rows, mod_w, mod_b)
    mod5 = mod.reshape(depth, bsz + 1, N_MOD, 1, d)
    lat_row = lambda b: b
    ctx_row = lambda b: bsz

    norm_g4 = norm_g[:, :, None, :]
    pool_bd = _block_diag(pool_w).astype(BF16)
    pool_scale3, hgrn_norm_g3 = pool_scale[:, None, :], hgrn_norm_g[:, None, :]
    gm_ln_g, gm_ln_b = gmlp_ln_g[:, None, :], gmlp_ln_b[:, None, :]
    gm_b_wide = jnp.broadcast_to(gmlp_b_s[:, :, :, None], gmlp_b_s.shape + (GMLP_GROUP_DIM,))
    gmlp_w_s2 = gmlp_w_s.reshape(gmlp_w_s.shape[0], -1, gmlp_w_s.shape[-1])

    def ffn_sources(i, j):
        return [(ffn_w1, (i, j)), (ffn_w3, (i, j)), (ffn_w2, (i, j))]

    def mixer_sources(i):
        if i % 2 == 0:
            return [(ab_w_in, (i // 2,)), (ab_w_out, (i // 2,))]
        return [(gmlp_w_in, (i // 2,)), (gmlp_w_s2, (i // 2,)), (gmlp_w_out, (i // 2,))]

    ffn_order = [(i, j) for i in range(depth) for j in range(2)]
    ffn_w = {ffn_order[0]: [arr[lead].astype(BF16) for arr, lead in ffn_sources(*ffn_order[0])]}
    mixer_w = {}

    def latent_ffn(xl, i, j, **kw):
        k = ffn_order.index((i, j))
        cast = ffn_sources(*ffn_order[k + 1]) if k + 1 < len(ffn_order) else []
        n_next = len(cast)
        if j == 0:
            cast = cast + mixer_sources(i)
        xl, done = _ffn(xl, mod5, i, 2 * j, lat_row, norm_g4, *ffn_w[(i, j)], final_g, cast=cast, **kw)
        if n_next:
            ffn_w[ffn_order[k + 1]] = done[:n_next]
        if j == 0:
            mixer_w[i] = done[n_next:]
        return xl

    def context_ffn(xc, i, j):
        return _ffn(xc, mod5, i, 2 * j, ctx_row, norm_g4, *ffn_w[(i, j)], final_g)[0]

    xl, xc = x, ctx
    for i in range(depth):
        ctx_live = any(j % 2 == 0 for j in range(i, depth))
        last = i == depth - 1

        xl = latent_ffn(xl, i, 0)
        if ctx_live:
            xc = context_ffn(xc, i, 0)

        if i % 2 == 0:
            e = i // 2
            ab_in_w, ab_out_w = mixer_w[i]
            ab_in = functools.partial(_ab_in, mod5=mod5, layer=i, norm_g4=norm_g4, w_in=ab_in_w,
                                      hgrn_lb=hgrn_lb, slot=e, d_pool=d_pool)
            q_c, v_c, lf_c, kf_c, lb_c, kb_c, sg_c, a_c = ab_in(xc, row_of_batch=ctx_row)
            q_l, v_l, lf_l, kf_l, lb_l, kb_l, sg_l, a_l = ab_in(xl, row_of_batch=lat_row)
            s0 = jnp.zeros((bsz, 2, n_heads, HGRN_HEAD_DIM, HGRN_HEAD_DIM), F32)
            of_c, ob_c, s_ctx = _hgrn_scan(q_c, v_c, lf_c, kf_c, lb_c, kb_c, s0)
            of_l, ob_l, _ = _hgrn_scan(q_l, v_l, lf_l, kf_l, lb_l, kb_l, s_ctx)
            ab_out = functools.partial(_ab_out, mod5=mod5, layer=i, g_norm=hgrn_norm_g3, e=e,
                                       w_pool_bd=pool_bd[e], pool_scale=pool_scale3, w_out=ab_out_w)
            xc = ab_out(xc, of_c, ob_c, sg_c, a_c, row_of_batch=ctx_row, row_len=xc.shape[1])
            xl = ab_out(xl, of_l, ob_l, sg_l, a_l, row_of_batch=lat_row, row_len=GRID_W)
        else:
            o = i // 2
            gm_in, gm_s2, gm_out = mixer_w[i]
            gmlp = functools.partial(_gmlp, mod5=mod5, layer=i, norm_g4=norm_g4, o=o, w_in=gm_in, ln_g=gm_ln_g,
                                     ln_b=gm_ln_b, w_s=gm_s2.reshape(gmlp_w_s.shape[1:]), b_s_wide=gm_b_wide,
                                     w_out=gm_out)
            xl = gmlp(xl, row_of_batch=lat_row)
            if ctx_live:
                xc = gmlp(xc, row_of_batch=ctx_row)

        xl = latent_ffn(xl, i, 1, final_norm=last)
        if ctx_live:
            xc = context_ffn(xc, i, 1)
    return xl
```

```python
import functools

import numpy as np
import jax
import jax.numpy as jnp
from jax import lax
from jax.experimental import pallas as pl
from jax.experimental.pallas import tpu as pltpu

F32 = jnp.float32
BF16 = jnp.bfloat16

EPS = 1e-6
N_MOD = 9
GRID_W = 64
POOL_WINDOWS = (2, 4, 8, 16)
HGRN_HEAD_DIM = 128
GMLP_CHUNK = 128
GMLP_GROUP_DIM = 128

LOG2_E = float(np.log2(np.e))
HGRN_CHUNK = 128
HGRN_DIAG_BLOCK = 32
HGRN_SAFE_LOG2_SPAN = 80.0

VMEM_LIMIT_BYTES = 56 * 1024 * 1024


def _cparams(n_grid_dims):
    return pltpu.CompilerParams(
        dimension_semantics=("arbitrary",) * n_grid_dims,
        vmem_limit_bytes=VMEM_LIMIT_BYTES)


def _sigmoid(x):
    return 1.0 / (1.0 + jnp.exp2(x * (-LOG2_E)))


def _silu(x):
    return x * _sigmoid(x)


def _rms(x, g):
    return x * lax.rsqrt(jnp.mean(x * x, axis=-1, keepdims=True) + EPS) * g


def _adaln(x, g, shift, scale):
    return _rms(x, g) * (1.0 + scale) + shift


def _dot(a, b):
    return jnp.dot(a, b, preferred_element_type=F32)


def _dot_tn(a, b):
    return lax.dot_general(a, b, (((0,), (0,)), ((), ())), preferred_element_type=F32)


def _const_spec(shape):
    nd = len(shape)
    return pl.BlockSpec(shape, lambda *_: (0,) * nd, pipeline_mode=pl.Buffered(1))


def _pick_spec(full_shape, lead):
    tail = tuple(full_shape[len(lead):])
    idx = tuple(lead) + (0,) * len(tail)
    return pl.BlockSpec((None,) * len(lead) + tail, lambda *_: idx, pipeline_mode=pl.Buffered(1))


def _tok_spec(tm, n):
    return pl.BlockSpec((None, tm, n), lambda b, t: (b, t, 0))


def _mod_kernel(c_ref, w_ref, b_ref, o_ref, a_scr):
    @pl.when((pl.program_id(0) == 0) & (pl.program_id(1) == 0))
    def _():
        a_scr[...] = _silu(c_ref[...])

    rows, d, lanes = a_scr.shape
    tn = w_ref.shape[1]
    n_blk = tn // lanes
    sub = 8

    def body(kc, accs):
        k0 = pl.multiple_of(kc * sub, sub)
        w8 = w_ref[pl.ds(k0, sub), :]
        out = []
        for r in range(rows):
            a8 = a_scr[r, pl.ds(k0, sub), :]
            out.append(tuple(accs[r][j] + a8 * w8[:, j * lanes:(j + 1) * lanes] for j in range(n_blk)))
        return tuple(out)

    zero = jnp.zeros((sub, lanes), F32)
    accs = lax.fori_loop(0, d // sub, body, tuple((zero,) * n_blk for _ in range(rows)), unroll=4)
    for r in range(rows):
        row = jnp.concatenate([jnp.sum(a, axis=0, keepdims=True) for a in accs[r]], axis=1)
        o_ref[r:r + 1, :] = row + b_ref[...]


def _modulation(c_rows, mod_w, mod_b, tn=1152):
    depth, d, n = mod_w.shape
    rows = c_rows.shape[0]
    lanes = 128
    c_wide = jnp.broadcast_to(c_rows[:, :, None], (rows, d, lanes))
    return pl.pallas_call(
        _mod_kernel,
        out_shape=jax.ShapeDtypeStruct((depth, rows, n), F32),
        grid=(depth, n // tn),
        in_specs=[
            _const_spec((rows, d, lanes)),
            pl.BlockSpec((None, d, tn), lambda l, j: (l, 0, j)),
            pl.BlockSpec((None, 1, tn), lambda l, j: (l, 0, j)),
        ],
        out_specs=pl.BlockSpec((None, rows, tn), lambda l, j: (l, 0, j)),
        scratch_shapes=[pltpu.VMEM((rows, d, lanes), F32)],
        compiler_params=_cparams(2),
        name="modulation",
    )(c_wide, mod_w, mod_b.reshape(depth, 1, n))


def _mod_index(layer, col, row_of_batch, b, t):
    return (layer, row_of_batch(b), col, 0, 0)


def _mod_block_specs(d_model, layer, sub, row_of_batch):
    return [pl.BlockSpec((None, None, None, 1, d_model),
                         functools.partial(_mod_index, layer, 3 * sub + k, row_of_batch))
            for k in range(3)]


def _ffn_kernel(x_ref, shift_ref, scale_ref, gate_ref, g_ref, w1_ref, w3_ref, w2_ref, fg_ref, *rest,
                n_cast, n_chunk, sub_rows, final_norm):
    cast_in, o_ref, cast_out, a_ref = rest[:n_cast], rest[n_cast], rest[n_cast + 1:-1], rest[-1]
    for src, dst in zip(cast_in, cast_out):
        dst[...] = src[...].astype(BF16)
    d_ff = w1_ref.shape[1]
    for r in range(x_ref.shape[0] // sub_rows):
        rows = slice(r * sub_rows, (r + 1) * sub_rows)
        x = x_ref[rows, :]
        h = _adaln(x, g_ref[...], shift_ref[...], scale_ref[...]).astype(BF16)
        for c in range(d_ff // n_chunk):
            sl = slice(c * n_chunk, (c + 1) * n_chunk)
            u = _dot(h, w1_ref[:, sl])
            v = _dot(h, w3_ref[:, sl])
            a_ref[rows, sl] = (_silu(u) * v).astype(BF16)
        y = _dot(a_ref[rows, :], w2_ref[...])
        out = x + (0.5 * gate_ref[...]) * y
        if final_norm:
            out = _rms(out, fg_ref[...])
        o_ref[rows, :] = out


def _ffn(x, mod5, layer, sub, row_of_batch, norm_g4, w1, w3, w2, final_g, *, cast=(), final_norm=False,
         tm=1024, n_chunk=256, sub_rows=512):
    bsz, length, d = x.shape
    d_ff = w1.shape[-1]
    tm = min(tm, length)
    n_t = length // tm
    n_steps = bsz * n_t
    bf16_rows = 16
    cast_in_specs, cast_out_specs, cast_shapes = [], [], []
    for arr, lead in cast:
        rows, cols = arr.shape[len(lead):]
        band = rows // n_steps
        assert band * n_steps == rows and band % bf16_rows == 0, (arr.shape, n_steps)
        cast_in_specs.append(pl.BlockSpec(
            (None,) * len(lead) + (band, cols),
            functools.partial(lambda lead, b, t: tuple(lead) + (b * n_t + t, 0), lead)))
        cast_out_specs.append(pl.BlockSpec((band, cols), lambda b, t: (b * n_t + t, 0)))
        cast_shapes.append(jax.ShapeDtypeStruct((rows, cols), BF16))
    kern = functools.partial(_ffn_kernel, n_cast=len(cast), n_chunk=n_chunk, sub_rows=min(sub_rows, tm),
                             final_norm=final_norm)
    out = pl.pallas_call(
        kern,
        out_shape=[jax.ShapeDtypeStruct(x.shape, F32)] + cast_shapes,
        grid=(bsz, n_t),
        in_specs=[_tok_spec(tm, d)]
        + _mod_block_specs(d, layer, sub, row_of_batch)
        + [_pick_spec(norm_g4.shape, (layer, sub)), _const_spec(w1.shape), _const_spec(w3.shape),
           _const_spec(w2.shape), _const_spec((1, d))]
        + cast_in_specs,
        out_specs=[_tok_spec(tm, d)] + cast_out_specs,
        scratch_shapes=[pltpu.VMEM((tm, d_ff), BF16)],
        compiler_params=_cparams(2),
        name="swiglu_ffn",
    )(x, mod5, mod5, mod5, norm_g4, w1, w3, w2, final_g.reshape(1, d), *[arr for arr, _ in cast])
    return out[0], list(out[1:])


def _hgrn_lower_bounds(raw, slot):
    ex = jnp.exp(raw - jnp.max(raw, axis=0, keepdims=True))
    sm = ex / jnp.sum(ex, axis=0, keepdims=True)
    lb = sm[0]
    for k in range(1, slot + 1):
        lb = lb + sm[k]
    return lb


def _abin_kernel(x_ref, shift_ref, scale_ref, g_ref, w_ref, lb_ref, q_ref, v_ref, lff_ref, kf_ref, lfb_ref,
                 kb_ref, sg_ref, p_ref, *, slot, sub_rows, n_chunk):
    dp, dh = p_ref.shape[-1], q_ref.shape[-1]
    lb_all = _hgrn_lower_bounds(lb_ref[...], slot)
    for r in range(x_ref.shape[0] // sub_rows):
        rows = slice(r * sub_rows, (r + 1) * sub_rows)
        h = _adaln(x_ref[rows, :], g_ref[...], shift_ref[...], scale_ref[...]).astype(BF16)

        p_ref[rows, :] = _dot(h, w_ref[:, 0:dp]).astype(BF16)
        for k in range(5):
            for c0 in range(0, dh, n_chunk):
                cols = slice(c0, c0 + n_chunk)
                z = _dot(h, w_ref[:, dp + k * dh + c0:dp + k * dh + c0 + n_chunk])
                if k == 0:
                    q_ref[rows, cols] = (_silu(z) * HGRN_HEAD_DIM ** -0.5).astype(BF16)
                elif k == 1:
                    v_ref[rows, cols] = z.astype(BF16)
                elif k == 4:
                    sg_ref[rows, cols] = _silu(z).astype(BF16)
                else:
                    lf_ref, kk_ref = ((lff_ref, kf_ref), (lfb_ref, kb_ref))[k - 2]
                    lb = lb_all[k - 2:k - 1, cols]
                    forget = lb + (1.0 - lb) * _sigmoid(z)
                    lf_ref[rows, cols] = jnp.log(forget) * LOG2_E
                    kk_ref[rows, cols] = (1.0 - forget).astype(BF16)


def _ab_in(x, mod5, layer, row_of_batch, norm_g4, w_in, hgrn_lb, slot, d_pool, *, tm=1024, sub_rows=256,
           n_chunk=768):
    bsz, length, d = x.shape
    tm = min(tm, length)
    dh = (w_in.shape[-1] - d_pool) // 5
    shift_spec, scale_spec, _ = _mod_block_specs(d, layer, 1, row_of_batch)
    kern = functools.partial(_abin_kernel, slot=slot, sub_rows=min(sub_rows, tm), n_chunk=n_chunk)
    widths = (dh, dh, dh, dh, dh, dh, dh, d_pool)
    dtypes = (BF16, BF16, F32, BF16, F32, BF16, BF16, BF16)
    return pl.pallas_call(
        kern,
        out_shape=[jax.ShapeDtypeStruct((bsz, length, n), dt) for n, dt in zip(widths, dtypes)],
        grid=(bsz, length // tm),
        in_specs=[_tok_spec(tm, d), shift_spec, scale_spec, _pick_spec(norm_g4.shape, (layer, 1)),
                  _const_spec(w_in.shape), _const_spec(hgrn_lb.shape)],
        out_specs=[_tok_spec(tm, n) for n in widths],
        compiler_params=_cparams(2),
        name="ab_in_proj",
    )(x, mod5, mod5, norm_g4, w_in, hgrn_lb)


def _level_table(c, diag_block):
    j = np.arange(c)[:, None]
    i = np.arange(c)[None, :]
    x = j ^ i
    bits = np.zeros((c, c), np.int32)
    for k in range(int(np.log2(c))):
        bits = np.where(x >= (1 << k), k + 1, bits)
    lvl = np.where(i < j, bits, 0)
    if diag_block > 1:
        lvl = np.where((i <= j) & (bits <= int(np.log2(diag_block))), int(np.log2(diag_block)), lvl)
    return lvl.astype(np.int32)


def _fine_decay(forget, rev):
    c, w = forget.shape
    f3 = forget.reshape(c // 8, 8, w)
    pos = lax.broadcasted_iota(jnp.int32, (c // 8, 8, w), 1)

    def at(offset):
        return f3 if offset == 0 else pltpu.roll(f3, (-offset) % 8, 1)

    sgn = 1 if rev else -1
    cq = [at(0)]
    ck = [None]
    for t in range(1, 4):
        cq.append(cq[-1] * at(sgn * t))
        ck.append(at(-sgn * t) if ck[-1] is None else ck[-1] * at(-sgn * t))
    out = {}
    for s in (2, 4, 8):
        half = s // 2
        p = pos % s
        e = jnp.ones_like(f3)
        for t in range(half):
            q_pos = (half - 1 - t) if rev else (half + t)
            k_pos = (half + t) if rev else (half - 1 - t)
            e = jnp.where(p == q_pos, cq[t], e)
            if t > 0:
                e = jnp.where(p == k_pos, ck[t], e)
        out[s] = e.reshape(c, w)
    return out


def _block_row(b, s, ridx):
    c, w = b.shape
    r = b.reshape(c // s, s, w)[:, ridx:ridx + 1, :]
    return jnp.broadcast_to(r, (c // s, s, w)).reshape(c, w)


def _key_decay_column(b_end, shape):
    return jnp.broadcast_to(jnp.exp2(b_end), shape).T


def _hgrn_head(qq, kk, v, forget, b, st, lvl, rev):
    c = qq.shape[0]
    b_end = b[0:1] if rev else b[c - 1:c]
    vb = v.astype(BF16)
    o = _dot((qq * jnp.exp2(b)).astype(BF16), st.astype(BF16))
    k_out = (kk * jnp.exp2(b_end - b)).astype(BF16)
    st_new = st * _key_decay_column(b_end, st.shape) + _dot_tn(k_out, vb)

    row = lax.broadcasted_iota(jnp.int32, (c, 1), 0)
    fine = _fine_decay(forget, rev)
    a = jnp.zeros((c, c), F32)
    s = c
    while s > 1:
        half = s // 2
        q_side = ((row % s) >= half) != rev
        if s >= 16:
            e = jnp.exp2(-jnp.abs(b - _block_row(b, s, half if rev else half - 1)))
        else:
            e = fine[s]
        x = jnp.where(q_side, qq, kk) * e
        a = jnp.where(lvl == int(np.log2(s)), _dot(x.astype(BF16), x.T.astype(BF16)), a)
        s //= 2
    o = o + _dot(a.astype(BF16), vb) + jnp.sum(qq * kk, axis=-1, keepdims=True) * v.astype(F32)
    return o, st_new


def _hgrn_mild_direction(rev, qq_ref, kk_ref, b_ref, v_ref, o_ref, st_ref, lvl, ops_ref, qa_ref, sv_ref):
    c = qq_ref.shape[0]
    n_heads = st_ref.shape[0]
    row = lax.broadcasted_iota(jnp.int32, (c, 1), 0)
    sizes = []
    s = c
    while s > HGRN_DIAG_BLOCK:
        sizes.append(s)
        s //= 2

    for h in range(n_heads):
        sl = slice(h * HGRN_HEAD_DIM, (h + 1) * HGRN_HEAD_DIM)
        qq, kk, b = qq_ref[:, sl].astype(F32), kk_ref[:, sl].astype(F32), b_ref[:, sl]
        b_end = b[0:1] if rev else b[c - 1:c]
        st = st_ref[h]
        dk = st.shape[0]
        qa_ref[h, :, 0:dk] = (qq * jnp.exp2(b)).astype(BF16)
        sv_ref[h, 0:dk, :] = st.astype(BF16)
        sv_ref[h, dk:, :] = v_ref[:, sl].astype(BF16)
        ops_ref[h, 0] = (kk * jnp.exp2(b_end - b)).astype(BF16)
        st_ref[h] = st * _key_decay_column(b_end, st.shape)
        for n, s in enumerate(sizes):
            half = s // 2
            q_side = ((row % s) >= half) != rev
            e = jnp.exp2(-jnp.abs(b - _block_row(b, s, half if rev else half - 1)))
            x = jnp.where(q_side, qq, kk) * e
            ops_ref[h, 1 + 2 * n] = x.astype(BF16)
            ops_ref[h, 2 + 2 * n] = x.T.astype(BF16)
        d = b - _block_row(b, HGRN_DIAG_BLOCK, HGRN_DIAG_BLOCK - 1 if rev else 0)
        ops_ref[h, 1 + 2 * len(sizes)] = (qq * jnp.exp2(d)).astype(BF16)
        ops_ref[h, 2 + 2 * len(sizes)] = (kk * jnp.exp2(-d)).T.astype(BF16)

    for h in range(n_heads):
        a = jnp.zeros((c, c), F32)
        for n, s in enumerate(sizes + [HGRN_DIAG_BLOCK]):
            a = jnp.where(lvl == int(np.log2(s)), _dot(ops_ref[h, 1 + 2 * n], ops_ref[h, 2 + 2 * n]), a)
        qa_ref[h, :, HGRN_HEAD_DIM:] = a.astype(BF16)

    for h in range(n_heads):
        sl = slice(h * HGRN_HEAD_DIM, (h + 1) * HGRN_HEAD_DIM)
        o_ref[:, sl] = _dot(qa_ref[h], sv_ref[h]).astype(o_ref.dtype)
        st_ref[h] = st_ref[h] + _dot_tn(ops_ref[h, 0], sv_ref[h, HGRN_HEAD_DIM:, :])


def _hgrn_kernel(qf_ref, vf_ref, lf_ref, kf_ref, qb_ref, vb_ref, lb_ref, kb_ref, s0_ref, tri_ref, lvl_ref,
                 of_ref, ob_ref, sout_ref, s_scr, b_s, ops_s, qa_s, sv_s):
    t = pl.program_id(1)

    @pl.when(t == 0)
    def _():
        s_scr[...] = s0_ref[...]

    c = HGRN_CHUNK
    n_sub = qf_ref.shape[0] // c
    n_heads = qf_ref.shape[-1] // HGRN_HEAD_DIM

    def chunk(i, carry):
        offs = (pl.multiple_of(i * c, c), pl.multiple_of((n_sub - 1 - i) * c, c))
        dirs = tuple(tuple(ref.at[pl.ds(off, c), :] for ref in refs)
                     for off, refs in zip(offs, ((qf_ref, vf_ref, lf_ref, kf_ref, of_ref),
                                                 (qb_ref, vb_ref, lb_ref, kb_ref, ob_ref))))
        weakest = None
        for d, (q_ref, v_ref, logf_ref, kk_ref, o_ref) in enumerate(dirs):
            logf = logf_ref[...]
            hi = logf.astype(BF16)
            lo = (logf - hi.astype(F32)).astype(BF16)
            tri = tri_ref[d]
            b = _dot(tri, hi) + _dot(tri, lo)
            b_s[d] = b
            for m in range(c // HGRN_DIAG_BLOCK):
                lo_row, hi_row = m * HGRN_DIAG_BLOCK, (m + 1) * HGRN_DIAG_BLOCK - 1
                if d == 0:
                    span = b[hi_row:hi_row + 1] - b[lo_row:lo_row + 1]
                else:
                    span = b[lo_row:lo_row + 1] - b[hi_row:hi_row + 1]
                weakest = span if weakest is None else jnp.minimum(weakest, span)
        in_range = jnp.min(weakest) > -HGRN_SAFE_LOG2_SPAN

        @pl.when(in_range)
        def _():
            for d, (q_ref, v_ref, logf_ref, kk_ref, o_ref) in enumerate(dirs):
                _hgrn_mild_direction(d == 1, q_ref, kk_ref, b_s.at[d], v_ref, o_ref, s_scr.at[d],
                                     lvl_ref[d], ops_s.at[d], qa_s.at[d], sv_s.at[d])

        @pl.when(jnp.logical_not(in_range))
        def _():
            for d, (q_ref, v_ref, logf_ref, kk_ref, o_ref) in enumerate(dirs):
                lvl = lvl_ref[2 + d]
                for h in range(n_heads):
                    sl = slice(h * HGRN_HEAD_DIM, (h + 1) * HGRN_HEAD_DIM)
                    o_h, st_new = _hgrn_head(q_ref[:, sl].astype(F32), kk_ref[:, sl].astype(F32), v_ref[:, sl],
                                             jnp.exp2(logf_ref[:, sl]), b_s[d, :, sl], s_scr[d, h], lvl, d == 1)
                    o_ref[:, sl] = o_h.astype(o_ref.dtype)
                    s_scr[d, h] = st_new

        return carry

    lax.fori_loop(0, n_sub, chunk, 0)

    @pl.when(t == pl.num_programs(1) - 1)
    def _():
        sout_ref[...] = s_scr[...]


def _hgrn_scan(q, v, logf_fwd, kk_fwd, logf_bwd, kk_bwd, s0, *, rows=512):
    bsz, length, dh = q.shape
    c = HGRN_CHUNK
    rows = min(rows, length)
    n = length // rows
    n_heads = dh // HGRN_HEAD_DIM
    assert c == HGRN_HEAD_DIM and c > HGRN_DIAG_BLOCK
    n_mild = int(np.log2(c // HGRN_DIAG_BLOCK)) + 1
    lvl_fast = _level_table(c, HGRN_DIAG_BLOCK)
    lvl_full = _level_table(c, 1)
    lvl = jnp.asarray(np.stack([lvl_fast, lvl_fast.T, lvl_full, lvl_full.T]))
    tri_f = np.tril(np.ones((c, c), np.float32))
    tri = jnp.asarray(np.stack([tri_f, tri_f.T]), BF16)
    fwd = pl.BlockSpec((None, rows, dh), lambda b, t: (b, t, 0))
    bwd = pl.BlockSpec((None, rows, dh), lambda b, t: (b, n - 1 - t, 0))
    s_shape = (2, n_heads, HGRN_HEAD_DIM, HGRN_HEAD_DIM)
    s_spec = pl.BlockSpec((None,) + s_shape, lambda b, t: (b, 0, 0, 0, 0))
    return pl.pallas_call(
        _hgrn_kernel,
        out_shape=[jax.ShapeDtypeStruct(q.shape, BF16), jax.ShapeDtypeStruct(q.shape, BF16),
                   jax.ShapeDtypeStruct((bsz,) + s_shape, F32)],
        grid=(bsz, n),
        in_specs=[fwd, fwd, fwd, fwd, bwd, bwd, bwd, bwd, s_spec, _const_spec((2, c, c)),
                  _const_spec((4, c, c))],
        out_specs=[fwd, bwd, s_spec],
        scratch_shapes=[pltpu.VMEM(s_shape, F32), pltpu.VMEM((2, c, dh), F32),
                        pltpu.VMEM((2, n_heads, 1 + 2 * n_mild, c, HGRN_HEAD_DIM), BF16),
                        pltpu.VMEM((2, n_heads, c, HGRN_HEAD_DIM + c), BF16),
                        pltpu.VMEM((2, n_heads, HGRN_HEAD_DIM + c, HGRN_HEAD_DIM), BF16)],
        compiler_params=_cparams(2),
        name="hgrn2_scan",
    )(q, v, logf_fwd, kk_fwd, q, v, logf_bwd, kk_bwd, s0, tri, lvl)


def _pool_tables(tm, row_len):
    t = np.arange(tm)
    same_row = (t[:, None] // row_len) == (t[None, :] // row_len)
    pos = t % row_len
    bands, inv = [], []
    for w in POOL_WINDOWS:
        lo = np.clip(pos - w // 2, 0, row_len)
        hi = np.clip(pos - w // 2 + w, 0, row_len)
        u = pos[None, :]
        bands.append(same_row & (u >= lo[:, None]) & (u < hi[:, None]))
        inv.append(1.0 / (hi - lo).astype(np.float64))
    return np.stack(bands).astype(np.float32), np.stack(inv, axis=1).astype(np.float32)


def _about_kernel(x_ref, of_ref, ob_ref, sg_ref, a_ref, gate_ref, gn_ref, band_ref, icnt_ref, wp_ref,
                  ps_ref, wo_ref, o_ref):
    sub_rows = band_ref.shape[-1]
    dp = a_ref.shape[-1]
    n_groups = band_ref.shape[0]
    for r in range(x_ref.shape[0] // sub_rows):
        rows = slice(r * sub_rows, (r + 1) * sub_rows)
        o = of_ref[rows, :].astype(F32) + ob_ref[rows, :].astype(F32)
        heads = []
        for h in range(o.shape[-1] // HGRN_HEAD_DIM):
            oh = o[:, h * HGRN_HEAD_DIM:(h + 1) * HGRN_HEAD_DIM]
            heads.append(oh * lax.rsqrt(jnp.mean(oh * oh, axis=-1, keepdims=True) + EPS))
        b_mix = jnp.concatenate(heads, axis=-1) * gn_ref[...] * sg_ref[rows, :].astype(F32)

        xab = a_ref[rows, :].astype(BF16)
        xa = xab.astype(F32)
        lane_group = lax.broadcasted_iota(jnp.int32, xa.shape, 1) // (dp // n_groups)
        total = jnp.zeros_like(xa)
        for gi in range(n_groups):
            total = jnp.where(lane_group == gi, _dot(band_ref[gi], xab), total)
        y = total * icnt_ref[...] - xa
        a_mix = _dot(y.astype(BF16), wp_ref[...]) * ps_ref[...]

        mix = _dot(a_mix.astype(BF16), wo_ref[0:dp, :]) + _dot(b_mix.astype(BF16), wo_ref[dp:, :])
        o_ref[rows, :] = x_ref[rows, :] + gate_ref[...] * mix


def _ab_out(x, o_f, o_b, g, xa, mod5, layer, row_of_batch, g_norm, e, w_pool_bd, pool_scale, w_out, row_len,
            *, tm=512, sub_rows=256):
    bsz, length, d = x.shape
    dh = o_f.shape[-1]
    dp = xa.shape[-1]
    tm = min(tm, length)
    sub_rows = min(sub_rows, tm)
    n_groups = len(POOL_WINDOWS)
    bands, inv = _pool_tables(sub_rows, row_len)
    icnt = jnp.asarray(np.repeat(inv, dp // n_groups, axis=1))
    _, _, gate_spec = _mod_block_specs(d, layer, 1, row_of_batch)
    return pl.pallas_call(
        _about_kernel,
        out_shape=jax.ShapeDtypeStruct(x.shape, F32),
        grid=(bsz, length // tm),
        in_specs=[_tok_spec(tm, d), _tok_spec(tm, dh), _tok_spec(tm, dh), _tok_spec(tm, dh), _tok_spec(tm, dp),
                  gate_spec, _pick_spec(g_norm.shape, (e,)), _const_spec((n_groups, sub_rows, sub_rows)),
                  _const_spec((sub_rows, dp)), _const_spec((dp, dp)), _pick_spec(pool_scale.shape, (e,)),
                  _const_spec(w_out.shape)],
        out_specs=_tok_spec(tm, d),
        compiler_params=_cparams(2),
        name="ab_out_proj",
    )(x, o_f, o_b, g, xa, mod5, g_norm, jnp.asarray(bands, BF16), icnt, w_pool_bd, pool_scale, w_out)


def _gelu_tanh(x):
    k1 = float(-2.0 * np.sqrt(2.0 / np.pi) * np.log2(np.e))
    k2 = 0.044715 * k1
    return x / (1.0 + jnp.exp2(x * (k1 + k2 * (x * x))))


def _gmlp_kernel(x_ref, shift_ref, scale_ref, gate_ref, g_ref, win_ref, lng_ref, lnb_ref, ws_ref, bs_ref,
                 wout_ref, o_ref, u_ref, v_ref, s_ref, *, sub_rows, n_chunk):
    dg = win_ref.shape[1] // 2
    for r in range(x_ref.shape[0] // sub_rows):
        rows = slice(r * sub_rows, (r + 1) * sub_rows)
        x = x_ref[rows, :]
        h = _adaln(x, g_ref[...], shift_ref[...], scale_ref[...]).astype(BF16)
        total = jnp.zeros((sub_rows, 1), F32)
        for c0 in range(0, dg, n_chunk):
            cols = slice(c0, c0 + n_chunk)
            u_ref[rows, cols] = _gelu_tanh(_dot(h, win_ref[:, c0:c0 + n_chunk]))
            zv = _gelu_tanh(_dot(h, win_ref[:, dg + c0:dg + c0 + n_chunk]))
            v_ref[rows, cols] = zv
            total = total + jnp.sum(zv, axis=-1, keepdims=True)
        mean = total / dg
        sq = jnp.zeros((sub_rows, 1), F32)
        for c0 in range(0, dg, n_chunk):
            vc = v_ref[rows, c0:c0 + n_chunk] - mean
            sq = sq + jnp.sum(vc * vc, axis=-1, keepdims=True)
        rstd = lax.rsqrt(sq / dg + EPS)
        for c in range(sub_rows // GMLP_CHUNK):
            crows = slice(c * GMLP_CHUNK, (c + 1) * GMLP_CHUNK)
            srows = slice(r * sub_rows + c * GMLP_CHUNK, r * sub_rows + (c + 1) * GMLP_CHUNK)
            for gi in range(dg // GMLP_GROUP_DIM):
                cols = slice(gi * GMLP_GROUP_DIM, (gi + 1) * GMLP_GROUP_DIM)
                vn = (v_ref[srows, cols] - mean[crows]) * rstd[crows] * lng_ref[:, cols] + lnb_ref[:, cols]
                sv = _dot(ws_ref[gi], vn.astype(BF16)) + bs_ref[gi]
                s_ref[srows, cols] = (u_ref[srows, cols] * sv).astype(BF16)
        o_ref[rows, :] = x + gate_ref[...] * _dot(s_ref[rows, :], wout_ref[...])


def _gmlp(x, mod5, layer, row_of_batch, norm_g4, o, w_in, ln_g, ln_b, w_s, b_s_wide, w_out, *, tm=1024,
          sub_rows=512, n_chunk=512):
    bsz, length, d = x.shape
    dg = w_in.shape[-1] // 2
    tm = min(tm, length)
    kern = functools.partial(_gmlp_kernel, sub_rows=min(sub_rows, tm), n_chunk=n_chunk)
    return pl.pallas_call(
        kern,
        out_shape=jax.ShapeDtypeStruct(x.shape, F32),
        grid=(bsz, length // tm),
        in_specs=[_tok_spec(tm, d)]
        + _mod_block_specs(d, layer, 1, row_of_batch)
        + [_pick_spec(norm_g4.shape, (layer, 1)), _const_spec(w_in.shape), _pick_spec(ln_g.shape, (o,)),
           _pick_spec(ln_b.shape, (o,)), _const_spec(w_s.shape), _pick_spec(b_s_wide.shape, (o,)),
           _const_spec(w_out.shape)],
        out_specs=_tok_spec(tm, d),
        scratch_shapes=[pltpu.VMEM((tm, dg), F32), pltpu.VMEM((tm, dg), F32), pltpu.VMEM((tm, dg), BF16)],
        compiler_params=_cparams(2),
        name="gmlp_mixer",
    )(x, mod5, mod5, mod5, norm_g4, w_in, ln_g, ln_b, w_s, b_s_wide, w_out)


def _block_diag(w):
    n, g, a, b = w.shape
    eye = jnp.eye(g, dtype=w.dtype)
    return (w[:, :, :, None, :] * eye[None, :, None, :, None]).reshape(n, g * a, g * b)


def kernel(x, c, ctx, c_ctx, mod_w, mod_b, norm_g, ffn_w1, ffn_w3, ffn_w2, ab_w_in, pool_w, pool_scale,
           hgrn_lb, hgrn_norm_g, ab_w_out, gmlp_w_in, gmlp_ln_g, gmlp_ln_b, gmlp_w_s, gmlp_b_s, gmlp_w_out,
           final_g):
    bsz, _, d = x.shape
    depth = mod_w.shape[0]
    d_pool = pool_scale.shape[-1]
    d_hgrn = hgrn_norm_g.shape[-1]
    n_heads = d_hgrn // HGRN_HEAD_DIM

    c_rows = jnp.concatenate([c, c_ctx[None, :]], axis=0)
    mod = _modulation(c_rows, mod_w, mod_b)
    mod5 = mod.reshape(depth, bsz + 1, N_MOD, 1, d)
    lat_row = lambda b: b
    ctx_row = lambda b: bsz

    norm_g4 = norm_g[:, :, None, :]
    pool_bd = _block_diag(pool_w).astype(BF16)
    pool_scale3, hgrn_norm_g3 = pool_scale[:, None, :], hgrn_norm_g[:, None, :]
    gm_ln_g, gm_ln_b = gmlp_ln_g[:, None, :], gmlp_ln_b[:, None, :]
    gm_b_wide = jnp.broadcast_to(gmlp_b_s[:, :, :, None], gmlp_b_s.shape + (GMLP_GROUP_DIM,))
    gmlp_w_s2 = gmlp_w_s.reshape(gmlp_w_s.shape[0], -1, gmlp_w_s.shape[-1])

    def ffn_sources(i, j):
        return [(ffn_w1, (i, j)), (ffn_w3, (i, j)), (ffn_w2, (i, j))]

    def mixer_sources(i):
        if i % 2 == 0:
            return [(ab_w_in, (i // 2,)), (ab_w_out, (i // 2,))]
        return [(gmlp_w_in, (i // 2,)), (gmlp_w_s2, (i // 2,)), (gmlp_w_out, (i // 2,))]

    ffn_order = [(i, j) for i in range(depth) for j in range(2)]
    ffn_w = {ffn_order[0]: [arr[lead].astype(BF16) for arr, lead in ffn_sources(*ffn_order[0])]}
    mixer_w = {}

    def latent_ffn(xl, i, j, **kw):
        k = ffn_order.index((i, j))
        cast = ffn_sources(*ffn_order[k + 1]) if k + 1 < len(ffn_order) else []
        n_next = len(cast)
        if j == 0:
            cast = cast + mixer_sources(i)
        xl, done = _ffn(xl, mod5, i, 2 * j, lat_row, norm_g4, *ffn_w[(i, j)], final_g, cast=cast, **kw)
        if n_next:
            ffn_w[ffn_order[k + 1]] = done[:n_next]
        if j == 0:
            mixer_w[i] = done[n_next:]
        return xl

    def context_ffn(xc, i, j):
        return _ffn(xc, mod5, i, 2 * j, ctx_row, norm_g4, *ffn_w[(i, j)], final_g)[0]

    xl, xc = x, ctx
    for i in range(depth):
        ctx_live = any(j % 2 == 0 for j in range(i, depth))
        last = i == depth - 1

        xl = latent_ffn(xl, i, 0)
        if ctx_live:
            xc = context_ffn(xc, i, 0)

        if i % 2 == 0:
            e = i // 2
            ab_in_w, ab_out_w = mixer_w[i]
            ab_in = functools.partial(_ab_in, mod5=mod5, layer=i, norm_g4=norm_g4, w_in=ab_in_w,
                                      hgrn_lb=hgrn_lb, slot=e, d_pool=d_pool)
            q_c, v_c, lf_c, kf_c, lb_c, kb_c, sg_c, a_c = ab_in(xc, row_of_batch=ctx_row)
            q_l, v_l, lf_l, kf_l, lb_l, kb_l, sg_l, a_l = ab_in(xl, row_of_batch=lat_row)
            s0 = jnp.zeros((bsz, 2, n_heads, HGRN_HEAD_DIM, HGRN_HEAD_DIM), F32)
            of_c, ob_c, s_ctx = _hgrn_scan(q_c, v_c, lf_c, kf_c, lb_c, kb_c, s0)
            of_l, ob_l, _ = _hgrn_scan(q_l, v_l, lf_l, kf_l, lb_l, kb_l, s_ctx)
            ab_out = functools.partial(_ab_out, mod5=mod5, layer=i, g_norm=hgrn_norm_g3, e=e,
                                       w_pool_bd=pool_bd[e], pool_scale=pool_scale3, w_out=ab_out_w)
            xc = ab_out(xc, of_c, ob_c, sg_c, a_c, row_of_batch=ctx_row, row_len=xc.shape[1])
            xl = ab_out(xl, of_l, ob_l, sg_l, a_l, row_of_batch=lat_row, row_len=GRID_W)
        else:
            o = i // 2
            gm_in, gm_s2, gm_out = mixer_w[i]
            gmlp = functools.partial(_gmlp, mod5=mod5, layer=i, norm_g4=norm_g4, o=o, w_in=gm_in, ln_g=gm_ln_g,
                                     ln_b=gm_ln_b, w_s=gm_s2.reshape(gmlp_w_s.shape[1:]), b_s_wide=gm_b_wide,
                                     w_out=gm_out)
            xl = gmlp(xl, row_of_batch=lat_row)
            if ctx_live:
                xc = gmlp(xc, row_of_batch=ctx_row)

        xl = latent_ffn(xl, i, 1, final_norm=last)
        if ctx_live:
            xc = context_ffn(xc, i, 1)
    return xl
```

```python
import functools

import numpy as np
import jax
import jax.numpy as jnp
from jax import lax
from jax.experimental import pallas as pl
from jax.experimental.pallas import tpu as pltpu

F32 = jnp.float32
BF16 = jnp.bfloat16

EPS = 1e-6
N_MOD = 9
GRID_W = 64
POOL_WINDOWS = (2, 4, 8, 16)
HGRN_HEAD_DIM = 128
GMLP_CHUNK = 128
GMLP_GROUP_DIM = 128

LOG2_E = float(np.log2(np.e))
HGRN_CHUNK = 128
HGRN_DIAG_BLOCK = 32
HGRN_SAFE_LOG2_SPAN = 80.0

VMEM_LIMIT_BYTES = 56 * 1024 * 1024


def _cparams(n_grid_dims):
    return pltpu.CompilerParams(
        dimension_semantics=("arbitrary",) * n_grid_dims,
        vmem_limit_bytes=VMEM_LIMIT_BYTES)


def _sigmoid(x):
    return 1.0 / (1.0 + jnp.exp2(x * (-LOG2_E)))


def _silu(x):
    return x * _sigmoid(x)


def _rms(x, g):
    return x * lax.rsqrt(jnp.mean(x * x, axis=-1, keepdims=True) + EPS) * g


def _adaln(x, g, shift, scale):
    return _rms(x, g) * (1.0 + scale) + shift


def _dot(a, b):
    return jnp.dot(a, b, preferred_element_type=F32)


def _dot_tn(a, b):
    return lax.dot_general(a, b, (((0,), (0,)), ((), ())), preferred_element_type=F32)


def _const_spec(shape):
    nd = len(shape)
    return pl.BlockSpec(shape, lambda *_: (0,) * nd, pipeline_mode=pl.Buffered(1))


def _pick_spec(full_shape, lead):
    tail = tuple(full_shape[len(lead):])
    idx = tuple(lead) + (0,) * len(tail)
    return pl.BlockSpec((None,) * len(lead) + tail, lambda *_: idx, pipeline_mode=pl.Buffered(1))


def _tok_spec(tm, n):
    return pl.BlockSpec((None, tm, n), lambda b, t: (b, t, 0))


def _mod_kernel(c_ref, w_ref, b_ref, o_ref, a_scr):
    @pl.when((pl.program_id(0) == 0) & (pl.program_id(1) == 0))
    def _():
        a_scr[...] = _silu(c_ref[...])

    rows, d, lanes = a_scr.shape
    tn = w_ref.shape[1]
    n_blk = tn // lanes
    sub = 8

    def body(kc, accs):
        k0 = pl.multiple_of(kc * sub, sub)
        w8 = w_ref[pl.ds(k0, sub), :]
        out = []
        for r in range(rows):
            a8 = a_scr[r, pl.ds(k0, sub), :]
            out.append(tuple(accs[r][j] + a8 * w8[:, j * lanes:(j + 1) * lanes] for j in range(n_blk)))
        return tuple(out)

    zero = jnp.zeros((sub, lanes), F32)
    accs = lax.fori_loop(0, d // sub, body, tuple((zero,) * n_blk for _ in range(rows)), unroll=4)
    for r in range(rows):
        row = jnp.concatenate([jnp.sum(a, axis=0, keepdims=True) for a in accs[r]], axis=1)
        o_ref[r:r + 1, :] = row + b_ref[...]


def _modulation(c_rows, mod_w, mod_b, tn=1152):
    depth, d, n = mod_w.shape
    rows = c_rows.shape[0]
    lanes = 128
    c_wide = jnp.broadcast_to(c_rows[:, :, None], (rows, d, lanes))
    return pl.pallas_call(
        _mod_kernel,
        out_shape=jax.ShapeDtypeStruct((depth, rows, n), F32),
        grid=(depth, n // tn),
        in_specs=[
            _const_spec((rows, d, lanes)),
            pl.BlockSpec((None, d, tn), lambda l, j: (l, 0, j)),
            pl.BlockSpec((None, 1, tn), lambda l, j: (l, 0, j)),
        ],
        out_specs=pl.BlockSpec((None, rows, tn), lambda l, j: (l, 0, j)),
        scratch_shapes=[pltpu.VMEM((rows, d, lanes), F32)],
        compiler_params=_cparams(2),
        name="modulation",
    )(c_wide, mod_w, mod_b.reshape(depth, 1, n))


def _mod_index(layer, col, row_of_batch, b, t):
    return (layer, row_of_batch(b), col, 0, 0)


def _mod_block_specs(d_model, layer, sub, row_of_batch):
    return [pl.BlockSpec((None, None, None, 1, d_model),
                         functools.partial(_mod_index, layer, 3 * sub + k, row_of_batch))
            for k in range(3)]


def _ffn_kernel(x_ref, shift_ref, scale_ref, gate_ref, g_ref, w1_ref, w3_ref, w2_ref, fg_ref, *rest,
                n_cast, n_chunk, sub_rows, final_norm):
    cast_in, o_ref, cast_out, a_ref = rest[:n_cast], rest[n_cast], rest[n_cast + 1:-1], rest[-1]
    for src, dst in zip(cast_in, cast_out):
        dst[...] = src[...].astype(BF16)
    d_ff = w1_ref.shape[1]
    for r in range(x_ref.shape[0] // sub_rows):
        rows = slice(r * sub_rows, (r + 1) * sub_rows)
        x = x_ref[rows, :]
        h = _adaln(x, g_ref[...], shift_ref[...], scale_ref[...]).astype(BF16)
        for c in range(d_ff // n_chunk):
            sl = slice(c * n_chunk, (c + 1) * n_chunk)
            u = _dot(h, w1_ref[:, sl])
            v = _dot(h, w3_ref[:, sl])
            a_ref[rows, sl] = (_silu(u) * v).astype(BF16)
        y = _dot(a_ref[rows, :], w2_ref[...])
        out = x + (0.5 * gate_ref[...]) * y
        if final_norm:
            out = _rms(out, fg_ref[...])
        o_ref[rows, :] = out


def _ffn(x, mod5, layer, sub, row_of_batch, norm_g4, w1, w3, w2, final_g, *, cast=(), final_norm=False,
         tm=1024, n_chunk=256, sub_rows=512):
    bsz, length, d = x.shape
    d_ff = w1.shape[-1]
    tm = min(tm, length)
    n_t = length // tm
    n_steps = bsz * n_t
    bf16_rows = 16
    cast_in_specs, cast_out_specs, cast_shapes = [], [], []
    for arr, lead in cast:
        rows, cols = arr.shape[len(lead):]
        band = rows // n_steps
        assert band * n_steps == rows and band % bf16_rows == 0, (arr.shape, n_steps)
        cast_in_specs.append(pl.BlockSpec(
            (None,) * len(lead) + (band, cols),
            functools.partial(lambda lead, b, t: tuple(lead) + (b * n_t + t, 0), lead)))
        cast_out_specs.append(pl.BlockSpec((band, cols), lambda b, t: (b * n_t + t, 0)))
        cast_shapes.append(jax.ShapeDtypeStruct((rows, cols), BF16))
    kern = functools.partial(_ffn_kernel, n_cast=len(cast), n_chunk=n_chunk, sub_rows=min(sub_rows, tm),
                             final_norm=final_norm)
    out = pl.pallas_call(
        kern,
        out_shape=[jax.ShapeDtypeStruct(x.shape, F32)] + cast_shapes,
        grid=(bsz, n_t),
        in_specs=[_tok_spec(tm, d)]
        + _mod_block_specs(d, layer, sub, row_of_batch)
        + [_pick_spec(norm_g4.shape, (layer, sub)), _const_spec(w1.shape), _const_spec(w3.shape),
           _const_spec(w2.shape), _const_spec((1, d))]
        + cast_in_specs,
        out_specs=[_tok_spec(tm, d)] + cast_out_specs,
        scratch_shapes=[pltpu.VMEM((tm, d_ff), BF16)],
        compiler_params=_cparams(2),
        name="swiglu_ffn",
    )(x, mod5, mod5, mod5, norm_g4, w1, w3, w2, final_g.reshape(1, d), *[arr for arr, _ in cast])
    return out[0], list(out[1:])


def _hgrn_lower_bounds(raw, slot):
    ex = jnp.exp(raw - jnp.max(raw, axis=0, keepdims=True))
    sm = ex / jnp.sum(ex, axis=0, keepdims=True)
    lb = sm[0]
    for k in range(1, slot + 1):
        lb = lb + sm[k]
    return lb


def _abin_kernel(x_ref, shift_ref, scale_ref, g_ref, w_ref, lb_ref, q_ref, v_ref, lff_ref, kf_ref, lfb_ref,
                 kb_ref, sg_ref, p_ref, *, slot, sub_rows, n_chunk):
    dp, dh = p_ref.shape[-1], q_ref.shape[-1]
    lb_all = _hgrn_lower_bounds(lb_ref[...], slot)
    for r in range(x_ref.shape[0] // sub_rows):
        rows = slice(r * sub_rows, (r + 1) * sub_rows)
        h = _adaln(x_ref[rows, :], g_ref[...], shift_ref[...], scale_ref[...]).astype(BF16)

        p_ref[rows, :] = _dot(h, w_ref[:, 0:dp]).astype(BF16)
        for k in range(5):
            for c0 in range(0, dh, n_chunk):
                cols = slice(c0, c0 + n_chunk)
                z = _dot(h, w_ref[:, dp + k * dh + c0:dp + k * dh + c0 + n_chunk])
                if k == 0:
                    q_ref[rows, cols] = (_silu(z) * HGRN_HEAD_DIM ** -0.5).astype(BF16)
                elif k == 1:
                    v_ref[rows, cols] = z.astype(BF16)
                elif k == 4:
                    sg_ref[rows, cols] = _silu(z).astype(BF16)
                else:
                    lf_ref, kk_ref = ((lff_ref, kf_ref), (lfb_ref, kb_ref))[k - 2]
                    lb = lb_all[k - 2:k - 1, cols]
                    forget = lb + (1.0 - lb) * _sigmoid(z)
                    lf_ref[rows, cols] = jnp.log(forget) * LOG2_E
                    kk_ref[rows, cols] = (1.0 - forget).astype(BF16)


def _ab_in(x, mod5, layer, row_of_batch, norm_g4, w_in, hgrn_lb, slot, d_pool, *, tm=1024, sub_rows=256,
           n_chunk=768):
    bsz, length, d = x.shape
    tm = min(tm, length)
    dh = (w_in.shape[-1] - d_pool) // 5
    shift_spec, scale_spec, _ = _mod_block_specs(d, layer, 1, row_of_batch)
    kern = functools.partial(_abin_kernel, slot=slot, sub_rows=min(sub_rows, tm), n_chunk=n_chunk)
    widths = (dh, dh, dh, dh, dh, dh, dh, d_pool)
    dtypes = (BF16, BF16, F32, BF16, F32, BF16, BF16, BF16)
    return pl.pallas_call(
        kern,
        out_shape=[jax.ShapeDtypeStruct((bsz, length, n), dt) for n, dt in zip(widths, dtypes)],
        grid=(bsz, length // tm),
        in_specs=[_tok_spec(tm, d), shift_spec, scale_spec, _pick_spec(norm_g4.shape, (layer, 1)),
                  _const_spec(w_in.shape), _const_spec(hgrn_lb.shape)],
        out_specs=[_tok_spec(tm, n) for n in widths],
        compiler_params=_cparams(2),
        name="ab_in_proj",
    )(x, mod5, mod5, norm_g4, w_in, hgrn_lb)


def _level_table(c, diag_block):
    j = np.arange(c)[:, None]
    i = np.arange(c)[None, :]
    x = j ^ i
    bits = np.zeros((c, c), np.int32)
    for k in range(int(np.log2(c))):
        bits = np.where(x >= (1 << k), k + 1, bits)
    lvl = np.where(i < j, bits, 0)
    if diag_block > 1:
        lvl = np.where((i <= j) & (bits <= int(np.log2(diag_block))), int(np.log2(diag_block)), lvl)
    return lvl.astype(np.int32)


def _fine_decay(forget, rev):
    c, w = forget.shape
    f3 = forget.reshape(c // 8, 8, w)
    pos = lax.broadcasted_iota(jnp.int32, (c // 8, 8, w), 1)

    def at(offset):
        return f3 if offset == 0 else pltpu.roll(f3, (-offset) % 8, 1)

    sgn = 1 if rev else -1
    cq = [at(0)]
    ck = [None]
    for t in range(1, 4):
        cq.append(cq[-1] * at(sgn * t))
        ck.append(at(-sgn * t) if ck[-1] is None else ck[-1] * at(-sgn * t))
    out = {}
    for s in (2, 4, 8):
        half = s // 2
        p = pos % s
        e = jnp.ones_like(f3)
        for t in range(half):
            q_pos = (half - 1 - t) if rev else (half + t)
            k_pos = (half + t) if rev else (half - 1 - t)
            e = jnp.where(p == q_pos, cq[t], e)
            if t > 0:
                e = jnp.where(p == k_pos, ck[t], e)
        out[s] = e.reshape(c, w)
    return out


def _block_row(b, s, ridx):
    c, w = b.shape
    r = b.reshape(c // s, s, w)[:, ridx:ridx + 1, :]
    return jnp.broadcast_to(r, (c // s, s, w)).reshape(c, w)


def _key_decay_column(b_end, shape):
    return jnp.broadcast_to(jnp.exp2(b_end), shape).T


def _hgrn_head(qq, kk, v, forget, b, st, lvl, rev):
    c = qq.shape[0]
    b_end = b[0:1] if rev else b[c - 1:c]
    vb = v.astype(BF16)
    o = _dot((qq * jnp.exp2(b)).astype(BF16), st.astype(BF16))
    k_out = (kk * jnp.exp2(b_end - b)).astype(BF16)
    st_new = st * _key_decay_column(b_end, st.shape) + _dot_tn(k_out, vb)

    row = lax.broadcasted_iota(jnp.int32, (c, 1), 0)
    fine = _fine_decay(forget, rev)
    a = jnp.zeros((c, c), F32)
    s = c
    while s > 1:
        half = s // 2
        q_side = ((row % s) >= half) != rev
        if s >= 16:
            e = jnp.exp2(-jnp.abs(b - _block_row(b, s, half if rev else half - 1)))
        else:
            e = fine[s]
        x = jnp.where(q_side, qq, kk) * e
        a = jnp.where(lvl == int(np.log2(s)), _dot(x.astype(BF16), x.T.astype(BF16)), a)
        s //= 2
    o = o + _dot(a.astype(BF16), vb) + jnp.sum(qq * kk, axis=-1, keepdims=True) * v.astype(F32)
    return o, st_new


def _hgrn_mild_direction(rev, qq_ref, kk_ref, b_ref, v_ref, o_ref, st_ref, lvl, ops_ref, a_ref):
    c = qq_ref.shape[0]
    n_heads = st_ref.shape[0]
    row = lax.broadcasted_iota(jnp.int32, (c, 1), 0)
    sizes = []
    s = c
    while s > HGRN_DIAG_BLOCK:
        sizes.append(s)
        s //= 2

    for h in range(n_heads):
        sl = slice(h * HGRN_HEAD_DIM, (h + 1) * HGRN_HEAD_DIM)
        qq, kk, b = qq_ref[:, sl].astype(F32), kk_ref[:, sl].astype(F32), b_ref[:, sl]
        b_end = b[0:1] if rev else b[c - 1:c]
        st = st_ref[h]
        ops_ref[h, 0] = (qq * jnp.exp2(b)).astype(BF16)
        ops_ref[h, 1] = (kk * jnp.exp2(b_end - b)).astype(BF16)
        ops_ref[h, 2] = st.astype(BF16)
        st_ref[h] = st * _key_decay_column(b_end, st.shape)
        for n, s in enumerate(sizes):
            half = s // 2
            q_side = ((row % s) >= half) != rev
            e = jnp.exp2(-jnp.abs(b - _block_row(b, s, half if rev else half - 1)))
            x = jnp.where(q_side, qq, kk) * e
            ops_ref[h, 3 + 2 * n] = x.astype(BF16)
            ops_ref[h, 4 + 2 * n] = x.T.astype(BF16)
        d = b - _block_row(b, HGRN_DIAG_BLOCK, HGRN_DIAG_BLOCK - 1 if rev else 0)
        ops_ref[h, 3 + 2 * len(sizes)] = (qq * jnp.exp2(d)).astype(BF16)
        ops_ref[h, 4 + 2 * len(sizes)] = (kk * jnp.exp2(-d)).T.astype(BF16)

    for h in range(n_heads):
        a = jnp.zeros((c, c), F32)
        for n, s in enumerate(sizes + [HGRN_DIAG_BLOCK]):
            a = jnp.where(lvl == int(np.log2(s)), _dot(ops_ref[h, 3 + 2 * n], ops_ref[h, 4 + 2 * n]), a)
        a_ref[h] = a.astype(BF16)

    for h in range(n_heads):
        sl = slice(h * HGRN_HEAD_DIM, (h + 1) * HGRN_HEAD_DIM)
        vb = v_ref[:, sl].astype(BF16)
        o = _dot(ops_ref[h, 0], ops_ref[h, 2]) + _dot(a_ref[h], vb)
        o_ref[:, sl] = o.astype(o_ref.dtype)
        st_ref[h] = st_ref[h] + _dot_tn(ops_ref[h, 1], vb)


def _hgrn_kernel(qf_ref, vf_ref, lf_ref, kf_ref, qb_ref, vb_ref, lb_ref, kb_ref, s0_ref, tri_ref, lvl_ref,
                 of_ref, ob_ref, sout_ref, s_scr, b_s, span_s, ops_s, a_s):
    t = pl.program_id(1)

    @pl.when(t == 0)
    def _():
        s_scr[...] = s0_ref[...]

    c = HGRN_CHUNK
    n_sub = qf_ref.shape[0] // c
    n_heads = qf_ref.shape[-1] // HGRN_HEAD_DIM

    for i in range(n_sub):
        weakest = None
        for d, (logf_ref, off) in enumerate(((lf_ref, i * c), (lb_ref, (n_sub - 1 - i) * c))):
            logf = logf_ref[off:off + c, :]
            hi = logf.astype(BF16)
            lo = (logf - hi.astype(F32)).astype(BF16)
            b = _dot(tri_ref[d], jnp.concatenate([hi, lo], axis=0))
            b_s[i, d] = b
            for m in range(c // HGRN_DIAG_BLOCK):
                lo_row, hi_row = m * HGRN_DIAG_BLOCK, (m + 1) * HGRN_DIAG_BLOCK - 1
                if d == 0:
                    span = b[hi_row:hi_row + 1] - b[lo_row:lo_row + 1]
                else:
                    span = b[lo_row:lo_row + 1] - b[hi_row:hi_row + 1]
                weakest = span if weakest is None else jnp.minimum(weakest, span)
        span_s[i] = jnp.min(weakest)

    def chunk(i, carry):
        offs = (pl.multiple_of(i * c, c), pl.multiple_of((n_sub - 1 - i) * c, c))
        dirs = tuple(tuple(ref.at[pl.ds(off, c), :] for ref in refs)
                     for off, refs in zip(offs, ((qf_ref, vf_ref, lf_ref, kf_ref, of_ref),
                                                 (qb_ref, vb_ref, lb_ref, kb_ref, ob_ref))))
        in_range = span_s[i] > -HGRN_SAFE_LOG2_SPAN

        @pl.when(in_range)
        def _():
            for d, (q_ref, v_ref, logf_ref, kk_ref, o_ref) in enumerate(dirs):
                _hgrn_mild_direction(d == 1, q_ref, kk_ref, b_s.at[i, d], v_ref, o_ref, s_scr.at[d],
                                     lvl_ref[d], ops_s.at[d], a_s.at[d])

        @pl.when(jnp.logical_not(in_range))
        def _():
            for d, (q_ref, v_ref, logf_ref, kk_ref, o_ref) in enumerate(dirs):
                lvl = lvl_ref[2 + d]
                for h in range(n_heads):
                    sl = slice(h * HGRN_HEAD_DIM, (h + 1) * HGRN_HEAD_DIM)
                    o_h, st_new = _hgrn_head(q_ref[:, sl].astype(F32), kk_ref[:, sl].astype(F32), v_ref[:, sl],
                                             jnp.exp2(logf_ref[:, sl]), b_s[i, d, :, sl], s_scr[d, h], lvl,
                                             d == 1)
                    o_ref[:, sl] = o_h.astype(o_ref.dtype)
                    s_scr[d, h] = st_new

        return carry

    lax.fori_loop(0, n_sub, chunk, 0)

    @pl.when(t == pl.num_programs(1) - 1)
    def _():
        sout_ref[...] = s_scr[...]


def _hgrn_scan(q, v, logf_fwd, kk_fwd, logf_bwd, kk_bwd, s0, *, rows=512):
    bsz, length, dh = q.shape
    c = HGRN_CHUNK
    rows = min(rows, length)
    n = length // rows
    n_heads = dh // HGRN_HEAD_DIM
    assert c == HGRN_HEAD_DIM and c > HGRN_DIAG_BLOCK
    n_mild = int(np.log2(c // HGRN_DIAG_BLOCK)) + 1
    lvl_fast = _level_table(c, HGRN_DIAG_BLOCK)
    lvl_full = _level_table(c, 1)
    lvl = jnp.asarray(np.stack([lvl_fast, lvl_fast.T, lvl_full, lvl_full.T]))
    tri_f = np.tril(np.ones((c, c), np.float32))
    tri = jnp.asarray(np.stack([np.tile(tri_f, (1, 2)), np.tile(tri_f.T, (1, 2))]), BF16)
    fwd = pl.BlockSpec((None, rows, dh), lambda b, t: (b, t, 0))
    bwd = pl.BlockSpec((None, rows, dh), lambda b, t: (b, n - 1 - t, 0))
    s_shape = (2, n_heads, HGRN_HEAD_DIM, HGRN_HEAD_DIM)
    s_spec = pl.BlockSpec((None,) + s_shape, lambda b, t: (b, 0, 0, 0, 0))
    return pl.pallas_call(
        _hgrn_kernel,
        out_shape=[jax.ShapeDtypeStruct(q.shape, BF16), jax.ShapeDtypeStruct(q.shape, BF16),
                   jax.ShapeDtypeStruct((bsz,) + s_shape, F32)],
        grid=(bsz, n),
        in_specs=[fwd, fwd, fwd, fwd, bwd, bwd, bwd, bwd, s_spec, _const_spec((2, c, 2 * c)),
                  _const_spec((4, c, c))],
        out_specs=[fwd, bwd, s_spec],
        scratch_shapes=[pltpu.VMEM(s_shape, F32), pltpu.VMEM((rows // c, 2, c, dh), F32),
                        pltpu.SMEM((rows // c,), F32),
                        pltpu.VMEM((2, n_heads, 3 + 2 * n_mild, c, HGRN_HEAD_DIM), BF16),
                        pltpu.VMEM((2, n_heads, c, c), BF16)],
        compiler_params=_cparams(2),
        name="hgrn2_scan",
    )(q, v, logf_fwd, kk_fwd, q, v, logf_bwd, kk_bwd, s0, tri, lvl)


def _pool_tables(tm, row_len):
    t = np.arange(tm)
    same_row = (t[:, None] // row_len) == (t[None, :] // row_len)
    pos = t % row_len
    bands, inv = [], []
    for w in POOL_WINDOWS:
        lo = np.clip(pos - w // 2, 0, row_len)
        hi = np.clip(pos - w // 2 + w, 0, row_len)
        u = pos[None, :]
        bands.append(same_row & (u >= lo[:, None]) & (u < hi[:, None]))
        inv.append(1.0 / (hi - lo).astype(np.float64))
    return np.stack(bands).astype(np.float32), np.stack(inv, axis=1).astype(np.float32)


def _about_kernel(x_ref, of_ref, ob_ref, sg_ref, a_ref, gate_ref, gn_ref, band_ref, icnt_ref, wp_ref,
                  ps_ref, wo_ref, o_ref):
    sub_rows = band_ref.shape[-1]
    dp = a_ref.shape[-1]
    n_groups = band_ref.shape[0]
    for r in range(x_ref.shape[0] // sub_rows):
        rows = slice(r * sub_rows, (r + 1) * sub_rows)
        o = of_ref[rows, :].astype(F32) + ob_ref[rows, :].astype(F32)
        heads = []
        for h in range(o.shape[-1] // HGRN_HEAD_DIM):
            oh = o[:, h * HGRN_HEAD_DIM:(h + 1) * HGRN_HEAD_DIM]
            heads.append(oh * lax.rsqrt(jnp.mean(oh * oh, axis=-1, keepdims=True) + EPS))
        b_mix = jnp.concatenate(heads, axis=-1) * gn_ref[...] * sg_ref[rows, :].astype(F32)

        xab = a_ref[rows, :].astype(BF16)
        xa = xab.astype(F32)
        lane_group = lax.broadcasted_iota(jnp.int32, xa.shape, 1) // (dp // n_groups)
        total = jnp.zeros_like(xa)
        for gi in range(n_groups):
            total = jnp.where(lane_group == gi, _dot(band_ref[gi], xab), total)
        y = total * icnt_ref[...] - xa
        a_mix = _dot(y.astype(BF16), wp_ref[...]) * ps_ref[...]

        mix = _dot(a_mix.astype(BF16), wo_ref[0:dp, :]) + _dot(b_mix.astype(BF16), wo_ref[dp:, :])
        o_ref[rows, :] = x_ref[rows, :] + gate_ref[...] * mix


def _ab_out(x, o_f, o_b, g, xa, mod5, layer, row_of_batch, g_norm, e, w_pool_bd, pool_scale, w_out, row_len,
            *, tm=512, sub_rows=256):
    bsz, length, d = x.shape
    dh = o_f.shape[-1]
    dp = xa.shape[-1]
    tm = min(tm, length)
    sub_rows = min(sub_rows, tm)
    n_groups = len(POOL_WINDOWS)
    bands, inv = _pool_tables(sub_rows, row_len)
    icnt = jnp.asarray(np.repeat(inv, dp // n_groups, axis=1))
    _, _, gate_spec = _mod_block_specs(d, layer, 1, row_of_batch)
    return pl.pallas_call(
        _about_kernel,
        out_shape=jax.ShapeDtypeStruct(x.shape, F32),
        grid=(bsz, length // tm),
        in_specs=[_tok_spec(tm, d), _tok_spec(tm, dh), _tok_spec(tm, dh), _tok_spec(tm, dh), _tok_spec(tm, dp),
                  gate_spec, _pick_spec(g_norm.shape, (e,)), _const_spec((n_groups, sub_rows, sub_rows)),
                  _const_spec((sub_rows, dp)), _const_spec((dp, dp)), _pick_spec(pool_scale.shape, (e,)),
                  _const_spec(w_out.shape)],
        out_specs=_tok_spec(tm, d),
        compiler_params=_cparams(2),
        name="ab_out_proj",
    )(x, o_f, o_b, g, xa, mod5, g_norm, jnp.asarray(bands, BF16), icnt, w_pool_bd, pool_scale, w_out)


def _gelu_tanh(x):
    k1 = float(-2.0 * np.sqrt(2.0 / np.pi) * np.log2(np.e))
    k2 = 0.044715 * k1
    return x / (1.0 + jnp.exp2(x * (k1 + k2 * (x * x))))


def _gmlp_kernel(x_ref, shift_ref, scale_ref, gate_ref, g_ref, win_ref, lng_ref, lnb_ref, ws_ref, bs_ref,
                 wout_ref, o_ref, u_ref, v_ref, s_ref, *, sub_rows, n_chunk):
    dg = win_ref.shape[1] // 2
    for r in range(x_ref.shape[0] // sub_rows):
        rows = slice(r * sub_rows, (r + 1) * sub_rows)
        x = x_ref[rows, :]
        h = _adaln(x, g_ref[...], shift_ref[...], scale_ref[...]).astype(BF16)
        total = jnp.zeros((sub_rows, 1), F32)
        for c0 in range(0, dg, n_chunk):
            cols = slice(c0, c0 + n_chunk)
            u_ref[rows, cols] = _gelu_tanh(_dot(h, win_ref[:, c0:c0 + n_chunk]))
            zv = _gelu_tanh(_dot(h, win_ref[:, dg + c0:dg + c0 + n_chunk]))
            v_ref[rows, cols] = zv
            total = total + jnp.sum(zv, axis=-1, keepdims=True)
        mean = total / dg
        sq = jnp.zeros((sub_rows, 1), F32)
        for c0 in range(0, dg, n_chunk):
            vc = v_ref[rows, c0:c0 + n_chunk] - mean
            sq = sq + jnp.sum(vc * vc, axis=-1, keepdims=True)
        rstd = lax.rsqrt(sq / dg + EPS)
        for c in range(sub_rows // GMLP_CHUNK):
            crows = slice(c * GMLP_CHUNK, (c + 1) * GMLP_CHUNK)
            srows = slice(r * sub_rows + c * GMLP_CHUNK, r * sub_rows + (c + 1) * GMLP_CHUNK)
            for gi in range(dg // GMLP_GROUP_DIM):
                cols = slice(gi * GMLP_GROUP_DIM, (gi + 1) * GMLP_GROUP_DIM)
                vn = (v_ref[srows, cols] - mean[crows]) * rstd[crows] * lng_ref[:, cols] + lnb_ref[:, cols]
                sv = _dot(ws_ref[gi], vn.astype(BF16)) + bs_ref[gi]
                s_ref[srows, cols] = (u_ref[srows, cols] * sv).astype(BF16)
        o_ref[rows, :] = x + gate_ref[...] * _dot(s_ref[rows, :], wout_ref[...])


def _gmlp(x, mod5, layer, row_of_batch, norm_g4, o, w_in, ln_g, ln_b, w_s, b_s_wide, w_out, *, tm=1024,
          sub_rows=512, n_chunk=512):
    bsz, length, d = x.shape
    dg = w_in.shape[-1] // 2
    tm = min(tm, length)
    kern = functools.partial(_gmlp_kernel, sub_rows=min(sub_rows, tm), n_chunk=n_chunk)
    return pl.pallas_call(
        kern,
        out_shape=jax.ShapeDtypeStruct(x.shape, F32),
        grid=(bsz, length // tm),
        in_specs=[_tok_spec(tm, d)]
        + _mod_block_specs(d, layer, 1, row_of_batch)
        + [_pick_spec(norm_g4.shape, (layer, 1)), _const_spec(w_in.shape), _pick_spec(ln_g.shape, (o,)),
           _pick_spec(ln_b.shape, (o,)), _const_spec(w_s.shape), _pick_spec(b_s_wide.shape, (o,)),
           _const_spec(w_out.shape)],
        out_specs=_tok_spec(tm, d),
        scratch_shapes=[pltpu.VMEM((tm, dg), F32), pltpu.VMEM((tm, dg), F32), pltpu.VMEM((tm, dg), BF16)],
        compiler_params=_cparams(2),
        name="gmlp_mixer",
    )(x, mod5, mod5, mod5, norm_g4, w_in, ln_g, ln_b, w_s, b_s_wide, w_out)


def _block_diag(w):
    n, g, a, b = w.shape
    eye = jnp.eye(g, dtype=w.dtype)
    return (w[:, :, :, None, :] * eye[None, :, None, :, None]).reshape(n, g * a, g * b)


def kernel(x, c, ctx, c_ctx, mod_w, mod_b, norm_g, ffn_w1, ffn_w3, ffn_w2, ab_w_in, pool_w, pool_scale,
           hgrn_lb, hgrn_norm_g, ab_w_out, gmlp_w_in, gmlp_ln_g, gmlp_ln_b, gmlp_w_s, gmlp_b_s, gmlp_w_out,
           final_g):
    bsz, _, d = x.shape
    depth = mod_w.shape[0]
    d_pool = pool_scale.shape[-1]
    d_hgrn = hgrn_norm_g.shape[-1]
    n_heads = d_hgrn // HGRN_HEAD_DIM

    c_rows = jnp.concatenate([c, c_ctx[None, :]], axis=0)
    mod = _modulation(c_rows, mod_w, mod_b)
    mod5 = mod.reshape(depth, bsz + 1, N_MOD, 1, d)
    lat_row = lambda b: b
    ctx_row = lambda b: bsz

    norm_g4 = norm_g[:, :, None, :]
    pool_bd = _block_diag(pool_w).astype(BF16)
    pool_scale3, hgrn_norm_g3 = pool_scale[:, None, :], hgrn_norm_g[:, None, :]
    gm_ln_g, gm_ln_b = gmlp_ln_g[:, None, :], gmlp_ln_b[:, None, :]
    gm_b_wide = jnp.broadcast_to(gmlp_b_s[:, :, :, None], gmlp_b_s.shape + (GMLP_GROUP_DIM,))
    gmlp_w_s2 = gmlp_w_s.reshape(gmlp_w_s.shape[0], -1, gmlp_w_s.shape[-1])

    def ffn_sources(i, j):
        return [(ffn_w1, (i, j)), (ffn_w3, (i, j)), (ffn_w2, (i, j))]

    def mixer_sources(i):
        if i % 2 == 0:
            return [(ab_w_in, (i // 2,)), (ab_w_out, (i // 2,))]
        return [(gmlp_w_in, (i // 2,)), (gmlp_w_s2, (i // 2,)), (gmlp_w_out, (i // 2,))]

    ffn_order = [(i, j) for i in range(depth) for j in range(2)]
    ffn_w = {ffn_order[0]: [arr[lead].astype(BF16) for arr, lead in ffn_sources(*ffn_order[0])]}
    mixer_w = {}

    def latent_ffn(xl, i, j, **kw):
        k = ffn_order.index((i, j))
        cast = ffn_sources(*ffn_order[k + 1]) if k + 1 < len(ffn_order) else []
        n_next = len(cast)
        if j == 0:
            cast = cast + mixer_sources(i)
        xl, done = _ffn(xl, mod5, i, 2 * j, lat_row, norm_g4, *ffn_w[(i, j)], final_g, cast=cast, **kw)
        if n_next:
            ffn_w[ffn_order[k + 1]] = done[:n_next]
        if j == 0:
            mixer_w[i] = done[n_next:]
        return xl

    def context_ffn(xc, i, j):
        return _ffn(xc, mod5, i, 2 * j, ctx_row, norm_g4, *ffn_w[(i, j)], final_g)[0]

    xl, xc = x, ctx
    for i in range(depth):
        ctx_live = any(j % 2 == 0 for j in range(i, depth))
        last = i == depth - 1

        xl = latent_ffn(xl, i, 0)
        if ctx_live:
            xc = context_ffn(xc, i, 0)

        if i % 2 == 0:
            e = i // 2
            ab_in_w, ab_out_w = mixer_w[i]
            ab_in = functools.partial(_ab_in, mod5=mod5, layer=i, norm_g4=norm_g4, w_in=ab_in_w,
                                      hgrn_lb=hgrn_lb, slot=e, d_pool=d_pool)
            q_c, v_c, lf_c, kf_c, lb_c, kb_c, sg_c, a_c = ab_in(xc, row_of_batch=ctx_row)
            q_l, v_l, lf_l, kf_l, lb_l, kb_l, sg_l, a_l = ab_in(xl, row_of_batch=lat_row)
            s0 = jnp.zeros((bsz, 2, n_heads, HGRN_HEAD_DIM, HGRN_HEAD_DIM), F32)
            of_c, ob_c, s_ctx = _hgrn_scan(q_c, v_c, lf_c, kf_c, lb_c, kb_c, s0)
            of_l, ob_l, _ = _hgrn_scan(q_l, v_l, lf_l, kf_l, lb_l, kb_l, s_ctx)
            ab_out = functools.partial(_ab_out, mod5=mod5, layer=i, g_norm=hgrn_norm_g3, e=e,
                                       w_pool_bd=pool_bd[e], pool_scale=pool_scale3, w_out=ab_out_w)
            xc = ab_out(xc, of_c, ob_c, sg_c, a_c, row_of_batch=ctx_row, row_len=xc.shape[1])
            xl = ab_out(xl, of_l, ob_l, sg_l, a_l, row_of_batch=lat_row, row_len=GRID_W)
        else:
            o = i // 2
            gm_in, gm_s2, gm_out = mixer_w[i]
            gmlp = functools.partial(_gmlp, mod5=mod5, layer=i, norm_g4=norm_g4, o=o, w_in=gm_in, ln_g=gm_ln_g,
                                     ln_b=gm_ln_b, w_s=gm_s2.reshape(gmlp_w_s.shape[1:]), b_s_wide=gm_b_wide,
                                     w_out=gm_out)
            xl = gmlp(xl, row_of_batch=lat_row)
            if ctx_live:
                xc = gmlp(xc, row_of_batch=ctx_row)

        xl = latent_ffn(xl, i, 1, final_norm=last)
        if ctx_live:
            xc = context_ffn(xc, i, 1)
    return xl
```

```python
import functools

import numpy as np
import jax
import jax.numpy as jnp
from jax import lax
from jax.experimental import pallas as pl
from jax.experimental.pallas import tpu as pltpu

F32 = jnp.float32
BF16 = jnp.bfloat16

EPS = 1e-6
N_MOD = 9
GRID_W = 64
POOL_WINDOWS = (2, 4, 8, 16)
HGRN_HEAD_DIM = 128
GMLP_CHUNK = 128
GMLP_GROUP_DIM = 128

LOG2_E = float(np.log2(np.e))
HGRN_CHUNK = 128
HGRN_DIAG_BLOCK = 32
HGRN_SAFE_LOG2_SPAN = 80.0

VMEM_LIMIT_BYTES = 56 * 1024 * 1024


def _cparams(n_grid_dims):
    return pltpu.CompilerParams(
        dimension_semantics=("arbitrary",) * n_grid_dims,
        vmem_limit_bytes=VMEM_LIMIT_BYTES)


def _sigmoid(x):
    return 1.0 / (1.0 + jnp.exp2(x * (-LOG2_E)))


def _silu(x):
    return x * _sigmoid(x)


def _rms(x, g):
    return x * lax.rsqrt(jnp.mean(x * x, axis=-1, keepdims=True) + EPS) * g


def _adaln(x, g, shift, scale):
    return _rms(x, g) * (1.0 + scale) + shift


def _dot(a, b):
    return jnp.dot(a, b, preferred_element_type=F32)


def _dot_tn(a, b):
    return lax.dot_general(a, b, (((0,), (0,)), ((), ())), preferred_element_type=F32)


def _const_spec(shape):
    nd = len(shape)
    return pl.BlockSpec(shape, lambda *_: (0,) * nd, pipeline_mode=pl.Buffered(1))


def _pick_spec(full_shape, lead):
    tail = tuple(full_shape[len(lead):])
    idx = tuple(lead) + (0,) * len(tail)
    return pl.BlockSpec((None,) * len(lead) + tail, lambda *_: idx, pipeline_mode=pl.Buffered(1))


def _tok_spec(tm, n):
    return pl.BlockSpec((None, tm, n), lambda b, t: (b, t, 0))


def _mod_kernel(c_ref, w_ref, b_ref, o_ref, a_scr, *, col_group):
    @pl.when((pl.program_id(0) == 0) & (pl.program_id(1) == 0))
    def _():
        a_scr[...] = _silu(c_ref[...])

    rows, d, lanes = a_scr.shape
    tn = w_ref.shape[1]
    n_blk = col_group // lanes
    sub = 8

    for g0 in range(0, tn, col_group):
        def body(kc, accs, g0=g0):
            k0 = pl.multiple_of(kc * sub, sub)
            w8 = w_ref[pl.ds(k0, sub), g0:g0 + col_group]
            out = []
            for r in range(rows):
                a8 = a_scr[r, pl.ds(k0, sub), :]
                out.append(tuple(accs[r][j] + a8 * w8[:, j * lanes:(j + 1) * lanes] for j in range(n_blk)))
            return tuple(out)

        zero = jnp.zeros((sub, lanes), F32)
        accs = lax.fori_loop(0, d // sub, body, tuple((zero,) * n_blk for _ in range(rows)), unroll=4)
        for r in range(rows):
            row = jnp.concatenate([jnp.sum(a, axis=0, keepdims=True) for a in accs[r]], axis=1)
            o_ref[r:r + 1, g0:g0 + col_group] = row + b_ref[:, g0:g0 + col_group]


def _modulation(c_rows, mod_w, mod_b, tn=4608, col_group=1152):
    depth, d, n = mod_w.shape
    rows = c_rows.shape[0]
    lanes = 128
    c_wide = jnp.broadcast_to(c_rows[:, :, None], (rows, d, lanes))
    return pl.pallas_call(
        functools.partial(_mod_kernel, col_group=col_group),
        out_shape=jax.ShapeDtypeStruct((depth, rows, n), F32),
        grid=(depth, n // tn),
        in_specs=[
            _const_spec((rows, d, lanes)),
            pl.BlockSpec((None, d, tn), lambda l, j: (l, 0, j)),
            pl.BlockSpec((None, 1, tn), lambda l, j: (l, 0, j)),
        ],
        out_specs=pl.BlockSpec((None, rows, tn), lambda l, j: (l, 0, j)),
        scratch_shapes=[pltpu.VMEM((rows, d, lanes), F32)],
        compiler_params=_cparams(2),
        name="modulation",
    )(c_wide, mod_w, mod_b.reshape(depth, 1, n))


def _mod_index(layer, col, row_of_batch, b, t):
    return (layer, row_of_batch(b), col, 0, 0)


def _mod_block_specs(d_model, layer, sub, row_of_batch):
    return [pl.BlockSpec((None, None, None, 1, d_model),
                         functools.partial(_mod_index, layer, 3 * sub + k, row_of_batch))
            for k in range(3)]


def _ffn_kernel(x_ref, shift_ref, scale_ref, gate_ref, g_ref, w1_ref, w3_ref, w2_ref, fg_ref, *rest,
                n_cast, n_chunk, sub_rows, final_norm):
    cast_in, o_ref, cast_out, a_ref = rest[:n_cast], rest[n_cast], rest[n_cast + 1:-1], rest[-1]
    for src, dst in zip(cast_in, cast_out):
        dst[...] = src[...].astype(BF16)
    d_ff = w1_ref.shape[1]
    for r in range(x_ref.shape[0] // sub_rows):
        rows = slice(r * sub_rows, (r + 1) * sub_rows)
        x = x_ref[rows, :]
        h = _adaln(x, g_ref[...], shift_ref[...], scale_ref[...]).astype(BF16)
        for c in range(d_ff // n_chunk):
            sl = slice(c * n_chunk, (c + 1) * n_chunk)
            u = _dot(h, w1_ref[:, sl])
            v = _dot(h, w3_ref[:, sl])
            a_ref[rows, sl] = (_silu(u) * v).astype(BF16)
        y = _dot(a_ref[rows, :], w2_ref[...])
        out = x + (0.5 * gate_ref[...]) * y
        if final_norm:
            out = _rms(out, fg_ref[...])
        o_ref[rows, :] = out


def _ffn(x, mod5, layer, sub, row_of_batch, norm_g4, w1, w3, w2, final_g, *, cast=(), final_norm=False,
         tm=1024, n_chunk=256, sub_rows=256):
    bsz, length, d = x.shape
    d_ff = w1.shape[-1]
    tm = min(tm, length)
    n_t = length // tm
    n_steps = bsz * n_t
    bf16_rows = 16
    cast_in_specs, cast_out_specs, cast_shapes = [], [], []
    for arr, lead in cast:
        rows, cols = arr.shape[len(lead):]
        band = rows // n_steps
        assert band * n_steps == rows and band % bf16_rows == 0, (arr.shape, n_steps)
        cast_in_specs.append(pl.BlockSpec(
            (None,) * len(lead) + (band, cols),
            functools.partial(lambda lead, b, t: tuple(lead) + (b * n_t + t, 0), lead)))
        cast_out_specs.append(pl.BlockSpec((band, cols), lambda b, t: (b * n_t + t, 0)))
        cast_shapes.append(jax.ShapeDtypeStruct((rows, cols), BF16))
    kern = functools.partial(_ffn_kernel, n_cast=len(cast), n_chunk=n_chunk, sub_rows=min(sub_rows, tm),
                             final_norm=final_norm)
    out = pl.pallas_call(
        kern,
        out_shape=[jax.ShapeDtypeStruct(x.shape, F32)] + cast_shapes,
        grid=(bsz, n_t),
        in_specs=[_tok_spec(tm, d)]
        + _mod_block_specs(d, layer, sub, row_of_batch)
        + [_pick_spec(norm_g4.shape, (layer, sub)), _const_spec(w1.shape), _const_spec(w3.shape),
           _const_spec(w2.shape), _const_spec((1, d))]
        + cast_in_specs,
        out_specs=[_tok_spec(tm, d)] + cast_out_specs,
        scratch_shapes=[pltpu.VMEM((tm, d_ff), BF16)],
        compiler_params=_cparams(2),
        name="swiglu_ffn",
    )(x, mod5, mod5, mod5, norm_g4, w1, w3, w2, final_g.reshape(1, d), *[arr for arr, _ in cast])
    return out[0], list(out[1:])


def _hgrn_lower_bounds(raw, slot):
    ex = jnp.exp(raw - jnp.max(raw, axis=0, keepdims=True))
    sm = ex / jnp.sum(ex, axis=0, keepdims=True)
    lb = sm[0]
    for k in range(1, slot + 1):
        lb = lb + sm[k]
    return lb


def _abin_kernel(x_ref, shift_ref, scale_ref, g_ref, w_ref, lb_ref, q_ref, v_ref, lff_ref, kf_ref, lfb_ref,
                 kb_ref, sg_ref, p_ref, *, slot, sub_rows, n_chunk):
    dp, dh = p_ref.shape[-1], q_ref.shape[-1]
    lb_all = _hgrn_lower_bounds(lb_ref[...], slot)
    for r in range(x_ref.shape[0] // sub_rows):
        rows = slice(r * sub_rows, (r + 1) * sub_rows)
        h = _adaln(x_ref[rows, :], g_ref[...], shift_ref[...], scale_ref[...]).astype(BF16)

        p_ref[rows, :] = _dot(h, w_ref[:, 0:dp]).astype(BF16)
        for k in range(5):
            for c0 in range(0, dh, n_chunk):
                cols = slice(c0, c0 + n_chunk)
                z = _dot(h, w_ref[:, dp + k * dh + c0:dp + k * dh + c0 + n_chunk])
                if k == 0:
                    q_ref[rows, cols] = (_silu(z) * HGRN_HEAD_DIM ** -0.5).astype(BF16)
                elif k == 1:
                    v_ref[rows, cols] = z.astype(BF16)
                elif k == 4:
                    sg_ref[rows, cols] = _silu(z).astype(BF16)
                else:
                    lf_ref, kk_ref = ((lff_ref, kf_ref), (lfb_ref, kb_ref))[k - 2]
                    lb = lb_all[k - 2:k - 1, cols]
                    forget = lb + (1.0 - lb) * _sigmoid(z)
                    lf_ref[rows, cols] = jnp.log(forget) * LOG2_E
                    kk_ref[rows, cols] = (1.0 - forget).astype(BF16)


def _ab_in(x, mod5, layer, row_of_batch, norm_g4, w_in, hgrn_lb, slot, d_pool, *, tm=1024, sub_rows=256,
           n_chunk=768):
    bsz, length, d = x.shape
    tm = min(tm, length)
    dh = (w_in.shape[-1] - d_pool) // 5
    shift_spec, scale_spec, _ = _mod_block_specs(d, layer, 1, row_of_batch)
    kern = functools.partial(_abin_kernel, slot=slot, sub_rows=min(sub_rows, tm), n_chunk=n_chunk)
    widths = (dh, dh, dh, dh, dh, dh, dh, d_pool)
    dtypes = (BF16, BF16, F32, BF16, F32, BF16, BF16, BF16)
    return pl.pallas_call(
        kern,
        out_shape=[jax.ShapeDtypeStruct((bsz, length, n), dt) for n, dt in zip(widths, dtypes)],
        grid=(bsz, length // tm),
        in_specs=[_tok_spec(tm, d), shift_spec, scale_spec, _pick_spec(norm_g4.shape, (layer, 1)),
                  _const_spec(w_in.shape), _const_spec(hgrn_lb.shape)],
        out_specs=[_tok_spec(tm, n) for n in widths],
        compiler_params=_cparams(2),
        name="ab_in_proj",
    )(x, mod5, mod5, norm_g4, w_in, hgrn_lb)


def _level_table(c, diag_block):
    j = np.arange(c)[:, None]
    i = np.arange(c)[None, :]
    x = j ^ i
    bits = np.zeros((c, c), np.int32)
    for k in range(int(np.log2(c))):
        bits = np.where(x >= (1 << k), k + 1, bits)
    lvl = np.where(i < j, bits, 0)
    if diag_block > 1:
        lvl = np.where((i <= j) & (bits <= int(np.log2(diag_block))), int(np.log2(diag_block)), lvl)
    return lvl.astype(np.int32)


def _fine_decay(forget, rev):
    c, w = forget.shape
    f3 = forget.reshape(c // 8, 8, w)
    pos = lax.broadcasted_iota(jnp.int32, (c // 8, 8, w), 1)

    def at(offset):
        return f3 if offset == 0 else pltpu.roll(f3, (-offset) % 8, 1)

    sgn = 1 if rev else -1
    cq = [at(0)]
    ck = [None]
    for t in range(1, 4):
        cq.append(cq[-1] * at(sgn * t))
        ck.append(at(-sgn * t) if ck[-1] is None else ck[-1] * at(-sgn * t))
    out = {}
    for s in (2, 4, 8):
        half = s // 2
        p = pos % s
        e = jnp.ones_like(f3)
        for t in range(half):
            q_pos = (half - 1 - t) if rev else (half + t)
            k_pos = (half + t) if rev else (half - 1 - t)
            e = jnp.where(p == q_pos, cq[t], e)
            if t > 0:
                e = jnp.where(p == k_pos, ck[t], e)
        out[s] = e.reshape(c, w)
    return out


def _block_row(b, s, ridx):
    c, w = b.shape
    r = b.reshape(c // s, s, w)[:, ridx:ridx + 1, :]
    return jnp.broadcast_to(r, (c // s, s, w)).reshape(c, w)


def _key_decay_column(b_end, shape):
    return jnp.broadcast_to(jnp.exp2(b_end), shape).T


def _hgrn_head(qq, kk, v, forget, b, st, lvl, rev):
    c = qq.shape[0]
    b_end = b[0:1] if rev else b[c - 1:c]
    vb = v.astype(BF16)
    o = _dot((qq * jnp.exp2(b)).astype(BF16), st.astype(BF16))
    k_out = (kk * jnp.exp2(b_end - b)).astype(BF16)
    st_new = st * _key_decay_column(b_end, st.shape) + _dot_tn(k_out, vb)

    row = lax.broadcasted_iota(jnp.int32, (c, 1), 0)
    fine = _fine_decay(forget, rev)
    a = jnp.zeros((c, c), F32)
    s = c
    while s > 1:
        half = s // 2
        q_side = ((row % s) >= half) != rev
        if s >= 16:
            e = jnp.exp2(-jnp.abs(b - _block_row(b, s, half if rev else half - 1)))
        else:
            e = fine[s]
        x = jnp.where(q_side, qq, kk) * e
        a = jnp.where(lvl == int(np.log2(s)), _dot(x.astype(BF16), x.T.astype(BF16)), a)
        s //= 2
    o = o + _dot(a.astype(BF16), vb) + jnp.sum(qq * kk, axis=-1, keepdims=True) * v.astype(F32)
    return o, st_new


def _hgrn_mild_direction(rev, qq_ref, kk_ref, b_ref, v_ref, o_ref, st_ref, lvl, ops_ref, a_ref):
    c = qq_ref.shape[0]
    n_heads = st_ref.shape[0]
    sizes = []
    s = c
    while s > HGRN_DIAG_BLOCK:
        sizes.append(s)
        s //= 2

    for h in range(n_heads):
        sl = slice(h * HGRN_HEAD_DIM, (h + 1) * HGRN_HEAD_DIM)
        qq, kk, b = qq_ref[:, sl], kk_ref[:, sl], b_ref[:, sl]
        b_end = b[0:1] if rev else b[c - 1:c]
        st = st_ref[h]
        ops_ref[h, 0] = qq * jnp.exp2(b).astype(BF16)
        ops_ref[h, 1] = kk * jnp.exp2(b_end - b).astype(BF16)
        ops_ref[h, 2] = st.astype(BF16)
        st_ref[h] = st * _key_decay_column(b_end, st.shape)
        for n, s in enumerate(sizes):
            half = s // 2
            r = _block_row(b, s, half if rev else half - 1)
            q_side = [((r0 % s) >= half) != rev for r0 in range(0, c, half)]
            blocks = [slice(r0, r0 + half) for r0 in range(0, c, half)]
            diff = jnp.concatenate([b[rs] - r[rs] if qs else r[rs] - b[rs] for rs, qs in zip(blocks, q_side)],
                                   axis=0)
            picks = jnp.concatenate([(qq if qs else kk)[rs] for rs, qs in zip(blocks, q_side)], axis=0)
            x = picks * jnp.exp2(diff).astype(BF16)
            ops_ref[h, 3 + 2 * n] = x
            ops_ref[h, 4 + 2 * n] = x.T
        d = b - _block_row(b, HGRN_DIAG_BLOCK, HGRN_DIAG_BLOCK - 1 if rev else 0)
        ops_ref[h, 3 + 2 * len(sizes)] = qq * jnp.exp2(d).astype(BF16)
        ops_ref[h, 4 + 2 * len(sizes)] = (kk * jnp.exp2(-d).astype(BF16)).T

    for h in range(n_heads):
        a = jnp.zeros((c, c), F32)
        for n, s in enumerate(sizes + [HGRN_DIAG_BLOCK]):
            a = jnp.where(lvl == int(np.log2(s)), _dot(ops_ref[h, 3 + 2 * n], ops_ref[h, 4 + 2 * n]), a)
        a_ref[h] = a.astype(BF16)

    for h in range(n_heads):
        sl = slice(h * HGRN_HEAD_DIM, (h + 1) * HGRN_HEAD_DIM)
        vb = v_ref[:, sl].astype(BF16)
        o = _dot(ops_ref[h, 0], ops_ref[h, 2]) + _dot(a_ref[h], vb)
        o_ref[:, sl] = o.astype(o_ref.dtype)
        st_ref[h] = st_ref[h] + _dot_tn(ops_ref[h, 1], vb)


def _hgrn_kernel(qf_ref, vf_ref, lf_ref, kf_ref, qb_ref, vb_ref, lb_ref, kb_ref, s0_ref, tri_ref, lvl_ref,
                 of_ref, ob_ref, sout_ref, s_scr, b_s, span_s, ops_s, a_s):
    t = pl.program_id(1)

    @pl.when(t == 0)
    def _():
        s_scr[...] = s0_ref[...]

    c = HGRN_CHUNK
    n_sub = qf_ref.shape[0] // c
    n_heads = qf_ref.shape[-1] // HGRN_HEAD_DIM

    for i in range(n_sub):
        weakest = None
        for d, (logf_ref, off) in enumerate(((lf_ref, i * c), (lb_ref, (n_sub - 1 - i) * c))):
            logf = logf_ref[off:off + c, :]
            hi = logf.astype(BF16)
            lo = (logf - hi.astype(F32)).astype(BF16)
            b = _dot(tri_ref[d], jnp.concatenate([hi, lo], axis=0))
            b_s[i, d] = b
            for m in range(c // HGRN_DIAG_BLOCK):
                lo_row, hi_row = m * HGRN_DIAG_BLOCK, (m + 1) * HGRN_DIAG_BLOCK - 1
                if d == 0:
                    span = b[hi_row:hi_row + 1] - b[lo_row:lo_row + 1]
                else:
                    span = b[lo_row:lo_row + 1] - b[hi_row:hi_row + 1]
                weakest = span if weakest is None else jnp.minimum(weakest, span)
        span_s[i] = jnp.min(weakest)

    def chunk(i, carry):
        offs = (pl.multiple_of(i * c, c), pl.multiple_of((n_sub - 1 - i) * c, c))
        dirs = tuple(tuple(ref.at[pl.ds(off, c), :] for ref in refs)
                     for off, refs in zip(offs, ((qf_ref, vf_ref, lf_ref, kf_ref, of_ref),
                                                 (qb_ref, vb_ref, lb_ref, kb_ref, ob_ref))))
        in_range = span_s[i] > -HGRN_SAFE_LOG2_SPAN

        @pl.when(in_range)
        def _():
            for d, (q_ref, v_ref, logf_ref, kk_ref, o_ref) in enumerate(dirs):
                _hgrn_mild_direction(d == 1, q_ref, kk_ref, b_s.at[i, d], v_ref, o_ref, s_scr.at[d],
                                     lvl_ref[d], ops_s.at[d], a_s.at[d])

        @pl.when(jnp.logical_not(in_range))
        def _():
            for d, (q_ref, v_ref, logf_ref, kk_ref, o_ref) in enumerate(dirs):
                lvl = lvl_ref[2 + d]
                for h in range(n_heads):
                    sl = slice(h * HGRN_HEAD_DIM, (h + 1) * HGRN_HEAD_DIM)
                    o_h, st_new = _hgrn_head(q_ref[:, sl].astype(F32), kk_ref[:, sl].astype(F32), v_ref[:, sl],
                                             jnp.exp2(logf_ref[:, sl]), b_s[i, d, :, sl], s_scr[d, h], lvl,
                                             d == 1)
                    o_ref[:, sl] = o_h.astype(o_ref.dtype)
                    s_scr[d, h] = st_new

        return carry

    lax.fori_loop(0, n_sub, chunk, 0)

    @pl.when(t == pl.num_programs(1) - 1)
    def _():
        sout_ref[...] = s_scr[...]


def _hgrn_scan(q, v, logf_fwd, kk_fwd, logf_bwd, kk_bwd, s0, *, rows=512):
    bsz, length, dh = q.shape
    c = HGRN_CHUNK
    rows = min(rows, length)
    n = length // rows
    n_heads = dh // HGRN_HEAD_DIM
    assert c == HGRN_HEAD_DIM and c > HGRN_DIAG_BLOCK
    n_mild = int(np.log2(c // HGRN_DIAG_BLOCK)) + 1
    lvl_fast = _level_table(c, HGRN_DIAG_BLOCK)
    lvl_full = _level_table(c, 1)
    lvl = jnp.asarray(np.stack([lvl_fast, lvl_fast.T, lvl_full, lvl_full.T]))
    tri_f = np.tril(np.ones((c, c), np.float32))
    tri = jnp.asarray(np.stack([np.tile(tri_f, (1, 2)), np.tile(tri_f.T, (1, 2))]), BF16)
    fwd = pl.BlockSpec((None, rows, dh), lambda b, t: (b, t, 0))
    bwd = pl.BlockSpec((None, rows, dh), lambda b, t: (b, n - 1 - t, 0))
    s_shape = (2, n_heads, HGRN_HEAD_DIM, HGRN_HEAD_DIM)
    s_spec = pl.BlockSpec((None,) + s_shape, lambda b, t: (b, 0, 0, 0, 0))
    return pl.pallas_call(
        _hgrn_kernel,
        out_shape=[jax.ShapeDtypeStruct(q.shape, BF16), jax.ShapeDtypeStruct(q.shape, BF16),
                   jax.ShapeDtypeStruct((bsz,) + s_shape, F32)],
        grid=(bsz, n),
        in_specs=[fwd, fwd, fwd, fwd, bwd, bwd, bwd, bwd, s_spec, _const_spec((2, c, 2 * c)),
                  _const_spec((4, c, c))],
        out_specs=[fwd, bwd, s_spec],
        scratch_shapes=[pltpu.VMEM(s_shape, F32), pltpu.VMEM((rows // c, 2, c, dh), F32),
                        pltpu.SMEM((rows // c,), F32),
                        pltpu.VMEM((2, n_heads, 3 + 2 * n_mild, c, HGRN_HEAD_DIM), BF16),
                        pltpu.VMEM((2, n_heads, c, c), BF16)],
        compiler_params=_cparams(2),
        name="hgrn2_scan",
    )(q, v, logf_fwd, kk_fwd, q, v, logf_bwd, kk_bwd, s0, tri, lvl)


def _pool_tables(tm, row_len):
    t = np.arange(tm)
    same_row = (t[:, None] // row_len) == (t[None, :] // row_len)
    pos = t % row_len
    bands, inv = [], []
    for w in POOL_WINDOWS:
        lo = np.clip(pos - w // 2, 0, row_len)
        hi = np.clip(pos - w // 2 + w, 0, row_len)
        u = pos[None, :]
        bands.append(same_row & (u >= lo[:, None]) & (u < hi[:, None]))
        inv.append(1.0 / (hi - lo).astype(np.float64))
    return np.stack(bands).astype(np.float32), np.stack(inv, axis=1).astype(np.float32)


def _about_kernel(x_ref, of_ref, ob_ref, sg_ref, a_ref, gate_ref, gn_ref, band_ref, icnt_ref, wp_ref,
                  ps_ref, wo_ref, o_ref):
    sub_rows = band_ref.shape[-1]
    dp = a_ref.shape[-1]
    n_groups = band_ref.shape[0]
    for r in range(x_ref.shape[0] // sub_rows):
        rows = slice(r * sub_rows, (r + 1) * sub_rows)
        o = of_ref[rows, :].astype(F32) + ob_ref[rows, :].astype(F32)
        heads = []
        for h in range(o.shape[-1] // HGRN_HEAD_DIM):
            oh = o[:, h * HGRN_HEAD_DIM:(h + 1) * HGRN_HEAD_DIM]
            heads.append(oh * lax.rsqrt(jnp.mean(oh * oh, axis=-1, keepdims=True) + EPS))
        b_mix = jnp.concatenate(heads, axis=-1) * gn_ref[...] * sg_ref[rows, :].astype(F32)

        xab = a_ref[rows, :].astype(BF16)
        xa = xab.astype(F32)
        lane_group = lax.broadcasted_iota(jnp.int32, xa.shape, 1) // (dp // n_groups)
        total = jnp.zeros_like(xa)
        for gi in range(n_groups):
            total = jnp.where(lane_group == gi, _dot(band_ref[gi], xab), total)
        y = total * icnt_ref[...] - xa
        a_mix = _dot(y.astype(BF16), wp_ref[...]) * ps_ref[...]

        mix = _dot(a_mix.astype(BF16), wo_ref[0:dp, :]) + _dot(b_mix.astype(BF16), wo_ref[dp:, :])
        o_ref[rows, :] = x_ref[rows, :] + gate_ref[...] * mix


def _ab_out(x, o_f, o_b, g, xa, mod5, layer, row_of_batch, g_norm, e, w_pool_bd, pool_scale, w_out, row_len,
            *, tm=1024, sub_rows=256):
    bsz, length, d = x.shape
    dh = o_f.shape[-1]
    dp = xa.shape[-1]
    tm = min(tm, length)
    sub_rows = min(sub_rows, tm)
    n_groups = len(POOL_WINDOWS)
    bands, inv = _pool_tables(sub_rows, row_len)
    icnt = jnp.asarray(np.repeat(inv, dp // n_groups, axis=1))
    _, _, gate_spec = _mod_block_specs(d, layer, 1, row_of_batch)
    return pl.pallas_call(
        _about_kernel,
        out_shape=jax.ShapeDtypeStruct(x.shape, F32),
        grid=(bsz, length // tm),
        in_specs=[_tok_spec(tm, d), _tok_spec(tm, dh), _tok_spec(tm, dh), _tok_spec(tm, dh), _tok_spec(tm, dp),
                  gate_spec, _pick_spec(g_norm.shape, (e,)), _const_spec((n_groups, sub_rows, sub_rows)),
                  _const_spec((sub_rows, dp)), _const_spec((dp, dp)), _pick_spec(pool_scale.shape, (e,)),
                  _const_spec(w_out.shape)],
        out_specs=_tok_spec(tm, d),
        compiler_params=_cparams(2),
        name="ab_out_proj",
    )(x, o_f, o_b, g, xa, mod5, g_norm, jnp.asarray(bands, BF16), icnt, w_pool_bd, pool_scale, w_out)


def _gelu_tanh(x):
    k1 = float(-2.0 * np.sqrt(2.0 / np.pi) * np.log2(np.e))
    k2 = 0.044715 * k1
    return x / (1.0 + jnp.exp2(x * (k1 + k2 * (x * x))))


def _gmlp_kernel(x_ref, shift_ref, scale_ref, gate_ref, g_ref, win_ref, lng_ref, lnb_ref, ws_ref, bs_ref,
                 wout_ref, o_ref, u_ref, v_ref, s_ref, *, sub_rows, n_chunk):
    dg = win_ref.shape[1] // 2
    for r in range(x_ref.shape[0] // sub_rows):
        rows = slice(r * sub_rows, (r + 1) * sub_rows)
        x = x_ref[rows, :]
        h = _adaln(x, g_ref[...], shift_ref[...], scale_ref[...]).astype(BF16)
        total = jnp.zeros((sub_rows, 1), F32)
        for c0 in range(0, dg, n_chunk):
            cols = slice(c0, c0 + n_chunk)
            u_ref[rows, cols] = _gelu_tanh(_dot(h, win_ref[:, c0:c0 + n_chunk]))
            zv = _gelu_tanh(_dot(h, win_ref[:, dg + c0:dg + c0 + n_chunk]))
            v_ref[rows, cols] = zv
            total = total + jnp.sum(zv, axis=-1, keepdims=True)
        mean = total / dg
        sq = jnp.zeros((sub_rows, 1), F32)
        for c0 in range(0, dg, n_chunk):
            vc = v_ref[rows, c0:c0 + n_chunk] - mean
            sq = sq + jnp.sum(vc * vc, axis=-1, keepdims=True)
        rstd = lax.rsqrt(sq / dg + EPS)
        for c in range(sub_rows // GMLP_CHUNK):
            crows = slice(c * GMLP_CHUNK, (c + 1) * GMLP_CHUNK)
            srows = slice(r * sub_rows + c * GMLP_CHUNK, r * sub_rows + (c + 1) * GMLP_CHUNK)
            for gi in range(dg // GMLP_GROUP_DIM):
                cols = slice(gi * GMLP_GROUP_DIM, (gi + 1) * GMLP_GROUP_DIM)
                vn = (v_ref[srows, cols] - mean[crows]) * rstd[crows] * lng_ref[:, cols] + lnb_ref[:, cols]
                sv = _dot(ws_ref[gi], vn.astype(BF16)) + bs_ref[gi]
                s_ref[srows, cols] = (u_ref[srows, cols] * sv).astype(BF16)
        o_ref[rows, :] = x + gate_ref[...] * _dot(s_ref[rows, :], wout_ref[...])


def _gmlp(x, mod5, layer, row_of_batch, norm_g4, o, w_in, ln_g, ln_b, w_s, b_s_wide, w_out, *, tm=1024,
          sub_rows=512, n_chunk=512):
    bsz, length, d = x.shape
    dg = w_in.shape[-1] // 2
    tm = min(tm, length)
    kern = functools.partial(_gmlp_kernel, sub_rows=min(sub_rows, tm), n_chunk=n_chunk)
    return pl.pallas_call(
        kern,
        out_shape=jax.ShapeDtypeStruct(x.shape, F32),
        grid=(bsz, length // tm),
        in_specs=[_tok_spec(tm, d)]
        + _mod_block_specs(d, layer, 1, row_of_batch)
        + [_pick_spec(norm_g4.shape, (layer, 1)), _const_spec(w_in.shape), _pick_spec(ln_g.shape, (o,)),
           _pick_spec(ln_b.shape, (o,)), _const_spec(w_s.shape), _pick_spec(b_s_wide.shape, (o,)),
           _const_spec(w_out.shape)],
        out_specs=_tok_spec(tm, d),
        scratch_shapes=[pltpu.VMEM((tm, dg), F32), pltpu.VMEM((tm, dg), F32), pltpu.VMEM((tm, dg), BF16)],
        compiler_params=_cparams(2),
        name="gmlp_mixer",
    )(x, mod5, mod5, mod5, norm_g4, w_in, ln_g, ln_b, w_s, b_s_wide, w_out)


def _block_diag(w):
    n, g, a, b = w.shape
    eye = jnp.eye(g, dtype=w.dtype)
    return (w[:, :, :, None, :] * eye[None, :, None, :, None]).reshape(n, g * a, g * b)


def kernel(x, c, ctx, c_ctx, mod_w, mod_b, norm_g, ffn_w1, ffn_w3, ffn_w2, ab_w_in, pool_w, pool_scale,
           hgrn_lb, hgrn_norm_g, ab_w_out, gmlp_w_in, gmlp_ln_g, gmlp_ln_b, gmlp_w_s, gmlp_b_s, gmlp_w_out,
           final_g):
    bsz, _, d = x.shape
    depth = mod_w.shape[0]
    d_pool = pool_scale.shape[-1]
    d_hgrn = hgrn_norm_g.shape[-1]
    n_heads = d_hgrn // HGRN_HEAD_DIM

    c_rows = jnp.concatenate([c, c_ctx[None, :]], axis=0)
    mod = _modulation(c_rows, mod_w, mod_b)
    mod5 = mod.reshape(depth, bsz + 1, N_MOD, 1, d)
    lat_row = lambda b: b
    ctx_row = lambda b: bsz

    norm_g4 = norm_g[:, :, None, :]
    pool_bd = _block_diag(pool_w).astype(BF16)
    pool_scale3, hgrn_norm_g3 = pool_scale[:, None, :], hgrn_norm_g[:, None, :]
    gm_ln_g, gm_ln_b = gmlp_ln_g[:, None, :], gmlp_ln_b[:, None, :]
    gm_b_wide = jnp.broadcast_to(gmlp_b_s[:, :, :, None], gmlp_b_s.shape + (GMLP_GROUP_DIM,))
    gmlp_w_s2 = gmlp_w_s.reshape(gmlp_w_s.shape[0], -1, gmlp_w_s.shape[-1])

    def ffn_sources(i, j):
        return [(ffn_w1, (i, j)), (ffn_w3, (i, j)), (ffn_w2, (i, j))]

    def mixer_sources(i):
        if i % 2 == 0:
            return [(ab_w_in, (i // 2,)), (ab_w_out, (i // 2,))]
        return [(gmlp_w_in, (i // 2,)), (gmlp_w_s2, (i // 2,)), (gmlp_w_out, (i // 2,))]

    ffn_order = [(i, j) for i in range(depth) for j in range(2)]
    ffn_w = {ffn_order[0]: [arr[lead].astype(BF16) for arr, lead in ffn_sources(*ffn_order[0])]}
    mixer_w = {}

    def latent_ffn(xl, i, j, **kw):
        k = ffn_order.index((i, j))
        cast = ffn_sources(*ffn_order[k + 1]) if k + 1 < len(ffn_order) else []
        n_next = len(cast)
        if j == 0:
            cast = cast + mixer_sources(i)
        xl, done = _ffn(xl, mod5, i, 2 * j, lat_row, norm_g4, *ffn_w[(i, j)], final_g, cast=cast, **kw)
        if n_next:
            ffn_w[ffn_order[k + 1]] = done[:n_next]
        if j == 0:
            mixer_w[i] = done[n_next:]
        return xl

    def context_ffn(xc, i, j):
        return _ffn(xc, mod5, i, 2 * j, ctx_row, norm_g4, *ffn_w[(i, j)], final_g)[0]

    xl, xc = x, ctx
    for i in range(depth):
        ctx_live = any(j % 2 == 0 for j in range(i, depth))
        last = i == depth - 1

        xl = latent_ffn(xl, i, 0)
        if ctx_live:
            xc = context_ffn(xc, i, 0)

        if i % 2 == 0:
            e = i // 2
            ab_in_w, ab_out_w = mixer_w[i]
            ab_in = functools.partial(_ab_in, mod5=mod5, layer=i, norm_g4=norm_g4, w_in=ab_in_w,
                                      hgrn_lb=hgrn_lb, slot=e, d_pool=d_pool)
            q_c, v_c, lf_c, kf_c, lb_c, kb_c, sg_c, a_c = ab_in(xc, row_of_batch=ctx_row)
            q_l, v_l, lf_l, kf_l, lb_l, kb_l, sg_l, a_l = ab_in(xl, row_of_batch=lat_row)
            s0 = jnp.zeros((bsz, 2, n_heads, HGRN_HEAD_DIM, HGRN_HEAD_DIM), F32)
            of_c, ob_c, s_ctx = _hgrn_scan(q_c, v_c, lf_c, kf_c, lb_c, kb_c, s0)
            of_l, ob_l, _ = _hgrn_scan(q_l, v_l, lf_l, kf_l, lb_l, kb_l, s_ctx)
            ab_out = functools.partial(_ab_out, mod5=mod5, layer=i, g_norm=hgrn_norm_g3, e=e,
                                       w_pool_bd=pool_bd[e], pool_scale=pool_scale3, w_out=ab_out_w)
            xc = ab_out(xc, of_c, ob_c, sg_c, a_c, row_of_batch=ctx_row, row_len=xc.shape[1])
            xl = ab_out(xl, of_l, ob_l, sg_l, a_l, row_of_batch=lat_row, row_len=GRID_W)
        else:
            o = i // 2
            gm_in, gm_s2, gm_out = mixer_w[i]
            gmlp = functools.partial(_gmlp, mod5=mod5, layer=i, norm_g4=norm_g4, o=o, w_in=gm_in, ln_g=gm_ln_g,
                                     ln_b=gm_ln_b, w_s=gm_s2.reshape(gmlp_w_s.shape[1:]), b_s_wide=gm_b_wide,
                                     w_out=gm_out)
            xl = gmlp(xl, row_of_batch=lat_row)
            if ctx_live:
                xc = gmlp(xc, row_of_batch=ctx_row)

        xl = latent_ffn(xl, i, 1, final_norm=last)
        if ctx_live:
            xc = context_ffn(xc, i, 1)
    return xl
```

```python
import functools

import numpy as np
import jax
import jax.numpy as jnp
from jax import lax
from jax.experimental import pallas as pl
from jax.experimental.pallas import tpu as pltpu

F32 = jnp.float32
BF16 = jnp.bfloat16

EPS = 1e-6
N_MOD = 9
GRID_W = 64
POOL_WINDOWS = (2, 4, 8, 16)
HGRN_HEAD_DIM = 128
GMLP_CHUNK = 128
GMLP_GROUP_DIM = 128

LOG2_E = float(np.log2(np.e))
HGRN_CHUNK = 128
HGRN_DIAG_BLOCK = 32
HGRN_SAFE_LOG2_SPAN = 80.0

VMEM_LIMIT_BYTES = 56 * 1024 * 1024


def _cparams(n_grid_dims):
    return pltpu.CompilerParams(
        dimension_semantics=("arbitrary",) * n_grid_dims,
        vmem_limit_bytes=VMEM_LIMIT_BYTES)


def _sigmoid(x):
    return 1.0 / (1.0 + jnp.exp2(x * (-LOG2_E)))


def _silu(x):
    return x * _sigmoid(x)


def _rms(x, g):
    return x * lax.rsqrt(jnp.mean(x * x, axis=-1, keepdims=True) + EPS) * g


def _adaln(x, g, shift, scale):
    return _rms(x, g) * (1.0 + scale) + shift


def _dot(a, b):
    return jnp.dot(a, b, preferred_element_type=F32)


def _dot_tn(a, b):
    return lax.dot_general(a, b, (((0,), (0,)), ((), ())), preferred_element_type=F32)


def _const_spec(shape):
    nd = len(shape)
    return pl.BlockSpec(shape, lambda *_: (0,) * nd, pipeline_mode=pl.Buffered(1))


def _pick_spec(full_shape, lead):
    tail = tuple(full_shape[len(lead):])
    idx = tuple(lead) + (0,) * len(tail)
    return pl.BlockSpec((None,) * len(lead) + tail, lambda *_: idx, pipeline_mode=pl.Buffered(1))


def _tok_spec(tm, n):
    return pl.BlockSpec((None, tm, n), lambda b, t: (b, t, 0))


def _mod_kernel(c_ref, w_ref, b_ref, o_ref, a_scr, *, col_group):
    @pl.when((pl.program_id(0) == 0) & (pl.program_id(1) == 0))
    def _():
        a_scr[...] = _silu(c_ref[...])

    rows, d, lanes = a_scr.shape
    tn = w_ref.shape[1]
    n_blk = col_group // lanes
    sub = 8

    for g0 in range(0, tn, col_group):
        def body(kc, accs, g0=g0):
            k0 = pl.multiple_of(kc * sub, sub)
            w8 = w_ref[pl.ds(k0, sub), g0:g0 + col_group]
            out = []
            for r in range(rows):
                a8 = a_scr[r, pl.ds(k0, sub), :]
                out.append(tuple(accs[r][j] + a8 * w8[:, j * lanes:(j + 1) * lanes] for j in range(n_blk)))
            return tuple(out)

        zero = jnp.zeros((sub, lanes), F32)
        accs = lax.fori_loop(0, d // sub, body, tuple((zero,) * n_blk for _ in range(rows)), unroll=4)
        for r in range(rows):
            row = jnp.concatenate([jnp.sum(a, axis=0, keepdims=True) for a in accs[r]], axis=1)
            o_ref[r:r + 1, g0:g0 + col_group] = row + b_ref[:, g0:g0 + col_group]


def _modulation(c_rows, mod_w, mod_b, tn=4608, col_group=1152):
    depth, d, n = mod_w.shape
    rows = c_rows.shape[0]
    lanes = 128
    c_wide = jnp.broadcast_to(c_rows[:, :, None], (rows, d, lanes))
    return pl.pallas_call(
        functools.partial(_mod_kernel, col_group=col_group),
        out_shape=jax.ShapeDtypeStruct((depth, rows, n), F32),
        grid=(depth, n // tn),
        in_specs=[
            _const_spec((rows, d, lanes)),
            pl.BlockSpec((None, d, tn), lambda l, j: (l, 0, j)),
            pl.BlockSpec((None, 1, tn), lambda l, j: (l, 0, j)),
        ],
        out_specs=pl.BlockSpec((None, rows, tn), lambda l, j: (l, 0, j)),
        scratch_shapes=[pltpu.VMEM((rows, d, lanes), F32)],
        compiler_params=_cparams(2),
        name="modulation",
    )(c_wide, mod_w, mod_b.reshape(depth, 1, n))


def _mod_index(layer, col, row_of_batch, b, t):
    return (layer, row_of_batch(b), col, 0, 0)


def _mod_block_specs(d_model, layer, sub, row_of_batch):
    return [pl.BlockSpec((None, None, None, 1, d_model),
                         functools.partial(_mod_index, layer, 3 * sub + k, row_of_batch))
            for k in range(3)]


def _ffn_kernel(x_ref, shift_ref, scale_ref, gate_ref, g_ref, w1_ref, w3_ref, w2_ref, fg_ref, *rest,
                n_cast, n_chunk, sub_rows, final_norm):
    cast_in, o_ref, cast_out, a_ref = rest[:n_cast], rest[n_cast], rest[n_cast + 1:-1], rest[-1]
    for src, dst in zip(cast_in, cast_out):
        dst[...] = src[...].astype(BF16)
    d_ff = w1_ref.shape[1]
    for r in range(x_ref.shape[0] // sub_rows):
        rows = slice(r * sub_rows, (r + 1) * sub_rows)
        x = x_ref[rows, :]
        h = _adaln(x, g_ref[...], shift_ref[...], scale_ref[...]).astype(BF16)
        for c in range(d_ff // n_chunk):
            sl = slice(c * n_chunk, (c + 1) * n_chunk)
            u = _dot(h, w1_ref[:, sl])
            v = _dot(h, w3_ref[:, sl])
            a_ref[rows, sl] = (_silu(u) * v).astype(BF16)
        y = _dot(a_ref[rows, :], w2_ref[...])
        out = x + (0.5 * gate_ref[...]) * y
        if final_norm:
            out = _rms(out, fg_ref[...])
        o_ref[rows, :] = out


def _ffn(x, mod5, layer, sub, row_of_batch, norm_g4, w1, w3, w2, final_g, *, cast=(), final_norm=False,
         tm=1024, n_chunk=256, sub_rows=256):
    bsz, length, d = x.shape
    d_ff = w1.shape[-1]
    tm = min(tm, length)
    n_t = length // tm
    n_steps = bsz * n_t
    bf16_rows = 16
    cast_in_specs, cast_out_specs, cast_shapes = [], [], []
    for arr, lead in cast:
        rows, cols = arr.shape[len(lead):]
        band = rows // n_steps
        assert band * n_steps == rows and band % bf16_rows == 0, (arr.shape, n_steps)
        cast_in_specs.append(pl.BlockSpec(
            (None,) * len(lead) + (band, cols),
            functools.partial(lambda lead, b, t: tuple(lead) + (b * n_t + t, 0), lead)))
        cast_out_specs.append(pl.BlockSpec((band, cols), lambda b, t: (b * n_t + t, 0)))
        cast_shapes.append(jax.ShapeDtypeStruct((rows, cols), BF16))
    kern = functools.partial(_ffn_kernel, n_cast=len(cast), n_chunk=n_chunk, sub_rows=min(sub_rows, tm),
                             final_norm=final_norm)
    out = pl.pallas_call(
        kern,
        out_shape=[jax.ShapeDtypeStruct(x.shape, F32)] + cast_shapes,
        grid=(bsz, n_t),
        in_specs=[_tok_spec(tm, d)]
        + _mod_block_specs(d, layer, sub, row_of_batch)
        + [_pick_spec(norm_g4.shape, (layer, sub)), _const_spec(w1.shape), _const_spec(w3.shape),
           _const_spec(w2.shape), _const_spec((1, d))]
        + cast_in_specs,
        out_specs=[_tok_spec(tm, d)] + cast_out_specs,
        scratch_shapes=[pltpu.VMEM((tm, d_ff), BF16)],
        compiler_params=_cparams(2),
        name="swiglu_ffn",
    )(x, mod5, mod5, mod5, norm_g4, w1, w3, w2, final_g.reshape(1, d), *[arr for arr, _ in cast])
    return out[0], list(out[1:])


def _hgrn_lower_bounds(raw, slot):
    ex = jnp.exp(raw - jnp.max(raw, axis=0, keepdims=True))
    sm = ex / jnp.sum(ex, axis=0, keepdims=True)
    lb = sm[0]
    for k in range(1, slot + 1):
        lb = lb + sm[k]
    return lb


def _abin_kernel(x_ref, shift_ref, scale_ref, g_ref, w_ref, lb_ref, q_ref, v_ref, lff_ref, kf_ref, lfb_ref,
                 kb_ref, sg_ref, p_ref, *, slot, sub_rows, n_chunk):
    dp, dh = p_ref.shape[-1], q_ref.shape[-1]
    lb_all = _hgrn_lower_bounds(lb_ref[...], slot)
    for r in range(x_ref.shape[0] // sub_rows):
        rows = slice(r * sub_rows, (r + 1) * sub_rows)
        h = _adaln(x_ref[rows, :], g_ref[...], shift_ref[...], scale_ref[...]).astype(BF16)

        p_ref[rows, :] = _dot(h, w_ref[:, 0:dp]).astype(BF16)
        for k in range(5):
            for c0 in range(0, dh, n_chunk):
                cols = slice(c0, c0 + n_chunk)
                z = _dot(h, w_ref[:, dp + k * dh + c0:dp + k * dh + c0 + n_chunk])
                if k == 0:
                    q_ref[rows, cols] = (_silu(z) * HGRN_HEAD_DIM ** -0.5).astype(BF16)
                elif k == 1:
                    v_ref[rows, cols] = z.astype(BF16)
                elif k == 4:
                    sg_ref[rows, cols] = _silu(z).astype(BF16)
                else:
                    lf_ref, kk_ref = ((lff_ref, kf_ref), (lfb_ref, kb_ref))[k - 2]
                    lb = lb_all[k - 2:k - 1, cols]
                    forget = lb + (1.0 - lb) * _sigmoid(z)
                    lf_ref[rows, cols] = (jnp.log(forget) * LOG2_E).astype(BF16)
                    kk_ref[rows, cols] = (1.0 - forget).astype(BF16)


def _ab_in(x, mod5, layer, row_of_batch, norm_g4, w_in, hgrn_lb, slot, d_pool, *, tm=1024, sub_rows=256,
           n_chunk=768):
    bsz, length, d = x.shape
    tm = min(tm, length)
    dh = (w_in.shape[-1] - d_pool) // 5
    shift_spec, scale_spec, _ = _mod_block_specs(d, layer, 1, row_of_batch)
    kern = functools.partial(_abin_kernel, slot=slot, sub_rows=min(sub_rows, tm), n_chunk=n_chunk)
    widths = (dh, dh, dh, dh, dh, dh, dh, d_pool)
    dtypes = (BF16,) * len(widths)
    return pl.pallas_call(
        kern,
        out_shape=[jax.ShapeDtypeStruct((bsz, length, n), dt) for n, dt in zip(widths, dtypes)],
        grid=(bsz, length // tm),
        in_specs=[_tok_spec(tm, d), shift_spec, scale_spec, _pick_spec(norm_g4.shape, (layer, 1)),
                  _const_spec(w_in.shape), _const_spec(hgrn_lb.shape)],
        out_specs=[_tok_spec(tm, n) for n in widths],
        compiler_params=_cparams(2),
        name="ab_in_proj",
    )(x, mod5, mod5, norm_g4, w_in, hgrn_lb)


def _level_table(c, diag_block):
    j = np.arange(c)[:, None]
    i = np.arange(c)[None, :]
    x = j ^ i
    bits = np.zeros((c, c), np.int32)
    for k in range(int(np.log2(c))):
        bits = np.where(x >= (1 << k), k + 1, bits)
    lvl = np.where(i < j, bits, 0)
    if diag_block > 1:
        lvl = np.where((i <= j) & (bits <= int(np.log2(diag_block))), int(np.log2(diag_block)), lvl)
    return lvl.astype(np.int32)


def _fine_decay(forget, rev):
    c, w = forget.shape
    f3 = forget.reshape(c // 8, 8, w)
    pos = lax.broadcasted_iota(jnp.int32, (c // 8, 8, w), 1)

    def at(offset):
        return f3 if offset == 0 else pltpu.roll(f3, (-offset) % 8, 1)

    sgn = 1 if rev else -1
    cq = [at(0)]
    ck = [None]
    for t in range(1, 4):
        cq.append(cq[-1] * at(sgn * t))
        ck.append(at(-sgn * t) if ck[-1] is None else ck[-1] * at(-sgn * t))
    out = {}
    for s in (2, 4, 8):
        half = s // 2
        p = pos % s
        e = jnp.ones_like(f3)
        for t in range(half):
            q_pos = (half - 1 - t) if rev else (half + t)
            k_pos = (half + t) if rev else (half - 1 - t)
            e = jnp.where(p == q_pos, cq[t], e)
            if t > 0:
                e = jnp.where(p == k_pos, ck[t], e)
        out[s] = e.reshape(c, w)
    return out


def _block_row(b, s, ridx):
    c, w = b.shape
    r = b.reshape(c // s, s, w)[:, ridx:ridx + 1, :]
    return jnp.broadcast_to(r, (c // s, s, w)).reshape(c, w)


def _key_decay_column(b_end, shape):
    return jnp.broadcast_to(jnp.exp2(b_end), shape).T


def _hgrn_head(qq, kk, v, forget, b, st, lvl, rev):
    c = qq.shape[0]
    b_end = b[0:1] if rev else b[c - 1:c]
    vb = v.astype(BF16)
    o = _dot((qq * jnp.exp2(b)).astype(BF16), st.astype(BF16))
    k_out = (kk * jnp.exp2(b_end - b)).astype(BF16)
    st_new = st * _key_decay_column(b_end, st.shape) + _dot_tn(k_out, vb)

    row = lax.broadcasted_iota(jnp.int32, (c, 1), 0)
    fine = _fine_decay(forget, rev)
    a = jnp.zeros((c, c), F32)
    s = c
    while s > 1:
        half = s // 2
        q_side = ((row % s) >= half) != rev
        if s >= 16:
            e = jnp.exp2(-jnp.abs(b - _block_row(b, s, half if rev else half - 1)))
        else:
            e = fine[s]
        x = jnp.where(q_side, qq, kk) * e
        a = jnp.where(lvl == int(np.log2(s)), _dot(x.astype(BF16), x.T.astype(BF16)), a)
        s //= 2
    o = o + _dot(a.astype(BF16), vb) + jnp.sum(qq * kk, axis=-1, keepdims=True) * v.astype(F32)
    return o, st_new


def _hgrn_mild_direction(rev, qq_ref, kk_ref, b_ref, v_ref, o_ref, st_ref, lvl, ops_ref, a_ref):
    c = qq_ref.shape[0]
    n_heads = st_ref.shape[0]
    sizes = []
    s = c
    while s > HGRN_DIAG_BLOCK:
        sizes.append(s)
        s //= 2

    for h in range(n_heads):
        sl = slice(h * HGRN_HEAD_DIM, (h + 1) * HGRN_HEAD_DIM)
        qq, kk, b = qq_ref[:, sl], kk_ref[:, sl], b_ref[:, sl]
        b_end = b[0:1] if rev else b[c - 1:c]
        st = st_ref[h]
        ops_ref[h, 0] = qq * jnp.exp2(b).astype(BF16)
        ops_ref[h, 1] = kk * jnp.exp2(b_end - b).astype(BF16)
        ops_ref[h, 2] = st.astype(BF16)
        st_ref[h] = st * _key_decay_column(b_end, st.shape)
        for n, s in enumerate(sizes):
            half = s // 2
            r = _block_row(b, s, half if rev else half - 1)
            q_side = [((r0 % s) >= half) != rev for r0 in range(0, c, half)]
            blocks = [slice(r0, r0 + half) for r0 in range(0, c, half)]
            diff = jnp.concatenate([b[rs] - r[rs] if qs else r[rs] - b[rs] for rs, qs in zip(blocks, q_side)],
                                   axis=0)
            picks = jnp.concatenate([(qq if qs else kk)[rs] for rs, qs in zip(blocks, q_side)], axis=0)
            x = picks * jnp.exp2(diff).astype(BF16)
            ops_ref[h, 3 + 2 * n] = x
            ops_ref[h, 4 + 2 * n] = x.T
        d = b - _block_row(b, HGRN_DIAG_BLOCK, HGRN_DIAG_BLOCK - 1 if rev else 0)
        ops_ref[h, 3 + 2 * len(sizes)] = qq * jnp.exp2(d).astype(BF16)
        ops_ref[h, 4 + 2 * len(sizes)] = (kk * jnp.exp2(-d).astype(BF16)).T

    for h in range(n_heads):
        a = jnp.zeros((c, c), F32)
        for n, s in enumerate(sizes + [HGRN_DIAG_BLOCK]):
            a = jnp.where(lvl == int(np.log2(s)), _dot(ops_ref[h, 3 + 2 * n], ops_ref[h, 4 + 2 * n]), a)
        a_ref[h] = a.astype(BF16)

    for h in range(n_heads):
        sl = slice(h * HGRN_HEAD_DIM, (h + 1) * HGRN_HEAD_DIM)
        vb = v_ref[:, sl].astype(BF16)
        o = _dot(ops_ref[h, 0], ops_ref[h, 2]) + _dot(a_ref[h], vb)
        o_ref[:, sl] = o.astype(o_ref.dtype)
        st_ref[h] = st_ref[h] + _dot_tn(ops_ref[h, 1], vb)


def _hgrn_kernel(qf_ref, vf_ref, lf_ref, kf_ref, qb_ref, vb_ref, lb_ref, kb_ref, s0_ref, tri_ref, lvl_ref,
                 of_ref, ob_ref, sout_ref, s_scr, b_s, span_s, ops_s, a_s):
    t = pl.program_id(1)

    @pl.when(t == 0)
    def _():
        s_scr[...] = s0_ref[...]

    c = HGRN_CHUNK
    n_sub = qf_ref.shape[0] // c
    n_heads = qf_ref.shape[-1] // HGRN_HEAD_DIM

    for i in range(n_sub):
        weakest = None
        for d, (logf_ref, off) in enumerate(((lf_ref, i * c), (lb_ref, (n_sub - 1 - i) * c))):
            b = _dot(tri_ref[d], logf_ref[off:off + c, :])
            b_s[i, d] = b
            for m in range(c // HGRN_DIAG_BLOCK):
                lo_row, hi_row = m * HGRN_DIAG_BLOCK, (m + 1) * HGRN_DIAG_BLOCK - 1
                if d == 0:
                    span = b[hi_row:hi_row + 1] - b[lo_row:lo_row + 1]
                else:
                    span = b[lo_row:lo_row + 1] - b[hi_row:hi_row + 1]
                weakest = span if weakest is None else jnp.minimum(weakest, span)
        span_s[i] = jnp.min(weakest)

    def chunk(i, carry):
        offs = (pl.multiple_of(i * c, c), pl.multiple_of((n_sub - 1 - i) * c, c))
        dirs = tuple(tuple(ref.at[pl.ds(off, c), :] for ref in refs)
                     for off, refs in zip(offs, ((qf_ref, vf_ref, lf_ref, kf_ref, of_ref),
                                                 (qb_ref, vb_ref, lb_ref, kb_ref, ob_ref))))
        in_range = span_s[i] > -HGRN_SAFE_LOG2_SPAN

        @pl.when(in_range)
        def _():
            for d, (q_ref, v_ref, logf_ref, kk_ref, o_ref) in enumerate(dirs):
                _hgrn_mild_direction(d == 1, q_ref, kk_ref, b_s.at[i, d], v_ref, o_ref, s_scr.at[d],
                                     lvl_ref[d], ops_s.at[d], a_s.at[d])

        @pl.when(jnp.logical_not(in_range))
        def _():
            for d, (q_ref, v_ref, logf_ref, kk_ref, o_ref) in enumerate(dirs):
                lvl = lvl_ref[2 + d]
                for h in range(n_heads):
                    sl = slice(h * HGRN_HEAD_DIM, (h + 1) * HGRN_HEAD_DIM)
                    o_h, st_new = _hgrn_head(q_ref[:, sl].astype(F32), kk_ref[:, sl].astype(F32), v_ref[:, sl],
                                             jnp.exp2(logf_ref[:, sl].astype(F32)), b_s[i, d, :, sl], s_scr[d, h], lvl,
                                             d == 1)
                    o_ref[:, sl] = o_h.astype(o_ref.dtype)
                    s_scr[d, h] = st_new

        return carry

    lax.fori_loop(0, n_sub, chunk, 0)

    @pl.when(t == pl.num_programs(1) - 1)
    def _():
        sout_ref[...] = s_scr[...]


def _hgrn_scan(q, v, logf_fwd, kk_fwd, logf_bwd, kk_bwd, s0, *, rows=512):
    bsz, length, dh = q.shape
    c = HGRN_CHUNK
    rows = min(rows, length)
    n = length // rows
    n_heads = dh // HGRN_HEAD_DIM
    assert c == HGRN_HEAD_DIM and c > HGRN_DIAG_BLOCK
    n_mild = int(np.log2(c // HGRN_DIAG_BLOCK)) + 1
    lvl_fast = _level_table(c, HGRN_DIAG_BLOCK)
    lvl_full = _level_table(c, 1)
    lvl = jnp.asarray(np.stack([lvl_fast, lvl_fast.T, lvl_full, lvl_full.T]))
    tri_f = np.tril(np.ones((c, c), np.float32))
    tri = jnp.asarray(np.stack([tri_f, tri_f.T]), BF16)
    fwd = pl.BlockSpec((None, rows, dh), lambda b, t: (b, t, 0))
    bwd = pl.BlockSpec((None, rows, dh), lambda b, t: (b, n - 1 - t, 0))
    s_shape = (2, n_heads, HGRN_HEAD_DIM, HGRN_HEAD_DIM)
    s_spec = pl.BlockSpec((None,) + s_shape, lambda b, t: (b, 0, 0, 0, 0))
    return pl.pallas_call(
        _hgrn_kernel,
        out_shape=[jax.ShapeDtypeStruct(q.shape, BF16), jax.ShapeDtypeStruct(q.shape, BF16),
                   jax.ShapeDtypeStruct((bsz,) + s_shape, F32)],
        grid=(bsz, n),
        in_specs=[fwd, fwd, fwd, fwd, bwd, bwd, bwd, bwd, s_spec, _const_spec((2, c, c)),
                  _const_spec((4, c, c))],
        out_specs=[fwd, bwd, s_spec],
        scratch_shapes=[pltpu.VMEM(s_shape, F32), pltpu.VMEM((rows // c, 2, c, dh), F32),
                        pltpu.SMEM((rows // c,), F32),
                        pltpu.VMEM((2, n_heads, 3 + 2 * n_mild, c, HGRN_HEAD_DIM), BF16),
                        pltpu.VMEM((2, n_heads, c, c), BF16)],
        compiler_params=_cparams(2),
        name="hgrn2_scan",
    )(q, v, logf_fwd, kk_fwd, q, v, logf_bwd, kk_bwd, s0, tri, lvl)


def _pool_tables(tm, row_len):
    t = np.arange(tm)
    same_row = (t[:, None] // row_len) == (t[None, :] // row_len)
    pos = t % row_len
    bands, inv = [], []
    for w in POOL_WINDOWS:
        lo = np.clip(pos - w // 2, 0, row_len)
        hi = np.clip(pos - w // 2 + w, 0, row_len)
        u = pos[None, :]
        bands.append(same_row & (u >= lo[:, None]) & (u < hi[:, None]))
        inv.append(1.0 / (hi - lo).astype(np.float64))
    return np.stack(bands).astype(np.float32), np.stack(inv, axis=1).astype(np.float32)


def _about_kernel(x_ref, of_ref, ob_ref, sg_ref, a_ref, gate_ref, gn_ref, band_ref, icnt_ref, wp_ref,
                  ps_ref, wo_ref, o_ref):
    sub_rows = band_ref.shape[-1]
    dp = a_ref.shape[-1]
    n_groups = band_ref.shape[0]
    for r in range(x_ref.shape[0] // sub_rows):
        rows = slice(r * sub_rows, (r + 1) * sub_rows)
        o = of_ref[rows, :].astype(F32) + ob_ref[rows, :].astype(F32)
        heads = []
        for h in range(o.shape[-1] // HGRN_HEAD_DIM):
            oh = o[:, h * HGRN_HEAD_DIM:(h + 1) * HGRN_HEAD_DIM]
            heads.append(oh * lax.rsqrt(jnp.mean(oh * oh, axis=-1, keepdims=True) + EPS))
        b_mix = jnp.concatenate(heads, axis=-1) * gn_ref[...] * sg_ref[rows, :].astype(F32)

        xab = a_ref[rows, :].astype(BF16)
        xa = xab.astype(F32)
        lane_group = lax.broadcasted_iota(jnp.int32, xa.shape, 1) // (dp // n_groups)
        total = jnp.zeros_like(xa)
        for gi in range(n_groups):
            total = jnp.where(lane_group == gi, _dot(band_ref[gi], xab), total)
        y = total * icnt_ref[...] - xa
        a_mix = _dot(y.astype(BF16), wp_ref[...]) * ps_ref[...]

        mix = _dot(a_mix.astype(BF16), wo_ref[0:dp, :]) + _dot(b_mix.astype(BF16), wo_ref[dp:, :])
        o_ref[rows, :] = x_ref[rows, :] + gate_ref[...] * mix


def _ab_out(x, o_f, o_b, g, xa, mod5, layer, row_of_batch, g_norm, e, w_pool_bd, pool_scale, w_out, row_len,
            *, tm=1024, sub_rows=256):
    bsz, length, d = x.shape
    dh = o_f.shape[-1]
    dp = xa.shape[-1]
    tm = min(tm, length)
    sub_rows = min(sub_rows, tm)
    n_groups = len(POOL_WINDOWS)
    bands, inv = _pool_tables(sub_rows, row_len)
    icnt = jnp.asarray(np.repeat(inv, dp // n_groups, axis=1))
    _, _, gate_spec = _mod_block_specs(d, layer, 1, row_of_batch)
    return pl.pallas_call(
        _about_kernel,
        out_shape=jax.ShapeDtypeStruct(x.shape, F32),
        grid=(bsz, length // tm),
        in_specs=[_tok_spec(tm, d), _tok_spec(tm, dh), _tok_spec(tm, dh), _tok_spec(tm, dh), _tok_spec(tm, dp),
                  gate_spec, _pick_spec(g_norm.shape, (e,)), _const_spec((n_groups, sub_rows, sub_rows)),
                  _const_spec((sub_rows, dp)), _const_spec((dp, dp)), _pick_spec(pool_scale.shape, (e,)),
                  _const_spec(w_out.shape)],
        out_specs=_tok_spec(tm, d),
        compiler_params=_cparams(2),
        name="ab_out_proj",
    )(x, o_f, o_b, g, xa, mod5, g_norm, jnp.asarray(bands, BF16), icnt, w_pool_bd, pool_scale, w_out)


def _gelu_tanh(x):
    k1 = float(-2.0 * np.sqrt(2.0 / np.pi) * np.log2(np.e))
    k2 = 0.044715 * k1
    return x / (1.0 + jnp.exp2(x * (k1 + k2 * (x * x))))


def _gmlp_kernel(x_ref, shift_ref, scale_ref, gate_ref, g_ref, win_ref, lng_ref, lnb_ref, ws_ref, bs_ref,
                 wout_ref, o_ref, u_ref, v_ref, s_ref, *, sub_rows, n_chunk):
    dg = win_ref.shape[1] // 2
    for r in range(x_ref.shape[0] // sub_rows):
        rows = slice(r * sub_rows, (r + 1) * sub_rows)
        x = x_ref[rows, :]
        h = _adaln(x, g_ref[...], shift_ref[...], scale_ref[...]).astype(BF16)
        total = jnp.zeros((sub_rows, 1), F32)
        for c0 in range(0, dg, n_chunk):
            cols = slice(c0, c0 + n_chunk)
            u_ref[rows, cols] = _gelu_tanh(_dot(h, win_ref[:, c0:c0 + n_chunk]))
            zv = _gelu_tanh(_dot(h, win_ref[:, dg + c0:dg + c0 + n_chunk]))
            v_ref[rows, cols] = zv
            total = total + jnp.sum(zv, axis=-1, keepdims=True)
        mean = total / dg
        sq = jnp.zeros((sub_rows, 1), F32)
        for c0 in range(0, dg, n_chunk):
            vc = v_ref[rows, c0:c0 + n_chunk] - mean
            sq = sq + jnp.sum(vc * vc, axis=-1, keepdims=True)
        rstd = lax.rsqrt(sq / dg + EPS)
        for c in range(sub_rows // GMLP_CHUNK):
            crows = slice(c * GMLP_CHUNK, (c + 1) * GMLP_CHUNK)
            srows = slice(r * sub_rows + c * GMLP_CHUNK, r * sub_rows + (c + 1) * GMLP_CHUNK)
            for gi in range(dg // GMLP_GROUP_DIM):
                cols = slice(gi * GMLP_GROUP_DIM, (gi + 1) * GMLP_GROUP_DIM)
                vn = (v_ref[srows, cols] - mean[crows]) * rstd[crows] * lng_ref[:, cols] + lnb_ref[:, cols]
                sv = _dot(ws_ref[gi], vn.astype(BF16)) + bs_ref[gi]
                s_ref[srows, cols] = (u_ref[srows, cols] * sv).astype(BF16)
        o_ref[rows, :] = x + gate_ref[...] * _dot(s_ref[rows, :], wout_ref[...])


def _gmlp(x, mod5, layer, row_of_batch, norm_g4, o, w_in, ln_g, ln_b, w_s, b_s_wide, w_out, *, tm=1024,
          sub_rows=512, n_chunk=512):
    bsz, length, d = x.shape
    dg = w_in.shape[-1] // 2
    tm = min(tm, length)
    kern = functools.partial(_gmlp_kernel, sub_rows=min(sub_rows, tm), n_chunk=n_chunk)
    return pl.pallas_call(
        kern,
        out_shape=jax.ShapeDtypeStruct(x.shape, F32),
        grid=(bsz, length // tm),
        in_specs=[_tok_spec(tm, d)]
        + _mod_block_specs(d, layer, 1, row_of_batch)
        + [_pick_spec(norm_g4.shape, (layer, 1)), _const_spec(w_in.shape), _pick_spec(ln_g.shape, (o,)),
           _pick_spec(ln_b.shape, (o,)), _const_spec(w_s.shape), _pick_spec(b_s_wide.shape, (o,)),
           _const_spec(w_out.shape)],
        out_specs=_tok_spec(tm, d),
        scratch_shapes=[pltpu.VMEM((tm, dg), F32), pltpu.VMEM((tm, dg), F32), pltpu.VMEM((tm, dg), BF16)],
        compiler_params=_cparams(2),
        name="gmlp_mixer",
    )(x, mod5, mod5, mod5, norm_g4, w_in, ln_g, ln_b, w_s, b_s_wide, w_out)


def _block_diag(w):
    n, g, a, b = w.shape
    eye = jnp.eye(g, dtype=w.dtype)
    return (w[:, :, :, None, :] * eye[None, :, None, :, None]).reshape(n, g * a, g * b)


def kernel(x, c, ctx, c_ctx, mod_w, mod_b, norm_g, ffn_w1, ffn_w3, ffn_w2, ab_w_in, pool_w, pool_scale,
           hgrn_lb, hgrn_norm_g, ab_w_out, gmlp_w_in, gmlp_ln_g, gmlp_ln_b, gmlp_w_s, gmlp_b_s, gmlp_w_out,
           final_g):
    bsz, _, d = x.shape
    depth = mod_w.shape[0]
    d_pool = pool_scale.shape[-1]
    d_hgrn = hgrn_norm_g.shape[-1]
    n_heads = d_hgrn // HGRN_HEAD_DIM

    c_rows = jnp.concatenate([c, c_ctx[None, :]], axis=0)
    mod = _modulation(c_rows, mod_w, mod_b)
    mod5 = mod.reshape(depth, bsz + 1, N_MOD, 1, d)
    lat_row = lambda b: b
    ctx_row = lambda b: bsz

    norm_g4 = norm_g[:, :, None, :]
    pool_bd = _block_diag(pool_w).astype(BF16)
    pool_scale3, hgrn_norm_g3 = pool_scale[:, None, :], hgrn_norm_g[:, None, :]
    gm_ln_g, gm_ln_b = gmlp_ln_g[:, None, :], gmlp_ln_b[:, None, :]
    gm_b_wide = jnp.broadcast_to(gmlp_b_s[:, :, :, None], gmlp_b_s.shape + (GMLP_GROUP_DIM,))
    gmlp_w_s2 = gmlp_w_s.reshape(gmlp_w_s.shape[0], -1, gmlp_w_s.shape[-1])

    def ffn_sources(i, j):
        return [(ffn_w1, (i, j)), (ffn_w3, (i, j)), (ffn_w2, (i, j))]

    def mixer_sources(i):
        if i % 2 == 0:
            return [(ab_w_in, (i // 2,)), (ab_w_out, (i // 2,))]
        return [(gmlp_w_in, (i // 2,)), (gmlp_w_s2, (i // 2,)), (gmlp_w_out, (i // 2,))]

    ffn_order = [(i, j) for i in range(depth) for j in range(2)]
    ffn_w = {ffn_order[0]: [arr[lead].astype(BF16) for arr, lead in ffn_sources(*ffn_order[0])]}
    mixer_w = {}

    def latent_ffn(xl, i, j, **kw):
        k = ffn_order.index((i, j))
        cast = ffn_sources(*ffn_order[k + 1]) if k + 1 < len(ffn_order) else []
        n_next = len(cast)
        if j == 0:
            cast = cast + mixer_sources(i)
        xl, done = _ffn(xl, mod5, i, 2 * j, lat_row, norm_g4, *ffn_w[(i, j)], final_g, cast=cast, **kw)
        if n_next:
            ffn_w[ffn_order[k + 1]] = done[:n_next]
        if j == 0:
            mixer_w[i] = done[n_next:]
        return xl

    def context_ffn(xc, i, j):
        return _ffn(xc, mod5, i, 2 * j, ctx_row, norm_g4, *ffn_w[(i, j)], final_g)[0]

    xl, xc = x, ctx
    for i in range(depth):
        ctx_live = any(j % 2 == 0 for j in range(i, depth))
        last = i == depth - 1

        xl = latent_ffn(xl, i, 0)
        if ctx_live:
            xc = context_ffn(xc, i, 0)

        if i % 2 == 0:
            e = i // 2
            ab_in_w, ab_out_w = mixer_w[i]
            ab_in = functools.partial(_ab_in, mod5=mod5, layer=i, norm_g4=norm_g4, w_in=ab_in_w,
                                      hgrn_lb=hgrn_lb, slot=e, d_pool=d_pool)
            q_c, v_c, lf_c, kf_c, lb_c, kb_c, sg_c, a_c = ab_in(xc, row_of_batch=ctx_row)
            q_l, v_l, lf_l, kf_l, lb_l, kb_l, sg_l, a_l = ab_in(xl, row_of_batch=lat_row)
            s0 = jnp.zeros((bsz, 2, n_heads, HGRN_HEAD_DIM, HGRN_HEAD_DIM), F32)
            of_c, ob_c, s_ctx = _hgrn_scan(q_c, v_c, lf_c, kf_c, lb_c, kb_c, s0)
            of_l, ob_l, _ = _hgrn_scan(q_l, v_l, lf_l, kf_l, lb_l, kb_l, s_ctx)
            ab_out = functools.partial(_ab_out, mod5=mod5, layer=i, g_norm=hgrn_norm_g3, e=e,
                                       w_pool_bd=pool_bd[e], pool_scale=pool_scale3, w_out=ab_out_w)
            xc = ab_out(xc, of_c, ob_c, sg_c, a_c, row_of_batch=ctx_row, row_len=xc.shape[1])
            xl = ab_out(xl, of_l, ob_l, sg_l, a_l, row_of_batch=lat_row, row_len=GRID_W)
        else:
            o = i // 2
            gm_in, gm_s2, gm_out = mixer_w[i]
            gmlp = functools.partial(_gmlp, mod5=mod5, layer=i, norm_g4=norm_g4, o=o, w_in=gm_in, ln_g=gm_ln_g,
                                     ln_b=gm_ln_b, w_s=gm_s2.reshape(gmlp_w_s.shape[1:]), b_s_wide=gm_b_wide,
                                     w_out=gm_out)
            xl = gmlp(xl, row_of_batch=lat_row)
            if ctx_live:
                xc = gmlp(xc, row_of_batch=ctx_row)

        xl = latent_ffn(xl, i, 1, final_norm=last)
        if ctx_live:
            xc = context_ffn(xc, i, 1)
    return xl
```

```python
import functools

import numpy as np
import jax
import jax.numpy as jnp
from jax import lax
from jax.experimental import pallas as pl
from jax.experimental.pallas import tpu as pltpu

F32 = jnp.float32
BF16 = jnp.bfloat16

EPS = 1e-6
N_MOD = 9
GRID_W = 64
POOL_WINDOWS = (2, 4, 8, 16)
HGRN_HEAD_DIM = 128
GMLP_CHUNK = 128
GMLP_GROUP_DIM = 128

LOG2_E = float(np.log2(np.e))
HGRN_CHUNK = 128
HGRN_DIAG_BLOCK = 32
HGRN_SAFE_LOG2_SPAN = 80.0
MOD_COL_CHUNK = 768
VMEM_LIMIT_BYTES = 56 * 1024 * 1024


def _cparams(n_grid_dims):
    return pltpu.CompilerParams(
        dimension_semantics=("arbitrary",) * n_grid_dims,
        vmem_limit_bytes=VMEM_LIMIT_BYTES)


def _sigmoid(x):
    return 1.0 / (1.0 + jnp.exp2(x * (-LOG2_E)))


def _silu(x):
    return x * _sigmoid(x)


def _rms(x, g):
    return x * lax.rsqrt(jnp.mean(x * x, axis=-1, keepdims=True) + EPS) * g


def _adaln(x, g, shift, scale):
    return _rms(x, g) * (1.0 + scale) + shift


def _dot(a, b):
    return jnp.dot(a, b, preferred_element_type=F32)


def _dot_tn(a, b):
    return lax.dot_general(a, b, (((0,), (0,)), ((), ())), preferred_element_type=F32)


def _const_spec(shape):
    nd = len(shape)
    return pl.BlockSpec(shape, lambda *_: (0,) * nd, pipeline_mode=pl.Buffered(1))


def _pick_spec(full_shape, lead):
    tail = tuple(full_shape[len(lead):])
    idx = tuple(lead) + (0,) * len(tail)
    return pl.BlockSpec((None,) * len(lead) + tail, lambda *_: idx, pipeline_mode=pl.Buffered(1))


def _tok_spec(tm, n):
    return pl.BlockSpec((None, tm, n), lambda b, t: (b, t, 0))


def _split_bf16(x):
    hi = x.astype(BF16)
    return hi, (x - hi.astype(F32)).astype(BF16)


def _mod_kernel(c_ref, w_ref, b_ref, o_ref):
    a_hi, a_lo = _split_bf16(_silu(c_ref[...]))
    for g0 in range(0, w_ref.shape[1], MOD_COL_CHUNK):
        cols = slice(g0, g0 + MOD_COL_CHUNK)
        w_hi, w_lo = _split_bf16(w_ref[:, cols])
        o_ref[:, cols] = _dot(a_hi, w_hi) + _dot(a_hi, w_lo) + _dot(a_lo, w_hi) + b_ref[:, cols]


def _modulation(c_rows, mod_w, mod_b, tn=2304):
    depth, d, n = mod_w.shape
    rows = 16
    c_pad = jnp.zeros((rows, d), F32).at[:c_rows.shape[0]].set(c_rows)
    return pl.pallas_call(
        _mod_kernel,
        out_shape=jax.ShapeDtypeStruct((depth, rows, n), F32),
        grid=(depth, n // tn),
        in_specs=[
            _const_spec((rows, d)),
            pl.BlockSpec((None, d, tn), lambda l, j: (l, 0, j)),
            pl.BlockSpec((None, 1, tn), lambda l, j: (l, 0, j)),
        ],
        out_specs=pl.BlockSpec((None, rows, tn), lambda l, j: (l, 0, j)),
        compiler_params=_cparams(2),
        name="modulation",
    )(c_pad, mod_w, mod_b.reshape(depth, 1, n))


def _mod_index(layer, col, row_of_batch, b, t):
    return (layer, row_of_batch(b), col, 0, 0)


def _mod_block_specs(d_model, layer, sub, row_of_batch):
    return [pl.BlockSpec((None, None, None, 1, d_model),
                         functools.partial(_mod_index, layer, 3 * sub + k, row_of_batch))
            for k in range(3)]


def _ffn_kernel(x_ref, shift_ref, scale_ref, gate_ref, g_ref, w1_ref, w3_ref, w2_ref, fg_ref, *rest,
                n_cast, n_chunk, sub_rows, final_norm):
    cast_in, o_ref, cast_out, a_ref = rest[:n_cast], rest[n_cast], rest[n_cast + 1:-1], rest[-1]
    for src, dst in zip(cast_in, cast_out):
        dst[...] = src[...].astype(BF16)
    d_ff = w1_ref.shape[1]
    for r in range(x_ref.shape[0] // sub_rows):
        rows = slice(r * sub_rows, (r + 1) * sub_rows)
        x = x_ref[rows, :]
        h = _adaln(x, g_ref[...], shift_ref[...], scale_ref[...]).astype(BF16)
        for c in range(d_ff // n_chunk):
            sl = slice(c * n_chunk, (c + 1) * n_chunk)
            u = _dot(h, w1_ref[:, sl])
            v = _dot(h, w3_ref[:, sl])
            a_ref[rows, sl] = (_silu(u) * v).astype(BF16)
        y = _dot(a_ref[rows, :], w2_ref[...])
        out = x + (0.5 * gate_ref[...]) * y
        if final_norm:
            out = _rms(out, fg_ref[...])
        o_ref[rows, :] = out


def _ffn(x, mod5, layer, sub, row_of_batch, norm_g4, w1, w3, w2, final_g, *, cast=(), final_norm=False,
         tm=1024, n_chunk=256, sub_rows=256):
    bsz, length, d = x.shape
    d_ff = w1.shape[-1]
    tm = min(tm, length)
    n_t = length // tm
    n_steps = bsz * n_t
    bf16_rows = 16
    cast_in_specs, cast_out_specs, cast_shapes = [], [], []
    for arr, lead in cast:
        rows, cols = arr.shape[len(lead):]
        band = rows // n_steps
        assert band * n_steps == rows and band % bf16_rows == 0, (arr.shape, n_steps)
        cast_in_specs.append(pl.BlockSpec(
            (None,) * len(lead) + (band, cols),
            functools.partial(lambda lead, b, t: tuple(lead) + (b * n_t + t, 0), lead)))
        cast_out_specs.append(pl.BlockSpec((band, cols), lambda b, t: (b * n_t + t, 0)))
        cast_shapes.append(jax.ShapeDtypeStruct((rows, cols), BF16))
    kern = functools.partial(_ffn_kernel, n_cast=len(cast), n_chunk=n_chunk, sub_rows=min(sub_rows, tm),
                             final_norm=final_norm)
    out = pl.pallas_call(
        kern,
        out_shape=[jax.ShapeDtypeStruct(x.shape, F32)] + cast_shapes,
        grid=(bsz, n_t),
        in_specs=[_tok_spec(tm, d)]
        + _mod_block_specs(d, layer, sub, row_of_batch)
        + [_pick_spec(norm_g4.shape, (layer, sub)), _const_spec(w1.shape), _const_spec(w3.shape),
           _const_spec(w2.shape), _const_spec((1, d))]
        + cast_in_specs,
        out_specs=[_tok_spec(tm, d)] + cast_out_specs,
        scratch_shapes=[pltpu.VMEM((tm, d_ff), BF16)],
        compiler_params=_cparams(2),
        name="swiglu_ffn",
    )(x, mod5, mod5, mod5, norm_g4, w1, w3, w2, final_g.reshape(1, d), *[arr for arr, _ in cast])
    return out[0], list(out[1:])


def _hgrn_lower_bounds(raw, slot):
    ex = jnp.exp(raw - jnp.max(raw, axis=0, keepdims=True))
    sm = ex / jnp.sum(ex, axis=0, keepdims=True)
    lb = sm[0]
    for k in range(1, slot + 1):
        lb = lb + sm[k]
    return lb


def _abin_kernel(x_ref, shift_ref, scale_ref, g_ref, w_ref, lb_ref, q_ref, v_ref, lff_ref, kf_ref, lfb_ref,
                 kb_ref, sg_ref, p_ref, *, slot, sub_rows, n_chunk):
    dp, dh = p_ref.shape[-1], q_ref.shape[-1]
    lb_all = _hgrn_lower_bounds(lb_ref[...], slot)
    for r in range(x_ref.shape[0] // sub_rows):
        rows = slice(r * sub_rows, (r + 1) * sub_rows)
        h = _adaln(x_ref[rows, :], g_ref[...], shift_ref[...], scale_ref[...]).astype(BF16)

        p_ref[rows, :] = _dot(h, w_ref[:, 0:dp]).astype(BF16)
        for k in (2, 1, 3, 0, 4):
            for c0 in range(0, dh, n_chunk):
                cols = slice(c0, c0 + n_chunk)
                z = _dot(h, w_ref[:, dp + k * dh + c0:dp + k * dh + c0 + n_chunk])
                if k == 0:
                    q_ref[rows, cols] = (_silu(z) * HGRN_HEAD_DIM ** -0.5).astype(BF16)
                elif k == 1:
                    v_ref[rows, cols] = z.astype(BF16)
                elif k == 4:
                    sg_ref[rows, cols] = _silu(z).astype(BF16)
                else:
                    lf_ref, kk_ref = ((lff_ref, kf_ref), (lfb_ref, kb_ref))[k - 2]
                    lb = lb_all[k - 2:k - 1, cols]
                    forget = lb + (1.0 - lb) * _sigmoid(z)
                    lf_ref[rows, cols] = jnp.log(forget) * LOG2_E
                    kk_ref[rows, cols] = (1.0 - forget).astype(BF16)


def _ab_in(x, mod5, layer, row_of_batch, norm_g4, w_in, hgrn_lb, slot, d_pool, *, tm=1024, sub_rows=256,
           n_chunk=768):
    bsz, length, d = x.shape
    tm = min(tm, length)
    dh = (w_in.shape[-1] - d_pool) // 5
    shift_spec, scale_spec, _ = _mod_block_specs(d, layer, 1, row_of_batch)
    kern = functools.partial(_abin_kernel, slot=slot, sub_rows=min(sub_rows, tm), n_chunk=n_chunk)
    widths = (dh, dh, dh, dh, dh, dh, dh, d_pool)
    dtypes = (BF16, BF16, F32, BF16, F32, BF16, BF16, BF16)
    return pl.pallas_call(
        kern,
        out_shape=[jax.ShapeDtypeStruct((bsz, length, n), dt) for n, dt in zip(widths, dtypes)],
        grid=(bsz, length // tm),
        in_specs=[_tok_spec(tm, d), shift_spec, scale_spec, _pick_spec(norm_g4.shape, (layer, 1)),
                  _const_spec(w_in.shape), _const_spec(hgrn_lb.shape)],
        out_specs=[_tok_spec(tm, n) for n in widths],
        compiler_params=_cparams(2),
        name="ab_in_proj",
    )(x, mod5, mod5, norm_g4, w_in, hgrn_lb)


def _level_table(c, diag_block):
    j = np.arange(c)[:, None]
    i = np.arange(c)[None, :]
    x = j ^ i
    bits = np.zeros((c, c), np.int32)
    for k in range(int(np.log2(c))):
        bits = np.where(x >= (1 << k), k + 1, bits)
    lvl = np.where(i < j, bits, 0)
    if diag_block > 1:
        lvl = np.where((i <= j) & (bits <= int(np.log2(diag_block))), int(np.log2(diag_block)), lvl)
    return lvl.astype(np.int32)


def _fine_decay(forget, rev):
    c, w = forget.shape
    f3 = forget.reshape(c // 8, 8, w)
    pos = lax.broadcasted_iota(jnp.int32, (c // 8, 8, w), 1)

    def at(offset):
        return f3 if offset == 0 else pltpu.roll(f3, (-offset) % 8, 1)

    sgn = 1 if rev else -1
    cq = [at(0)]
    ck = [None]
    for t in range(1, 4):
        cq.append(cq[-1] * at(sgn * t))
        ck.append(at(-sgn * t) if ck[-1] is None else ck[-1] * at(-sgn * t))
    out = {}
    for s in (2, 4, 8):
        half = s // 2
        p = pos % s
        e = jnp.ones_like(f3)
        for t in range(half):
            q_pos = (half - 1 - t) if rev else (half + t)
            k_pos = (half + t) if rev else (half - 1 - t)
            e = jnp.where(p == q_pos, cq[t], e)
            if t > 0:
                e = jnp.where(p == k_pos, ck[t], e)
        out[s] = e.reshape(c, w)
    return out


def _block_row(b, s, ridx):
    c, w = b.shape
    r = b.reshape(c // s, s, w)[:, ridx:ridx + 1, :]
    return jnp.broadcast_to(r, (c // s, s, w)).reshape(c, w)


def _key_decay_column(b_end, shape):
    return jnp.broadcast_to(jnp.exp2(b_end), shape).T


def _hgrn_head(qq, kk, v, forget, b, st, lvl, rev):
    c = qq.shape[0]
    b_end = b[0:1] if rev else b[c - 1:c]
    vb = v.astype(BF16)
    o = _dot((qq * jnp.exp2(b)).astype(BF16), st.astype(BF16))
    k_out = (kk * jnp.exp2(b_end - b)).astype(BF16)
    st_new = st * _key_decay_column(b_end, st.shape) + _dot_tn(k_out, vb)

    row = lax.broadcasted_iota(jnp.int32, (c, 1), 0)
    fine = _fine_decay(forget, rev)
    a = jnp.zeros((c, c), F32)
    s = c
    while s > 1:
        half = s // 2
        q_side = ((row % s) >= half) != rev
        if s >= 16:
            e = jnp.exp2(-jnp.abs(b - _block_row(b, s, half if rev else half - 1)))
        else:
            e = fine[s]
        x = jnp.where(q_side, qq, kk) * e
        a = jnp.where(lvl == int(np.log2(s)), _dot(x.astype(BF16), x.T.astype(BF16)), a)
        s //= 2
    o = o + _dot(a.astype(BF16), vb) + jnp.sum(qq * kk, axis=-1, keepdims=True) * v.astype(F32)
    return o, st_new


def _hgrn_mild_direction(rev, qq_ref, kk_ref, b_ref, v_ref, o_ref, st_ref, lvl, ops_ref, a_ref):
    c = qq_ref.shape[0]
    n_heads = st_ref.shape[0]
    sizes = []
    s = c
    while s > HGRN_DIAG_BLOCK:
        sizes.append(s)
        s //= 2

    for h in range(n_heads):
        sl = slice(h * HGRN_HEAD_DIM, (h + 1) * HGRN_HEAD_DIM)
        qq, kk, b = qq_ref[:, sl], kk_ref[:, sl], b_ref[:, sl]
        b_end = b[0:1] if rev else b[c - 1:c]
        st = st_ref[h]
        ops_ref[h, 0] = qq * jnp.exp2(b).astype(BF16)
        ops_ref[h, 1] = kk * jnp.exp2(b_end - b).astype(BF16)
        ops_ref[h, 2] = st.astype(BF16)
        st_ref[h] = st * _key_decay_column(b_end, st.shape)
        for n, s in enumerate(sizes):
            half = s // 2
            r = _block_row(b, s, half if rev else half - 1)
            q_side = [((r0 % s) >= half) != rev for r0 in range(0, c, half)]
            blocks = [slice(r0, r0 + half) for r0 in range(0, c, half)]
            diff = jnp.concatenate([b[rs] - r[rs] if qs else r[rs] - b[rs] for rs, qs in zip(blocks, q_side)],
                                   axis=0)
            picks = jnp.concatenate([(qq if qs else kk)[rs] for rs, qs in zip(blocks, q_side)], axis=0)
            x = picks * jnp.exp2(diff).astype(BF16)
            ops_ref[h, 3 + 2 * n] = x
            ops_ref[h, 4 + 2 * n] = x.T
        d = b - _block_row(b, HGRN_DIAG_BLOCK, HGRN_DIAG_BLOCK - 1 if rev else 0)
        ops_ref[h, 3 + 2 * len(sizes)] = qq * jnp.exp2(d).astype(BF16)
        ops_ref[h, 4 + 2 * len(sizes)] = (kk * jnp.exp2(-d).astype(BF16)).T

    for h in range(n_heads):
        a = jnp.zeros((c, c), F32)
        for n, s in enumerate(sizes + [HGRN_DIAG_BLOCK]):
            a = jnp.where(lvl == int(np.log2(s)), _dot(ops_ref[h, 3 + 2 * n], ops_ref[h, 4 + 2 * n]), a)
        a_ref[h] = a.astype(BF16)

    for h in range(n_heads):
        sl = slice(h * HGRN_HEAD_DIM, (h + 1) * HGRN_HEAD_DIM)
        vb = v_ref[:, sl].astype(BF16)
        o = _dot(ops_ref[h, 0], ops_ref[h, 2]) + _dot(a_ref[h], vb)
        o_ref[:, sl] = o.astype(o_ref.dtype)
        st_ref[h] = st_ref[h] + _dot_tn(ops_ref[h, 1], vb)


def _hgrn_kernel(qf_ref, vf_ref, lf_ref, kf_ref, qb_ref, vb_ref, lb_ref, kb_ref, s0_ref, tri_ref, lvl_ref,
                 of_ref, ob_ref, sout_ref, s_scr, b_s, span_s, ops_s, a_s):
    t = pl.program_id(1)

    @pl.when(t == 0)
    def _():
        s_scr[...] = s0_ref[...]

    c = HGRN_CHUNK
    n_sub = qf_ref.shape[0] // c
    n_heads = qf_ref.shape[-1] // HGRN_HEAD_DIM

    for i in range(n_sub):
        weakest = None
        for d, (logf_ref, off) in enumerate(((lf_ref, i * c), (lb_ref, (n_sub - 1 - i) * c))):
            logf = logf_ref[off:off + c, :]
            hi = logf.astype(BF16)
            lo = (logf - hi.astype(F32)).astype(BF16)
            b = _dot(tri_ref[d], jnp.concatenate([hi, lo], axis=0))
            b_s[i, d] = b
            for m in range(c // HGRN_DIAG_BLOCK):
                lo_row, hi_row = m * HGRN_DIAG_BLOCK, (m + 1) * HGRN_DIAG_BLOCK - 1
                if d == 0:
                    span = b[hi_row:hi_row + 1] - b[lo_row:lo_row + 1]
                else:
                    span = b[lo_row:lo_row + 1] - b[hi_row:hi_row + 1]
                weakest = span if weakest is None else jnp.minimum(weakest, span)
        span_s[i] = jnp.min(weakest)

    def chunk(i, carry):
        offs = (pl.multiple_of(i * c, c), pl.multiple_of((n_sub - 1 - i) * c, c))
        dirs = tuple(tuple(ref.at[pl.ds(off, c), :] for ref in refs)
                     for off, refs in zip(offs, ((qf_ref, vf_ref, lf_ref, kf_ref, of_ref),
                                                 (qb_ref, vb_ref, lb_ref, kb_ref, ob_ref))))
        in_range = span_s[i] > -HGRN_SAFE_LOG2_SPAN

        @pl.when(in_range)
        def _():
            for d, (q_ref, v_ref, logf_ref, kk_ref, o_ref) in enumerate(dirs):
                _hgrn_mild_direction(d == 1, q_ref, kk_ref, b_s.at[i, d], v_ref, o_ref, s_scr.at[d],
                                     lvl_ref[d], ops_s.at[d], a_s.at[d])

        @pl.when(jnp.logical_not(in_range))
        def _():
            for d, (q_ref, v_ref, logf_ref, kk_ref, o_ref) in enumerate(dirs):
                lvl = lvl_ref[2 + d]
                for h in range(n_heads):
                    sl = slice(h * HGRN_HEAD_DIM, (h + 1) * HGRN_HEAD_DIM)
                    o_h, st_new = _hgrn_head(q_ref[:, sl].astype(F32), kk_ref[:, sl].astype(F32), v_ref[:, sl],
                                             jnp.exp2(logf_ref[:, sl]), b_s[i, d, :, sl], s_scr[d, h], lvl,
                                             d == 1)
                    o_ref[:, sl] = o_h.astype(o_ref.dtype)
                    s_scr[d, h] = st_new

        return carry

    lax.fori_loop(0, n_sub, chunk, 0)

    @pl.when(t == pl.num_programs(1) - 1)
    def _():
        sout_ref[...] = s_scr[...]


def _hgrn_scan(q, v, logf_fwd, kk_fwd, logf_bwd, kk_bwd, s0, *, rows=512):
    bsz, length, dh = q.shape
    c = HGRN_CHUNK
    rows = min(rows, length)
    n = length // rows
    n_heads = dh // HGRN_HEAD_DIM
    assert c == HGRN_HEAD_DIM and c > HGRN_DIAG_BLOCK
    n_mild = int(np.log2(c // HGRN_DIAG_BLOCK)) + 1
    lvl_fast = _level_table(c, HGRN_DIAG_BLOCK)
    lvl_full = _level_table(c, 1)
    lvl = jnp.asarray(np.stack([lvl_fast, lvl_fast.T, lvl_full, lvl_full.T]))
    tri_f = np.tril(np.ones((c, c), np.float32))
    tri = jnp.asarray(np.stack([np.tile(tri_f, (1, 2)), np.tile(tri_f.T, (1, 2))]), BF16)
    fwd = pl.BlockSpec((None, rows, dh), lambda b, t: (b, t, 0))
    bwd = pl.BlockSpec((None, rows, dh), lambda b, t: (b, n - 1 - t, 0))
    s_shape = (2, n_heads, HGRN_HEAD_DIM, HGRN_HEAD_DIM)
    s_spec = pl.BlockSpec((None,) + s_shape, lambda b, t: (b, 0, 0, 0, 0))
    return pl.pallas_call(
        _hgrn_kernel,
        out_shape=[jax.ShapeDtypeStruct(q.shape, BF16), jax.ShapeDtypeStruct(q.shape, BF16),
                   jax.ShapeDtypeStruct((bsz,) + s_shape, F32)],
        grid=(bsz, n),
        in_specs=[fwd, fwd, fwd, fwd, bwd, bwd, bwd, bwd, s_spec, _const_spec((2, c, 2 * c)),
                  _const_spec((4, c, c))],
        out_specs=[fwd, bwd, s_spec],
        scratch_shapes=[pltpu.VMEM(s_shape, F32), pltpu.VMEM((rows // c, 2, c, dh), F32),
                        pltpu.SMEM((rows // c,), F32),
                        pltpu.VMEM((2, n_heads, 3 + 2 * n_mild, c, HGRN_HEAD_DIM), BF16),
                        pltpu.VMEM((2, n_heads, c, c), BF16)],
        compiler_params=_cparams(2),
        name="hgrn2_scan",
    )(q, v, logf_fwd, kk_fwd, q, v, logf_bwd, kk_bwd, s0, tri, lvl)


def _pool_tables(tm, row_len):
    t = np.arange(tm)
    same_row = (t[:, None] // row_len) == (t[None, :] // row_len)
    pos = t % row_len
    bands, inv = [], []
    for w in POOL_WINDOWS:
        lo = np.clip(pos - w // 2, 0, row_len)
        hi = np.clip(pos - w // 2 + w, 0, row_len)
        u = pos[None, :]
        bands.append(same_row & (u >= lo[:, None]) & (u < hi[:, None]))
        inv.append(1.0 / (hi - lo).astype(np.float64))
    return np.stack(bands).astype(np.float32), np.stack(inv, axis=1).astype(np.float32)


def _about_kernel(x_ref, of_ref, ob_ref, sg_ref, a_ref, gate_ref, gn_ref, band_ref, icnt_ref, wp_ref,
                  ps_ref, wo_ref, o_ref):
    sub_rows = band_ref.shape[-1]
    dp = a_ref.shape[-1]
    n_groups = band_ref.shape[0]
    for r in range(x_ref.shape[0] // sub_rows):
        rows = slice(r * sub_rows, (r + 1) * sub_rows)
        o = of_ref[rows, :].astype(F32) + ob_ref[rows, :].astype(F32)
        heads = []
        for h in range(o.shape[-1] // HGRN_HEAD_DIM):
            oh = o[:, h * HGRN_HEAD_DIM:(h + 1) * HGRN_HEAD_DIM]
            heads.append(oh * lax.rsqrt(jnp.mean(oh * oh, axis=-1, keepdims=True) + EPS))
        b_mix = jnp.concatenate(heads, axis=-1) * gn_ref[...] * sg_ref[rows, :].astype(F32)

        xab = a_ref[rows, :].astype(BF16)
        xa = xab.astype(F32)
        lane_group = lax.broadcasted_iota(jnp.int32, xa.shape, 1) // (dp // n_groups)
        total = jnp.zeros_like(xa)
        for gi in range(n_groups):
            total = jnp.where(lane_group == gi, _dot(band_ref[gi], xab), total)
        y = total * icnt_ref[...] - xa
        a_mix = _dot(y.astype(BF16), wp_ref[...]) * ps_ref[...]

        mix = _dot(a_mix.astype(BF16), wo_ref[0:dp, :]) + _dot(b_mix.astype(BF16), wo_ref[dp:, :])
        o_ref[rows, :] = x_ref[rows, :] + gate_ref[...] * mix


def _ab_out(x, o_f, o_b, g, xa, mod5, layer, row_of_batch, g_norm, e, w_pool_bd, pool_scale, w_out, row_len,
            *, tm=1024, sub_rows=256):
    bsz, length, d = x.shape
    dh = o_f.shape[-1]
    dp = xa.shape[-1]
    tm = min(tm, length)
    sub_rows = min(sub_rows, tm)
    n_groups = len(POOL_WINDOWS)
    bands, inv = _pool_tables(sub_rows, row_len)
    icnt = jnp.asarray(np.repeat(inv, dp // n_groups, axis=1))
    _, _, gate_spec = _mod_block_specs(d, layer, 1, row_of_batch)
    return pl.pallas_call(
        _about_kernel,
        out_shape=jax.ShapeDtypeStruct(x.shape, F32),
        grid=(bsz, length // tm),
        in_specs=[_tok_spec(tm, d), _tok_spec(tm, dh), _tok_spec(tm, dh), _tok_spec(tm, dh), _tok_spec(tm, dp),
                  gate_spec, _pick_spec(g_norm.shape, (e,)), _const_spec((n_groups, sub_rows, sub_rows)),
                  _const_spec((sub_rows, dp)), _const_spec((dp, dp)), _pick_spec(pool_scale.shape, (e,)),
                  _const_spec(w_out.shape)],
        out_specs=_tok_spec(tm, d),
        compiler_params=_cparams(2),
        name="ab_out_proj",
    )(x, o_f, o_b, g, xa, mod5, g_norm, jnp.asarray(bands, BF16), icnt, w_pool_bd, pool_scale, w_out)


def _gelu_tanh(x):
    k1 = float(-2.0 * np.sqrt(2.0 / np.pi) * np.log2(np.e))
    k2 = 0.044715 * k1
    return x / (1.0 + jnp.exp2(x * (k1 + k2 * (x * x))))


def _gmlp_kernel(x_ref, shift_ref, scale_ref, gate_ref, g_ref, win_ref, lng_ref, lnb_ref, ws_ref, bs_ref,
                 wout_ref, o_ref, u_ref, v_ref, s_ref, *, sub_rows, n_chunk):
    dg = win_ref.shape[1] // 2
    n_sub = x_ref.shape[0] // sub_rows
    means = []
    for r in range(n_sub):
        rows = slice(r * sub_rows, (r + 1) * sub_rows)
        h = _adaln(x_ref[rows, :], g_ref[...], shift_ref[...], scale_ref[...]).astype(BF16)
        total = jnp.zeros((sub_rows, 1), F32)
        for c0 in range(0, dg, n_chunk):
            cols = slice(c0, c0 + n_chunk)
            u_ref[rows, cols] = _gelu_tanh(_dot(h, win_ref[:, c0:c0 + n_chunk]))
            zv = _gelu_tanh(_dot(h, win_ref[:, dg + c0:dg + c0 + n_chunk]))
            v_ref[rows, cols] = zv
            total = total + jnp.sum(zv, axis=-1, keepdims=True)
        means.append(total / dg)
    for r in range(n_sub):
        rows = slice(r * sub_rows, (r + 1) * sub_rows)
        mean = means[r]
        sq = jnp.zeros((sub_rows, 1), F32)
        for c0 in range(0, dg, n_chunk):
            vc = v_ref[rows, c0:c0 + n_chunk] - mean
            sq = sq + jnp.sum(vc * vc, axis=-1, keepdims=True)
        rstd = lax.rsqrt(sq / dg + EPS)
        for c in range(sub_rows // GMLP_CHUNK):
            crows = slice(c * GMLP_CHUNK, (c + 1) * GMLP_CHUNK)
            srows = slice(r * sub_rows + c * GMLP_CHUNK, r * sub_rows + (c + 1) * GMLP_CHUNK)
            for gi in range(dg // GMLP_GROUP_DIM):
                cols = slice(gi * GMLP_GROUP_DIM, (gi + 1) * GMLP_GROUP_DIM)
                vn = (v_ref[srows, cols] - mean[crows]) * rstd[crows] * lng_ref[:, cols] + lnb_ref[:, cols]
                sv = _dot(ws_ref[gi], vn.astype(BF16)) + bs_ref[gi]
                s_ref[srows, cols] = (u_ref[srows, cols] * sv).astype(BF16)
        o_ref[rows, :] = x_ref[rows, :] + gate_ref[...] * _dot(s_ref[rows, :], wout_ref[...])


def _gmlp(x, mod5, layer, row_of_batch, norm_g4, o, w_in, ln_g, ln_b, w_s, b_s_wide, w_out, *, tm=1024,
          sub_rows=512, n_chunk=512):
    bsz, length, d = x.shape
    dg = w_in.shape[-1] // 2
    tm = min(tm, length)
    kern = functools.partial(_gmlp_kernel, sub_rows=min(sub_rows, tm), n_chunk=n_chunk)
    return pl.pallas_call(
        kern,
        out_shape=jax.ShapeDtypeStruct(x.shape, F32),
        grid=(bsz, length // tm),
        in_specs=[_tok_spec(tm, d)]
        + _mod_block_specs(d, layer, 1, row_of_batch)
        + [_pick_spec(norm_g4.shape, (layer, 1)), _const_spec(w_in.shape), _pick_spec(ln_g.shape, (o,)),
           _pick_spec(ln_b.shape, (o,)), _const_spec(w_s.shape), _pick_spec(b_s_wide.shape, (o,)),
           _const_spec(w_out.shape)],
        out_specs=_tok_spec(tm, d),
        scratch_shapes=[pltpu.VMEM((tm, dg), F32), pltpu.VMEM((tm, dg), F32), pltpu.VMEM((tm, dg), BF16)],
        compiler_params=_cparams(2),
        name="gmlp_mixer",
    )(x, mod5, mod5, mod5, norm_g4, w_in, ln_g, ln_b, w_s, b_s_wide, w_out)


def _block_diag(w):
    n, g, a, b = w.shape
    eye = jnp.eye(g, dtype=w.dtype)
    return (w[:, :, :, None, :] * eye[None, :, None, :, None]).reshape(n, g * a, g * b)


def kernel(x, c, ctx, c_ctx, mod_w, mod_b, norm_g, ffn_w1, ffn_w3, ffn_w2, ab_w_in, pool_w, pool_scale,
           hgrn_lb, hgrn_norm_g, ab_w_out, gmlp_w_in, gmlp_ln_g, gmlp_ln_b, gmlp_w_s, gmlp_b_s, gmlp_w_out,
           final_g):
    bsz, _, d = x.shape
    depth = mod_w.shape[0]
    d_pool = pool_scale.shape[-1]
    d_hgrn = hgrn_norm_g.shape[-1]
    n_heads = d_hgrn // HGRN_HEAD_DIM

    c_rows = jnp.concatenate([c, c_ctx[None, :]], axis=0)
    mod = _modulation(c_rows, mod_w, mod_b)
    mod5 = mod.reshape(depth, mod.shape[1], N_MOD, 1, d)
    lat_row = lambda b: b
    ctx_row = lambda b: bsz

    norm_g4 = norm_g[:, :, None, :]
    pool_bd = _block_diag(pool_w).astype(BF16)
    pool_scale3, hgrn_norm_g3 = pool_scale[:, None, :], hgrn_norm_g[:, None, :]
    gm_ln_g, gm_ln_b = gmlp_ln_g[:, None, :], gmlp_ln_b[:, None, :]
    gm_b_wide = jnp.broadcast_to(gmlp_b_s[:, :, :, None], gmlp_b_s.shape + (GMLP_GROUP_DIM,))
    gmlp_w_s2 = gmlp_w_s.reshape(gmlp_w_s.shape[0], -1, gmlp_w_s.shape[-1])

    def ffn_sources(i, j):
        return [(ffn_w1, (i, j)), (ffn_w3, (i, j)), (ffn_w2, (i, j))]

    def mixer_sources(i):
        if i % 2 == 0:
            return [(ab_w_in, (i // 2,)), (ab_w_out, (i // 2,))]
        return [(gmlp_w_in, (i // 2,)), (gmlp_w_s2, (i // 2,)), (gmlp_w_out, (i // 2,))]

    ffn_order = [(i, j) for i in range(depth) for j in range(2)]
    ffn_w = {ffn_order[0]: [arr[lead].astype(BF16) for arr, lead in ffn_sources(*ffn_order[0])]}
    mixer_w = {}

    def latent_ffn(xl, i, j, **kw):
        k = ffn_order.index((i, j))
        cast = ffn_sources(*ffn_order[k + 1]) if k + 1 < len(ffn_order) else []
        n_next = len(cast)
        if j == 0:
            cast = cast + mixer_sources(i)
        xl, done = _ffn(xl, mod5, i, 2 * j, lat_row, norm_g4, *ffn_w[(i, j)], final_g, cast=cast, **kw)
        if n_next:
            ffn_w[ffn_order[k + 1]] = done[:n_next]
        if j == 0:
            mixer_w[i] = done[n_next:]
        return xl

    def context_ffn(xc, i, j):
        return _ffn(xc, mod5, i, 2 * j, ctx_row, norm_g4, *ffn_w[(i, j)], final_g)[0]

    xl, xc = x, ctx
    for i in range(depth):
        ctx_live = any(j % 2 == 0 for j in range(i, depth))
        last = i == depth - 1

        xl = latent_ffn(xl, i, 0)
        if ctx_live:
            xc = context_ffn(xc, i, 0)

        if i % 2 == 0:
            e = i // 2
            ab_in_w, ab_out_w = mixer_w[i]
            ab_in = functools.partial(_ab_in, mod5=mod5, layer=i, norm_g4=norm_g4, w_in=ab_in_w,
                                      hgrn_lb=hgrn_lb, slot=e, d_pool=d_pool)
            q_c, v_c, lf_c, kf_c, lb_c, kb_c, sg_c, a_c = ab_in(xc, row_of_batch=ctx_row)
            q_l, v_l, lf_l, kf_l, lb_l, kb_l, sg_l, a_l = ab_in(xl, row_of_batch=lat_row)
            s0 = jnp.zeros((bsz, 2, n_heads, HGRN_HEAD_DIM, HGRN_HEAD_DIM), F32)
            of_c, ob_c, s_ctx = _hgrn_scan(q_c, v_c, lf_c, kf_c, lb_c, kb_c, s0)
            of_l, ob_l, _ = _hgrn_scan(q_l, v_l, lf_l, kf_l, lb_l, kb_l, s_ctx)
            ab_out = functools.partial(_ab_out, mod5=mod5, layer=i, g_norm=hgrn_norm_g3, e=e,
                                       w_pool_bd=pool_bd[e], pool_scale=pool_scale3, w_out=ab_out_w)
            xc = ab_out(xc, of_c, ob_c, sg_c, a_c, row_of_batch=ctx_row, row_len=xc.shape[1])
            xl = ab_out(xl, of_l, ob_l, sg_l, a_l, row_of_batch=lat_row, row_len=GRID_W)
        else:
            o = i // 2
            gm_in, gm_s2, gm_out = mixer_w[i]
            gmlp = functools.partial(_gmlp, mod5=mod5, layer=i, norm_g4=norm_g4, o=o, w_in=gm_in, ln_g=gm_ln_g,
                                     ln_b=gm_ln_b, w_s=gm_s2.reshape(gmlp_w_s.shape[1:]), b_s_wide=gm_b_wide,
                                     w_out=gm_out)
            xl = gmlp(xl, row_of_batch=lat_row)
            if ctx_live:
                xc = gmlp(xc, row_of_batch=ctx_row)

        xl = latent_ffn(xl, i, 1, final_norm=last)
        if ctx_live:
            xc = context_ffn(xc, i, 1)
    return xl
```

```python
import functools

import numpy as np
import jax
import jax.numpy as jnp
from jax import lax
from jax.experimental import pallas as pl
from jax.experimental.pallas import tpu as pltpu

F32 = jnp.float32
BF16 = jnp.bfloat16

EPS = 1e-6
N_MOD = 9
GRID_W = 64
POOL_WINDOWS = (2, 4, 8, 16)
HGRN_HEAD_DIM = 128
GMLP_CHUNK = 128
GMLP_GROUP_DIM = 128

LOG2_E = float(np.log2(np.e))
HGRN_CHUNK = 128
HGRN_DIAG_BLOCK = 32
HGRN_SAFE_LOG2_SPAN = 100.0
MOD_COL_CHUNK = 768
VMEM_LIMIT_BYTES = 56 * 1024 * 1024


def _cparams(n_grid_dims):
    return pltpu.CompilerParams(
        dimension_semantics=("arbitrary",) * n_grid_dims,
        vmem_limit_bytes=VMEM_LIMIT_BYTES)


def _sigmoid(x):
    return 1.0 / (1.0 + jnp.exp2(x * (-LOG2_E)))


def _silu(x):
    return x * _sigmoid(x)


def _rms(x, g):
    return x * lax.rsqrt(jnp.mean(x * x, axis=-1, keepdims=True) + EPS) * g


def _adaln(x, g, shift, scale):
    return _rms(x, g) * (1.0 + scale) + shift


def _dot(a, b):
    return jnp.dot(a, b, preferred_element_type=F32)


def _dot_tn(a, b):
    return lax.dot_general(a, b, (((0,), (0,)), ((), ())), preferred_element_type=F32)


def _const_spec(shape):
    nd = len(shape)
    return pl.BlockSpec(shape, lambda *_: (0,) * nd, pipeline_mode=pl.Buffered(1))


def _pick_spec(full_shape, lead):
    tail = tuple(full_shape[len(lead):])
    idx = tuple(lead) + (0,) * len(tail)
    return pl.BlockSpec((None,) * len(lead) + tail, lambda *_: idx, pipeline_mode=pl.Buffered(1))


def _tok_spec(tm, n):
    return pl.BlockSpec((None, tm, n), lambda b, t: (b, t, 0))


def _split_bf16(x):
    hi = x.astype(BF16)
    return hi, (x - hi.astype(F32)).astype(BF16)


def _mod_kernel(c_ref, w_ref, b_ref, o_ref):
    a_hi, a_lo = _split_bf16(_silu(c_ref[...]))
    for g0 in range(0, w_ref.shape[1], MOD_COL_CHUNK):
        cols = slice(g0, g0 + MOD_COL_CHUNK)
        w_hi, w_lo = _split_bf16(w_ref[:, cols])
        o_ref[:, cols] = _dot(a_hi, w_hi) + _dot(a_hi, w_lo) + _dot(a_lo, w_hi) + b_ref[:, cols]


def _modulation(c_rows, mod_w, mod_b, tn=2304):
    depth, d, n = mod_w.shape
    rows = 16
    c_pad = jnp.zeros((rows, d), F32).at[:c_rows.shape[0]].set(c_rows)
    return pl.pallas_call(
        _mod_kernel,
        out_shape=jax.ShapeDtypeStruct((depth, rows, n), F32),
        grid=(depth, n // tn),
        in_specs=[
            _const_spec((rows, d)),
            pl.BlockSpec((None, d, tn), lambda l, j: (l, 0, j)),
            pl.BlockSpec((None, 1, tn), lambda l, j: (l, 0, j)),
        ],
        out_specs=pl.BlockSpec((None, rows, tn), lambda l, j: (l, 0, j)),
        compiler_params=_cparams(2),
        name="modulation",
    )(c_pad, mod_w, mod_b.reshape(depth, 1, n))


def _mod_index(layer, col, row_of_batch, b, t):
    return (layer, row_of_batch(b), col, 0, 0)


def _mod_block_specs(d_model, layer, sub, row_of_batch):
    return [pl.BlockSpec((None, None, None, 1, d_model),
                         functools.partial(_mod_index, layer, 3 * sub + k, row_of_batch))
            for k in range(3)]


def _ffn_kernel(x_ref, shift_ref, scale_ref, gate_ref, g_ref, w1_ref, w3_ref, w2_ref, fg_ref, *rest,
                n_cast, n_chunk, sub_rows, final_norm):
    cast_in, o_ref, cast_out, a_ref = rest[:n_cast], rest[n_cast], rest[n_cast + 1:-1], rest[-1]
    for src, dst in zip(cast_in, cast_out):
        dst[...] = src[...].astype(BF16)
    d_ff = w1_ref.shape[1]
    for r in range(x_ref.shape[0] // sub_rows):
        rows = slice(r * sub_rows, (r + 1) * sub_rows)
        x = x_ref[rows, :]
        h = _adaln(x, g_ref[...], shift_ref[...], scale_ref[...]).astype(BF16)
        for c in range(d_ff // n_chunk):
            sl = slice(c * n_chunk, (c + 1) * n_chunk)
            u = _dot(h, w1_ref[:, sl])
            v = _dot(h, w3_ref[:, sl])
            a_ref[rows, sl] = (_silu(u) * v).astype(BF16)
        y = _dot(a_ref[rows, :], w2_ref[...])
        out = x + (0.5 * gate_ref[...]) * y
        if final_norm:
            out = _rms(out, fg_ref[...])
        o_ref[rows, :] = out


def _ffn(x, mod5, layer, sub, row_of_batch, norm_g4, w1, w3, w2, final_g, *, cast=(), final_norm=False,
         tm=1024, n_chunk=256, sub_rows=256):
    bsz, length, d = x.shape
    d_ff = w1.shape[-1]
    tm = min(tm, length)
    n_t = length // tm
    n_steps = bsz * n_t
    bf16_rows = 16
    cast_in_specs, cast_out_specs, cast_shapes = [], [], []
    for arr, lead in cast:
        rows, cols = arr.shape[len(lead):]
        band = rows // n_steps
        assert band * n_steps == rows and band % bf16_rows == 0, (arr.shape, n_steps)
        cast_in_specs.append(pl.BlockSpec(
            (None,) * len(lead) + (band, cols),
            functools.partial(lambda lead, b, t: tuple(lead) + (b * n_t + t, 0), lead)))
        cast_out_specs.append(pl.BlockSpec((band, cols), lambda b, t: (b * n_t + t, 0)))
        cast_shapes.append(jax.ShapeDtypeStruct((rows, cols), BF16))
    kern = functools.partial(_ffn_kernel, n_cast=len(cast), n_chunk=n_chunk, sub_rows=min(sub_rows, tm),
                             final_norm=final_norm)
    out = pl.pallas_call(
        kern,
        out_shape=[jax.ShapeDtypeStruct(x.shape, F32)] + cast_shapes,
        grid=(bsz, n_t),
        in_specs=[_tok_spec(tm, d)]
        + _mod_block_specs(d, layer, sub, row_of_batch)
        + [_pick_spec(norm_g4.shape, (layer, sub)), _const_spec(w1.shape), _const_spec(w3.shape),
           _const_spec(w2.shape), _const_spec((1, d))]
        + cast_in_specs,
        out_specs=[_tok_spec(tm, d)] + cast_out_specs,
        scratch_shapes=[pltpu.VMEM((tm, d_ff), BF16)],
        compiler_params=_cparams(2),
        name="swiglu_ffn",
    )(x, mod5, mod5, mod5, norm_g4, w1, w3, w2, final_g.reshape(1, d), *[arr for arr, _ in cast])
    return out[0], list(out[1:])


def _hgrn_lower_bounds(raw, slot):
    ex = jnp.exp(raw - jnp.max(raw, axis=0, keepdims=True))
    sm = ex / jnp.sum(ex, axis=0, keepdims=True)
    lb = sm[0]
    for k in range(1, slot + 1):
        lb = lb + sm[k]
    return lb


def _abin_kernel(x_ref, shift_ref, scale_ref, g_ref, w_ref, lb_ref, q_ref, v_ref, lff_ref, kf_ref, lfb_ref,
                 kb_ref, sg_ref, p_ref, *, slot, sub_rows, n_chunk):
    dp, dh = p_ref.shape[-1], q_ref.shape[-1]
    lb_all = _hgrn_lower_bounds(lb_ref[...], slot)
    for r in range(x_ref.shape[0] // sub_rows):
        rows = slice(r * sub_rows, (r + 1) * sub_rows)
        h = _adaln(x_ref[rows, :], g_ref[...], shift_ref[...], scale_ref[...]).astype(BF16)

        p_ref[rows, :] = _dot(h, w_ref[:, 0:dp]).astype(BF16)
        for k in (2, 1, 3, 0, 4):
            for c0 in range(0, dh, n_chunk):
                cols = slice(c0, c0 + n_chunk)
                z = _dot(h, w_ref[:, dp + k * dh + c0:dp + k * dh + c0 + n_chunk])
                if k == 0:
                    q_ref[rows, cols] = (_silu(z) * HGRN_HEAD_DIM ** -0.5).astype(BF16)
                elif k == 1:
                    v_ref[rows, cols] = z.astype(BF16)
                elif k == 4:
                    sg_ref[rows, cols] = _silu(z).astype(BF16)
                else:
                    lf_ref, kk_ref = ((lff_ref, kf_ref), (lfb_ref, kb_ref))[k - 2]
                    lb = lb_all[k - 2:k - 1, cols]
                    forget = lb + (1.0 - lb) * _sigmoid(z)
                    lf_ref[rows, cols] = jnp.log(forget) * LOG2_E
                    kk_ref[rows, cols] = (1.0 - forget).astype(BF16)


def _ab_in(x, mod5, layer, row_of_batch, norm_g4, w_in, hgrn_lb, slot, d_pool, *, tm=1024, sub_rows=256,
           n_chunk=768):
    bsz, length, d = x.shape
    tm = min(tm, length)
    dh = (w_in.shape[-1] - d_pool) // 5
    shift_spec, scale_spec, _ = _mod_block_specs(d, layer, 1, row_of_batch)
    kern = functools.partial(_abin_kernel, slot=slot, sub_rows=min(sub_rows, tm), n_chunk=n_chunk)
    widths = (dh, dh, dh, dh, dh, dh, dh, d_pool)
    dtypes = (BF16, BF16, F32, BF16, F32, BF16, BF16, BF16)
    return pl.pallas_call(
        kern,
        out_shape=[jax.ShapeDtypeStruct((bsz, length, n), dt) for n, dt in zip(widths, dtypes)],
        grid=(bsz, length // tm),
        in_specs=[_tok_spec(tm, d), shift_spec, scale_spec, _pick_spec(norm_g4.shape, (layer, 1)),
                  _const_spec(w_in.shape), _const_spec(hgrn_lb.shape)],
        out_specs=[_tok_spec(tm, n) for n in widths],
        compiler_params=_cparams(2),
        name="ab_in_proj",
    )(x, mod5, mod5, norm_g4, w_in, hgrn_lb)


def _level_table(c, diag_block):
    j = np.arange(c)[:, None]
    i = np.arange(c)[None, :]
    x = j ^ i
    bits = np.zeros((c, c), np.int32)
    for k in range(int(np.log2(c))):
        bits = np.where(x >= (1 << k), k + 1, bits)
    lvl = np.where(i < j, bits, 0)
    if diag_block > 1:
        lvl = np.where((i <= j) & (bits <= int(np.log2(diag_block))), int(np.log2(diag_block)), lvl)
    return lvl.astype(np.int32)


def _fine_decay(forget, rev):
    c, w = forget.shape
    f3 = forget.reshape(c // 8, 8, w)
    pos = lax.broadcasted_iota(jnp.int32, (c // 8, 8, w), 1)

    def at(offset):
        return f3 if offset == 0 else pltpu.roll(f3, (-offset) % 8, 1)

    sgn = 1 if rev else -1
    cq = [at(0)]
    ck = [None]
    for t in range(1, 4):
        cq.append(cq[-1] * at(sgn * t))
        ck.append(at(-sgn * t) if ck[-1] is None else ck[-1] * at(-sgn * t))
    out = {}
    for s in (2, 4, 8):
        half = s // 2
        p = pos % s
        e = jnp.ones_like(f3)
        for t in range(half):
            q_pos = (half - 1 - t) if rev else (half + t)
            k_pos = (half + t) if rev else (half - 1 - t)
            e = jnp.where(p == q_pos, cq[t], e)
            if t > 0:
                e = jnp.where(p == k_pos, ck[t], e)
        out[s] = e.reshape(c, w)
    return out


def _block_row(b, s, ridx):
    c, w = b.shape
    r = b.reshape(c // s, s, w)[:, ridx:ridx + 1, :]
    return jnp.broadcast_to(r, (c // s, s, w)).reshape(c, w)


def _key_decay_column(b_end, shape):
    return jnp.broadcast_to(jnp.exp2(b_end), shape).T


def _hgrn_head(qq, kk, v, forget, b, st, lvl, rev):
    c = qq.shape[0]
    b_end = b[0:1] if rev else b[c - 1:c]
    vb = v.astype(BF16)
    o = _dot((qq * jnp.exp2(b)).astype(BF16), st.astype(BF16))
    k_out = (kk * jnp.exp2(b_end - b)).astype(BF16)
    st_new = st * _key_decay_column(b_end, st.shape) + _dot_tn(k_out, vb)

    row = lax.broadcasted_iota(jnp.int32, (c, 1), 0)
    fine = _fine_decay(forget, rev)
    a = jnp.zeros((c, c), F32)
    s = c
    while s > 1:
        half = s // 2
        q_side = ((row % s) >= half) != rev
        if s >= 16:
            e = jnp.exp2(-jnp.abs(b - _block_row(b, s, half if rev else half - 1)))
        else:
            e = fine[s]
        x = jnp.where(q_side, qq, kk) * e
        a = jnp.where(lvl == int(np.log2(s)), _dot(x.astype(BF16), x.T.astype(BF16)), a)
        s //= 2
    o = o + _dot(a.astype(BF16), vb) + jnp.sum(qq * kk, axis=-1, keepdims=True) * v.astype(F32)
    return o, st_new


def _hgrn_mild_direction(rev, qq_ref, kk_ref, b_ref, v_ref, o_ref, st_ref, lvl, ops_ref, a_ref):
    c = qq_ref.shape[0]
    n_heads = st_ref.shape[0]
    sizes = []
    s = c
    while s > HGRN_DIAG_BLOCK:
        sizes.append(s)
        s //= 2

    for h in range(n_heads):
        sl = slice(h * HGRN_HEAD_DIM, (h + 1) * HGRN_HEAD_DIM)
        qq, kk, b = qq_ref[:, sl], kk_ref[:, sl], b_ref[:, sl]
        b_end = b[0:1] if rev else b[c - 1:c]
        st = st_ref[h]
        ops_ref[h, 0] = qq * jnp.exp2(b).astype(BF16)
        ops_ref[h, 1] = kk * jnp.exp2(b_end - b).astype(BF16)
        ops_ref[h, 2] = st.astype(BF16)
        st_ref[h] = st * _key_decay_column(b_end, st.shape)
        for n, s in enumerate(sizes):
            half = s // 2
            r = _block_row(b, s, half if rev else half - 1)
            q_side = [((r0 % s) >= half) != rev for r0 in range(0, c, half)]
            blocks = [slice(r0, r0 + half) for r0 in range(0, c, half)]
            diff = jnp.concatenate([b[rs] - r[rs] if qs else r[rs] - b[rs] for rs, qs in zip(blocks, q_side)],
                                   axis=0)
            picks = jnp.concatenate([(qq if qs else kk)[rs] for rs, qs in zip(blocks, q_side)], axis=0)
            x = picks * jnp.exp2(diff).astype(BF16)
            ops_ref[h, 3 + 2 * n] = x
            ops_ref[h, 4 + 2 * n] = x.T
        d = b - _block_row(b, HGRN_DIAG_BLOCK, HGRN_DIAG_BLOCK - 1 if rev else 0)
        ops_ref[h, 3 + 2 * len(sizes)] = qq * jnp.exp2(d).astype(BF16)
        ops_ref[h, 4 + 2 * len(sizes)] = (kk * jnp.exp2(-d).astype(BF16)).T

    for h in range(n_heads):
        a = jnp.zeros((c, c), F32)
        for n, s in enumerate(sizes + [HGRN_DIAG_BLOCK]):
            a = jnp.where(lvl == int(np.log2(s)), _dot(ops_ref[h, 3 + 2 * n], ops_ref[h, 4 + 2 * n]), a)
        a_ref[h] = a.astype(BF16)

    for h in range(n_heads):
        sl = slice(h * HGRN_HEAD_DIM, (h + 1) * HGRN_HEAD_DIM)
        vb = v_ref[:, sl].astype(BF16)
        o = _dot(ops_ref[h, 0], ops_ref[h, 2]) + _dot(a_ref[h], vb)
        o_ref[:, sl] = o.astype(o_ref.dtype)
        st_ref[h] = st_ref[h] + _dot_tn(ops_ref[h, 1], vb)


def _hgrn_kernel(qf_ref, vf_ref, lf_ref, kf_ref, qb_ref, vb_ref, lb_ref, kb_ref, s0_ref, tri_ref, lvl_ref,
                 of_ref, ob_ref, sout_ref, s_scr, b_s, span_s, ops_s, a_s):
    t = pl.program_id(1)

    @pl.when(t == 0)
    def _():
        s_scr[...] = s0_ref[...]

    c = HGRN_CHUNK
    n_sub = qf_ref.shape[0] // c
    n_heads = qf_ref.shape[-1] // HGRN_HEAD_DIM

    for i in range(n_sub):
        weakest = None
        for d, (logf_ref, off) in enumerate(((lf_ref, i * c), (lb_ref, (n_sub - 1 - i) * c))):
            logf = logf_ref[off:off + c, :]
            hi = logf.astype(BF16)
            lo = (logf - hi.astype(F32)).astype(BF16)
            b = _dot(tri_ref[d], jnp.concatenate([hi, lo], axis=0))
            b_s[i, d] = b
            for m in range(c // HGRN_DIAG_BLOCK):
                lo_row, hi_row = m * HGRN_DIAG_BLOCK, (m + 1) * HGRN_DIAG_BLOCK - 1
                if d == 0:
                    span = b[hi_row:hi_row + 1] - b[lo_row:lo_row + 1]
                else:
                    span = b[lo_row:lo_row + 1] - b[hi_row:hi_row + 1]
                weakest = span if weakest is None else jnp.minimum(weakest, span)
        span_s[i] = jnp.min(weakest)

    def chunk(i, carry):
        offs = (pl.multiple_of(i * c, c), pl.multiple_of((n_sub - 1 - i) * c, c))
        dirs = tuple(tuple(ref.at[pl.ds(off, c), :] for ref in refs)
                     for off, refs in zip(offs, ((qf_ref, vf_ref, lf_ref, kf_ref, of_ref),
                                                 (qb_ref, vb_ref, lb_ref, kb_ref, ob_ref))))
        in_range = span_s[i] > -HGRN_SAFE_LOG2_SPAN

        @pl.when(in_range)
        def _():
            for d, (q_ref, v_ref, logf_ref, kk_ref, o_ref) in enumerate(dirs):
                _hgrn_mild_direction(d == 1, q_ref, kk_ref, b_s.at[i, d], v_ref, o_ref, s_scr.at[d],
                                     lvl_ref[d], ops_s.at[d], a_s.at[d])

        @pl.when(jnp.logical_not(in_range))
        def _():
            for d, (q_ref, v_ref, logf_ref, kk_ref, o_ref) in enumerate(dirs):
                lvl = lvl_ref[2 + d]
                for h in range(n_heads):
                    sl = slice(h * HGRN_HEAD_DIM, (h + 1) * HGRN_HEAD_DIM)
                    o_h, st_new = _hgrn_head(q_ref[:, sl].astype(F32), kk_ref[:, sl].astype(F32), v_ref[:, sl],
                                             jnp.exp2(logf_ref[:, sl]), b_s[i, d, :, sl], s_scr[d, h], lvl,
                                             d == 1)
                    o_ref[:, sl] = o_h.astype(o_ref.dtype)
                    s_scr[d, h] = st_new

        return carry

    lax.fori_loop(0, n_sub, chunk, 0)

    @pl.when(t == pl.num_programs(1) - 1)
    def _():
        sout_ref[...] = s_scr[...]


def _hgrn_scan(q, v, logf_fwd, kk_fwd, logf_bwd, kk_bwd, s0, *, rows=512):
    bsz, length, dh = q.shape
    c = HGRN_CHUNK
    rows = min(rows, length)
    n = length // rows
    n_heads = dh // HGRN_HEAD_DIM
    assert c == HGRN_HEAD_DIM and c > HGRN_DIAG_BLOCK
    n_mild = int(np.log2(c // HGRN_DIAG_BLOCK)) + 1
    lvl_fast = _level_table(c, HGRN_DIAG_BLOCK)
    lvl_full = _level_table(c, 1)
    lvl = jnp.asarray(np.stack([lvl_fast, lvl_fast.T, lvl_full, lvl_full.T]))
    tri_f = np.tril(np.ones((c, c), np.float32))
    tri = jnp.asarray(np.stack([np.tile(tri_f, (1, 2)), np.tile(tri_f.T, (1, 2))]), BF16)
    fwd = pl.BlockSpec((None, rows, dh), lambda b, t: (b, t, 0))
    bwd = pl.BlockSpec((None, rows, dh), lambda b, t: (b, n - 1 - t, 0))
    s_shape = (2, n_heads, HGRN_HEAD_DIM, HGRN_HEAD_DIM)
    s_spec = pl.BlockSpec((None,) + s_shape, lambda b, t: (b, 0, 0, 0, 0))
    return pl.pallas_call(
        _hgrn_kernel,
        out_shape=[jax.ShapeDtypeStruct(q.shape, BF16), jax.ShapeDtypeStruct(q.shape, BF16),
                   jax.ShapeDtypeStruct((bsz,) + s_shape, F32)],
        grid=(bsz, n),
        in_specs=[fwd, fwd, fwd, fwd, bwd, bwd, bwd, bwd, s_spec, _const_spec((2, c, 2 * c)),
                  _const_spec((4, c, c))],
        out_specs=[fwd, bwd, s_spec],
        scratch_shapes=[pltpu.VMEM(s_shape, F32), pltpu.VMEM((rows // c, 2, c, dh), F32),
                        pltpu.SMEM((rows // c,), F32),
                        pltpu.VMEM((2, n_heads, 3 + 2 * n_mild, c, HGRN_HEAD_DIM), BF16),
                        pltpu.VMEM((2, n_heads, c, c), BF16)],
        compiler_params=_cparams(2),
        name="hgrn2_scan",
    )(q, v, logf_fwd, kk_fwd, q, v, logf_bwd, kk_bwd, s0, tri, lvl)


def _pool_tables(tm, row_len):
    t = np.arange(tm)
    same_row = (t[:, None] // row_len) == (t[None, :] // row_len)
    pos = t % row_len
    bands, inv = [], []
    for w in POOL_WINDOWS:
        lo = np.clip(pos - w // 2, 0, row_len)
        hi = np.clip(pos - w // 2 + w, 0, row_len)
        u = pos[None, :]
        bands.append(same_row & (u >= lo[:, None]) & (u < hi[:, None]))
        inv.append(1.0 / (hi - lo).astype(np.float64))
    return np.stack(bands).astype(np.float32), np.stack(inv, axis=1).astype(np.float32)


def _about_kernel(x_ref, of_ref, ob_ref, sg_ref, a_ref, gate_ref, gn_ref, band_ref, icnt_ref, wp_ref,
                  ps_ref, wo_ref, o_ref):
    sub_rows = band_ref.shape[-1]
    dp = a_ref.shape[-1]
    n_groups = band_ref.shape[0]
    for r in range(x_ref.shape[0] // sub_rows):
        rows = slice(r * sub_rows, (r + 1) * sub_rows)
        o = of_ref[rows, :].astype(F32) + ob_ref[rows, :].astype(F32)
        heads = []
        for h in range(o.shape[-1] // HGRN_HEAD_DIM):
            oh = o[:, h * HGRN_HEAD_DIM:(h + 1) * HGRN_HEAD_DIM]
            heads.append(oh * lax.rsqrt(jnp.mean(oh * oh, axis=-1, keepdims=True) + EPS))
        b_mix = jnp.concatenate(heads, axis=-1) * gn_ref[...] * sg_ref[rows, :].astype(F32)

        xab = a_ref[rows, :].astype(BF16)
        xa = xab.astype(F32)
        lane_group = lax.broadcasted_iota(jnp.int32, xa.shape, 1) // (dp // n_groups)
        total = jnp.zeros_like(xa)
        for gi in range(n_groups):
            total = jnp.where(lane_group == gi, _dot(band_ref[gi], xab), total)
        y = total * icnt_ref[...] - xa
        a_mix = _dot(y.astype(BF16), wp_ref[...]) * ps_ref[...]

        mix = _dot(a_mix.astype(BF16), wo_ref[0:dp, :]) + _dot(b_mix.astype(BF16), wo_ref[dp:, :])
        o_ref[rows, :] = x_ref[rows, :] + gate_ref[...] * mix


def _ab_out(x, o_f, o_b, g, xa, mod5, layer, row_of_batch, g_norm, e, w_pool_bd, pool_scale, w_out, row_len,
            *, tm=1024, sub_rows=256):
    bsz, length, d = x.shape
    dh = o_f.shape[-1]
    dp = xa.shape[-1]
    tm = min(tm, length)
    sub_rows = min(sub_rows, tm)
    n_groups = len(POOL_WINDOWS)
    bands, inv = _pool_tables(sub_rows, row_len)
    icnt = jnp.asarray(np.repeat(inv, dp // n_groups, axis=1))
    _, _, gate_spec = _mod_block_specs(d, layer, 1, row_of_batch)
    return pl.pallas_call(
        _about_kernel,
        out_shape=jax.ShapeDtypeStruct(x.shape, F32),
        grid=(bsz, length // tm),
        in_specs=[_tok_spec(tm, d), _tok_spec(tm, dh), _tok_spec(tm, dh), _tok_spec(tm, dh), _tok_spec(tm, dp),
                  gate_spec, _pick_spec(g_norm.shape, (e,)), _const_spec((n_groups, sub_rows, sub_rows)),
                  _const_spec((sub_rows, dp)), _const_spec((dp, dp)), _pick_spec(pool_scale.shape, (e,)),
                  _const_spec(w_out.shape)],
        out_specs=_tok_spec(tm, d),
        compiler_params=_cparams(2),
        name="ab_out_proj",
    )(x, o_f, o_b, g, xa, mod5, g_norm, jnp.asarray(bands, BF16), icnt, w_pool_bd, pool_scale, w_out)


def _gelu_tanh(x):
    k1 = float(-2.0 * np.sqrt(2.0 / np.pi) * np.log2(np.e))
    k2 = 0.044715 * k1
    return x / (1.0 + jnp.exp2(x * (k1 + k2 * (x * x))))


def _gmlp_kernel(x_ref, shift_ref, scale_ref, gate_ref, g_ref, win_ref, lng_ref, lnb_ref, ws_ref, bs_ref,
                 wout_ref, o_ref, u_ref, v_ref, s_ref, *, sub_rows, n_chunk):
    dg = win_ref.shape[1] // 2
    n_sub = x_ref.shape[0] // sub_rows
    means = []
    for r in range(n_sub):
        rows = slice(r * sub_rows, (r + 1) * sub_rows)
        h = _adaln(x_ref[rows, :], g_ref[...], shift_ref[...], scale_ref[...]).astype(BF16)
        total = jnp.zeros((sub_rows, 1), F32)
        for c0 in range(0, dg, n_chunk):
            cols = slice(c0, c0 + n_chunk)
            u_ref[rows, cols] = _gelu_tanh(_dot(h, win_ref[:, c0:c0 + n_chunk]))
            zv = _gelu_tanh(_dot(h, win_ref[:, dg + c0:dg + c0 + n_chunk]))
            v_ref[rows, cols] = zv
            total = total + jnp.sum(zv, axis=-1, keepdims=True)
        means.append(total / dg)
    for r in range(n_sub):
        rows = slice(r * sub_rows, (r + 1) * sub_rows)
        mean = means[r]
        sq = jnp.zeros((sub_rows, 1), F32)
        for c0 in range(0, dg, n_chunk):
            vc = v_ref[rows, c0:c0 + n_chunk] - mean
            sq = sq + jnp.sum(vc * vc, axis=-1, keepdims=True)
        rstd = lax.rsqrt(sq / dg + EPS)
        for c in range(sub_rows // GMLP_CHUNK):
            crows = slice(c * GMLP_CHUNK, (c + 1) * GMLP_CHUNK)
            srows = slice(r * sub_rows + c * GMLP_CHUNK, r * sub_rows + (c + 1) * GMLP_CHUNK)
            for gi in range(dg // GMLP_GROUP_DIM):
                cols = slice(gi * GMLP_GROUP_DIM, (gi + 1) * GMLP_GROUP_DIM)
                vn = (v_ref[srows, cols] - mean[crows]) * rstd[crows] * lng_ref[:, cols] + lnb_ref[:, cols]
                sv = _dot(ws_ref[gi], vn.astype(BF16)) + bs_ref[gi]
                s_ref[srows, cols] = (u_ref[srows, cols] * sv).astype(BF16)
        o_ref[rows, :] = x_ref[rows, :] + gate_ref[...] * _dot(s_ref[rows, :], wout_ref[...])


def _gmlp(x, mod5, layer, row_of_batch, norm_g4, o, w_in, ln_g, ln_b, w_s, b_s_wide, w_out, *, tm=1024,
          sub_rows=512, n_chunk=512):
    bsz, length, d = x.shape
    dg = w_in.shape[-1] // 2
    tm = min(tm, length)
    kern = functools.partial(_gmlp_kernel, sub_rows=min(sub_rows, tm), n_chunk=n_chunk)
    return pl.pallas_call(
        kern,
        out_shape=jax.ShapeDtypeStruct(x.shape, F32),
        grid=(bsz, length // tm),
        in_specs=[_tok_spec(tm, d)]
        + _mod_block_specs(d, layer, 1, row_of_batch)
        + [_pick_spec(norm_g4.shape, (layer, 1)), _const_spec(w_in.shape), _pick_spec(ln_g.shape, (o,)),
           _pick_spec(ln_b.shape, (o,)), _const_spec(w_s.shape), _pick_spec(b_s_wide.shape, (o,)),
           _const_spec(w_out.shape)],
        out_specs=_tok_spec(tm, d),
        scratch_shapes=[pltpu.VMEM((tm, dg), F32), pltpu.VMEM((tm, dg), F32), pltpu.VMEM((tm, dg), BF16)],
        compiler_params=_cparams(2),
        name="gmlp_mixer",
    )(x, mod5, mod5, mod5, norm_g4, w_in, ln_g, ln_b, w_s, b_s_wide, w_out)


def _block_diag(w):
    n, g, a, b = w.shape
    eye = jnp.eye(g, dtype=w.dtype)
    return (w[:, :, :, None, :] * eye[None, :, None, :, None]).reshape(n, g * a, g * b)


def kernel(x, c, ctx, c_ctx, mod_w, mod_b, norm_g, ffn_w1, ffn_w3, ffn_w2, ab_w_in, pool_w, pool_scale,
           hgrn_lb, hgrn_norm_g, ab_w_out, gmlp_w_in, gmlp_ln_g, gmlp_ln_b, gmlp_w_s, gmlp_b_s, gmlp_w_out,
           final_g):
    bsz, _, d = x.shape
    depth = mod_w.shape[0]
    d_pool = pool_scale.shape[-1]
    d_hgrn = hgrn_norm_g.shape[-1]
    n_heads = d_hgrn // HGRN_HEAD_DIM

    c_rows = jnp.concatenate([c, c_ctx[None, :]], axis=0)
    mod = _modulation(c_rows, mod_w, mod_b)
    mod5 = mod.reshape(depth, mod.shape[1], N_MOD, 1, d)
    lat_row = lambda b: b
    ctx_row = lambda b: bsz

    norm_g4 = norm_g[:, :, None, :]
    pool_bd = _block_diag(pool_w).astype(BF16)
    pool_scale3, hgrn_norm_g3 = pool_scale[:, None, :], hgrn_norm_g[:, None, :]
    gm_ln_g, gm_ln_b = gmlp_ln_g[:, None, :], gmlp_ln_b[:, None, :]
    gm_b_wide = jnp.broadcast_to(gmlp_b_s[:, :, :, None], gmlp_b_s.shape + (GMLP_GROUP_DIM,))
    gmlp_w_s2 = gmlp_w_s.reshape(gmlp_w_s.shape[0], -1, gmlp_w_s.shape[-1])

    def ffn_sources(i, j):
        return [(ffn_w1, (i, j)), (ffn_w3, (i, j)), (ffn_w2, (i, j))]

    def mixer_sources(i):
        if i % 2 == 0:
            return [(ab_w_in, (i // 2,)), (ab_w_out, (i // 2,))]
        return [(gmlp_w_in, (i // 2,)), (gmlp_w_s2, (i // 2,)), (gmlp_w_out, (i // 2,))]

    ffn_order = [(i, j) for i in range(depth) for j in range(2)]
    ffn_w = {ffn_order[0]: [arr[lead].astype(BF16) for arr, lead in ffn_sources(*ffn_order[0])]}
    mixer_w = {}

    def latent_ffn(xl, i, j, **kw):
        k = ffn_order.index((i, j))
        cast = ffn_sources(*ffn_order[k + 1]) if k + 1 < len(ffn_order) else []
        n_next = len(cast)
        if j == 0:
            cast = cast + mixer_sources(i)
        xl, done = _ffn(xl, mod5, i, 2 * j, lat_row, norm_g4, *ffn_w[(i, j)], final_g, cast=cast, **kw)
        if n_next:
            ffn_w[ffn_order[k + 1]] = done[:n_next]
        if j == 0:
            mixer_w[i] = done[n_next:]
        return xl

    def context_ffn(xc, i, j):
        return _ffn(xc, mod5, i, 2 * j, ctx_row, norm_g4, *ffn_w[(i, j)], final_g)[0]

    xl, xc = x, ctx
    for i in range(depth):
        ctx_live = any(j % 2 == 0 for j in range(i, depth))
        last = i == depth - 1

        xl = latent_ffn(xl, i, 0)
        if ctx_live:
            xc = context_ffn(xc, i, 0)

        if i % 2 == 0:
            e = i // 2
            ab_in_w, ab_out_w = mixer_w[i]
            ab_in = functools.partial(_ab_in, mod5=mod5, layer=i, norm_g4=norm_g4, w_in=ab_in_w,
                                      hgrn_lb=hgrn_lb, slot=e, d_pool=d_pool)
            q_c, v_c, lf_c, kf_c, lb_c, kb_c, sg_c, a_c = ab_in(xc, row_of_batch=ctx_row)
            q_l, v_l, lf_l, kf_l, lb_l, kb_l, sg_l, a_l = ab_in(xl, row_of_batch=lat_row)
            s0 = jnp.zeros((bsz, 2, n_heads, HGRN_HEAD_DIM, HGRN_HEAD_DIM), F32)
            of_c, ob_c, s_ctx = _hgrn_scan(q_c, v_c, lf_c, kf_c, lb_c, kb_c, s0)
            of_l, ob_l, _ = _hgrn_scan(q_l, v_l, lf_l, kf_l, lb_l, kb_l, s_ctx)
            ab_out = functools.partial(_ab_out, mod5=mod5, layer=i, g_norm=hgrn_norm_g3, e=e,
                                       w_pool_bd=pool_bd[e], pool_scale=pool_scale3, w_out=ab_out_w)
            xc = ab_out(xc, of_c, ob_c, sg_c, a_c, row_of_batch=ctx_row, row_len=xc.shape[1])
            xl = ab_out(xl, of_l, ob_l, sg_l, a_l, row_of_batch=lat_row, row_len=GRID_W)
        else:
            o = i // 2
            gm_in, gm_s2, gm_out = mixer_w[i]
            gmlp = functools.partial(_gmlp, mod5=mod5, layer=i, norm_g4=norm_g4, o=o, w_in=gm_in, ln_g=gm_ln_g,
                                     ln_b=gm_ln_b, w_s=gm_s2.reshape(gmlp_w_s.shape[1:]), b_s_wide=gm_b_wide,
                                     w_out=gm_out)
            xl = gmlp(xl, row_of_batch=lat_row)
            if ctx_live:
                xc = gmlp(xc, row_of_batch=ctx_row)

        xl = latent_ffn(xl, i, 1, final_norm=last)
        if ctx_live:
            xc = context_ffn(xc, i, 1)
    return xl
```

```python
import functools

import numpy as np
import jax
import jax.numpy as jnp
from jax import lax
from jax.experimental import pallas as pl
from jax.experimental.pallas import tpu as pltpu

F32 = jnp.float32
BF16 = jnp.bfloat16

EPS = 1e-6
N_MOD = 9
GRID_W = 64
POOL_WINDOWS = (2, 4, 8, 16)
HGRN_HEAD_DIM = 128
GMLP_CHUNK = 128
GMLP_GROUP_DIM = 128

LOG2_E = float(np.log2(np.e))
HGRN_CHUNK = 128
HGRN_DIAG_BLOCK = 32
HGRN_SAFE_LOG2_SPAN = 100.0
MOD_COL_CHUNK = 768
VMEM_LIMIT_BYTES = 56 * 1024 * 1024


def _cparams(n_grid_dims):
    return pltpu.CompilerParams(
        dimension_semantics=("arbitrary",) * n_grid_dims,
        vmem_limit_bytes=VMEM_LIMIT_BYTES)


def _sigmoid(x):
    return 1.0 / (1.0 + jnp.exp2(x * (-LOG2_E)))


def _silu(x):
    return x * _sigmoid(x)


def _rms(x, g):
    return x * lax.rsqrt(jnp.mean(x * x, axis=-1, keepdims=True) + EPS) * g


def _adaln(x, g, shift, scale):
    return _rms(x, g) * (1.0 + scale) + shift


def _dot(a, b):
    return jnp.dot(a, b, preferred_element_type=F32)


def _dot_tn(a, b):
    return lax.dot_general(a, b, (((0,), (0,)), ((), ())), preferred_element_type=F32)


def _const_spec(shape):
    nd = len(shape)
    return pl.BlockSpec(shape, lambda *_: (0,) * nd, pipeline_mode=pl.Buffered(1))


def _pick_spec(full_shape, lead):
    tail = tuple(full_shape[len(lead):])
    idx = tuple(lead) + (0,) * len(tail)
    return pl.BlockSpec((None,) * len(lead) + tail, lambda *_: idx, pipeline_mode=pl.Buffered(1))


def _tok_spec(tm, n):
    return pl.BlockSpec((None, tm, n), lambda b, t: (b, t, 0))


def _split_bf16(x):
    hi = x.astype(BF16)
    return hi, (x - hi.astype(F32)).astype(BF16)


def _mod_kernel(c_ref, w_ref, b_ref, o_ref):
    a_hi, a_lo = _split_bf16(_silu(c_ref[...]))
    for g0 in range(0, w_ref.shape[1], MOD_COL_CHUNK):
        cols = slice(g0, g0 + MOD_COL_CHUNK)
        w_hi, w_lo = _split_bf16(w_ref[:, cols])
        o_ref[:, cols] = _dot(a_hi, w_hi) + _dot(a_hi, w_lo) + _dot(a_lo, w_hi) + b_ref[:, cols]


def _modulation(c_rows, mod_w, mod_b, tn=2304):
    depth, d, n = mod_w.shape
    rows = 16
    c_pad = jnp.zeros((rows, d), F32).at[:c_rows.shape[0]].set(c_rows)
    return pl.pallas_call(
        _mod_kernel,
        out_shape=jax.ShapeDtypeStruct((depth, rows, n), F32),
        grid=(depth, n // tn),
        in_specs=[
            _const_spec((rows, d)),
            pl.BlockSpec((None, d, tn), lambda l, j: (l, 0, j)),
            pl.BlockSpec((None, 1, tn), lambda l, j: (l, 0, j)),
        ],
        out_specs=pl.BlockSpec((None, rows, tn), lambda l, j: (l, 0, j)),
        compiler_params=_cparams(2),
        name="modulation",
    )(c_pad, mod_w, mod_b.reshape(depth, 1, n))


def _mod_index(layer, col, row_of_batch, b, t):
    return (layer, row_of_batch(b), col, 0, 0)


def _mod_block_specs(d_model, layer, sub, row_of_batch):
    return [pl.BlockSpec((None, None, None, 1, d_model),
                         functools.partial(_mod_index, layer, 3 * sub + k, row_of_batch))
            for k in range(3)]


def _ffn_kernel(x_ref, shift_ref, scale_ref, gate_ref, g_ref, w1_ref, w3_ref, w2_ref, fg_ref, *rest,
                n_cast, n_chunk, sub_rows, final_norm, has_delta):
    d_ref = rest[0] if has_delta else None
    rest = rest[1:] if has_delta else rest
    cast_in, o_ref, cast_out, a_ref = rest[:n_cast], rest[n_cast], rest[n_cast + 1:-1], rest[-1]
    for src, dst in zip(cast_in, cast_out):
        dst[...] = src[...].astype(BF16)
    d_ff = w1_ref.shape[1]
    for r in range(x_ref.shape[0] // sub_rows):
        rows = slice(r * sub_rows, (r + 1) * sub_rows)
        x = x_ref[rows, :]
        if has_delta:
            x = x + d_ref[rows, :].astype(F32)
        h = _adaln(x, g_ref[...], shift_ref[...], scale_ref[...]).astype(BF16)
        for c in range(d_ff // n_chunk):
            sl = slice(c * n_chunk, (c + 1) * n_chunk)
            u = _dot(h, w1_ref[:, sl])
            v = _dot(h, w3_ref[:, sl])
            a_ref[rows, sl] = (_silu(u) * v).astype(BF16)
        y = _dot(a_ref[rows, :], w2_ref[...])
        out = x + (0.5 * gate_ref[...]) * y
        if final_norm:
            out = _rms(out, fg_ref[...])
        o_ref[rows, :] = out


def _ffn(x, mod5, layer, sub, row_of_batch, norm_g4, w1, w3, w2, final_g, *, cast=(), delta=None,
         final_norm=False, tm=1024, n_chunk=256, sub_rows=256):
    bsz, length, d = x.shape
    d_ff = w1.shape[-1]
    tm = min(tm, length)
    n_t = length // tm
    n_steps = bsz * n_t
    bf16_rows = 16
    cast_in_specs, cast_out_specs, cast_shapes = [], [], []
    for arr, lead in cast:
        rows, cols = arr.shape[len(lead):]
        band = rows // n_steps
        assert band * n_steps == rows and band % bf16_rows == 0, (arr.shape, n_steps)
        cast_in_specs.append(pl.BlockSpec(
            (None,) * len(lead) + (band, cols),
            functools.partial(lambda lead, b, t: tuple(lead) + (b * n_t + t, 0), lead)))
        cast_out_specs.append(pl.BlockSpec((band, cols), lambda b, t: (b * n_t + t, 0)))
        cast_shapes.append(jax.ShapeDtypeStruct((rows, cols), BF16))
    has_delta = delta is not None
    kern = functools.partial(_ffn_kernel, n_cast=len(cast), n_chunk=n_chunk, sub_rows=min(sub_rows, tm),
                             final_norm=final_norm, has_delta=has_delta)
    out = pl.pallas_call(
        kern,
        out_shape=[jax.ShapeDtypeStruct(x.shape, F32)] + cast_shapes,
        grid=(bsz, n_t),
        in_specs=[_tok_spec(tm, d)]
        + _mod_block_specs(d, layer, sub, row_of_batch)
        + [_pick_spec(norm_g4.shape, (layer, sub)), _const_spec(w1.shape), _const_spec(w3.shape),
           _const_spec(w2.shape), _const_spec((1, d))]
        + ([_tok_spec(tm, d)] if has_delta else [])
        + cast_in_specs,
        out_specs=[_tok_spec(tm, d)] + cast_out_specs,
        scratch_shapes=[pltpu.VMEM((tm, d_ff), BF16)],
        compiler_params=_cparams(2),
        name="swiglu_ffn",
    )(x, mod5, mod5, mod5, norm_g4, w1, w3, w2, final_g.reshape(1, d), *([delta] if has_delta else []),
      *[arr for arr, _ in cast])
    return out[0], list(out[1:])


def _hgrn_lower_bounds(raw, slot):
    ex = jnp.exp(raw - jnp.max(raw, axis=0, keepdims=True))
    sm = ex / jnp.sum(ex, axis=0, keepdims=True)
    lb = sm[0]
    for k in range(1, slot + 1):
        lb = lb + sm[k]
    return lb


def _abin_kernel(x_ref, shift_ref, scale_ref, g_ref, w_ref, lb_ref, q_ref, v_ref, lff_ref, kf_ref, lfb_ref,
                 kb_ref, sg_ref, p_ref, *, slot, sub_rows, n_chunk):
    dp, dh = p_ref.shape[-1], q_ref.shape[-1]
    lb_all = _hgrn_lower_bounds(lb_ref[...], slot)
    for r in range(x_ref.shape[0] // sub_rows):
        rows = slice(r * sub_rows, (r + 1) * sub_rows)
        h = _adaln(x_ref[rows, :], g_ref[...], shift_ref[...], scale_ref[...]).astype(BF16)

        p_ref[rows, :] = _dot(h, w_ref[:, 0:dp]).astype(BF16)
        for k in (2, 1, 3, 0, 4):
            for c0 in range(0, dh, n_chunk):
                cols = slice(c0, c0 + n_chunk)
                z = _dot(h, w_ref[:, dp + k * dh + c0:dp + k * dh + c0 + n_chunk])
                if k == 0:
                    q_ref[rows, cols] = (_silu(z) * HGRN_HEAD_DIM ** -0.5).astype(BF16)
                elif k == 1:
                    v_ref[rows, cols] = z.astype(BF16)
                elif k == 4:
                    sg_ref[rows, cols] = _silu(z).astype(BF16)
                else:
                    lf_ref, kk_ref = ((lff_ref, kf_ref), (lfb_ref, kb_ref))[k - 2]
                    lb = lb_all[k - 2:k - 1, cols]
                    forget = lb + (1.0 - lb) * _sigmoid(z)
                    lf_ref[rows, cols] = jnp.log(forget) * LOG2_E
                    kk_ref[rows, cols] = (1.0 - forget).astype(BF16)


def _ab_in(x, mod5, layer, row_of_batch, norm_g4, w_in, hgrn_lb, slot, d_pool, *, tm=1024, sub_rows=256,
           n_chunk=768):
    bsz, length, d = x.shape
    tm = min(tm, length)
    dh = (w_in.shape[-1] - d_pool) // 5
    shift_spec, scale_spec, _ = _mod_block_specs(d, layer, 1, row_of_batch)
    kern = functools.partial(_abin_kernel, slot=slot, sub_rows=min(sub_rows, tm), n_chunk=n_chunk)
    widths = (dh, dh, dh, dh, dh, dh, dh, d_pool)
    dtypes = (BF16, BF16, F32, BF16, F32, BF16, BF16, BF16)
    return pl.pallas_call(
        kern,
        out_shape=[jax.ShapeDtypeStruct((bsz, length, n), dt) for n, dt in zip(widths, dtypes)],
        grid=(bsz, length // tm),
        in_specs=[_tok_spec(tm, d), shift_spec, scale_spec, _pick_spec(norm_g4.shape, (layer, 1)),
                  _const_spec(w_in.shape), _const_spec(hgrn_lb.shape)],
        out_specs=[_tok_spec(tm, n) for n in widths],
        compiler_params=_cparams(2),
        name="ab_in_proj",
    )(x, mod5, mod5, norm_g4, w_in, hgrn_lb)


def _level_table(c, diag_block):
    j = np.arange(c)[:, None]
    i = np.arange(c)[None, :]
    x = j ^ i
    bits = np.zeros((c, c), np.int32)
    for k in range(int(np.log2(c))):
        bits = np.where(x >= (1 << k), k + 1, bits)
    lvl = np.where(i < j, bits, 0)
    if diag_block > 1:
        lvl = np.where((i <= j) & (bits <= int(np.log2(diag_block))), int(np.log2(diag_block)), lvl)
    return lvl.astype(np.int32)


def _fine_decay(forget, rev):
    c, w = forget.shape
    f3 = forget.reshape(c // 8, 8, w)
    pos = lax.broadcasted_iota(jnp.int32, (c // 8, 8, w), 1)

    def at(offset):
        return f3 if offset == 0 else pltpu.roll(f3, (-offset) % 8, 1)

    sgn = 1 if rev else -1
    cq = [at(0)]
    ck = [None]
    for t in range(1, 4):
        cq.append(cq[-1] * at(sgn * t))
        ck.append(at(-sgn * t) if ck[-1] is None else ck[-1] * at(-sgn * t))
    out = {}
    for s in (2, 4, 8):
        half = s // 2
        p = pos % s
        e = jnp.ones_like(f3)
        for t in range(half):
            q_pos = (half - 1 - t) if rev else (half + t)
            k_pos = (half + t) if rev else (half - 1 - t)
            e = jnp.where(p == q_pos, cq[t], e)
            if t > 0:
                e = jnp.where(p == k_pos, ck[t], e)
        out[s] = e.reshape(c, w)
    return out


def _block_row(b, s, ridx):
    c, w = b.shape
    r = b.reshape(c // s, s, w)[:, ridx:ridx + 1, :]
    return jnp.broadcast_to(r, (c // s, s, w)).reshape(c, w)


def _key_decay_column(b_end, shape):
    return jnp.broadcast_to(jnp.exp2(b_end), shape).T


def _hgrn_head(qq, kk, v, forget, b, st, lvl, rev):
    c = qq.shape[0]
    b_end = b[0:1] if rev else b[c - 1:c]
    vb = v.astype(BF16)
    o = _dot((qq * jnp.exp2(b)).astype(BF16), st.astype(BF16))
    k_out = (kk * jnp.exp2(b_end - b)).astype(BF16)
    st_new = st * _key_decay_column(b_end, st.shape) + _dot_tn(k_out, vb)

    row = lax.broadcasted_iota(jnp.int32, (c, 1), 0)
    fine = _fine_decay(forget, rev)
    a = jnp.zeros((c, c), F32)
    s = c
    while s > 1:
        half = s // 2
        q_side = ((row % s) >= half) != rev
        if s >= 16:
            e = jnp.exp2(-jnp.abs(b - _block_row(b, s, half if rev else half - 1)))
        else:
            e = fine[s]
        x = jnp.where(q_side, qq, kk) * e
        a = jnp.where(lvl == int(np.log2(s)), _dot(x.astype(BF16), x.T.astype(BF16)), a)
        s //= 2
    o = o + _dot(a.astype(BF16), vb) + jnp.sum(qq * kk, axis=-1, keepdims=True) * v.astype(F32)
    return o, st_new


def _hgrn_mild_direction(rev, qq_ref, kk_ref, b_ref, v_ref, o_ref, st_ref, lvl, ops_ref, a_ref):
    c = qq_ref.shape[0]
    n_heads = st_ref.shape[0]
    sizes = []
    s = c
    while s > HGRN_DIAG_BLOCK:
        sizes.append(s)
        s //= 2

    for h in range(n_heads):
        sl = slice(h * HGRN_HEAD_DIM, (h + 1) * HGRN_HEAD_DIM)
        qq, kk, b = qq_ref[:, sl], kk_ref[:, sl], b_ref[:, sl]
        b_end = b[0:1] if rev else b[c - 1:c]
        st = st_ref[h]
        ops_ref[h, 0] = qq * jnp.exp2(b).astype(BF16)
        ops_ref[h, 1] = kk * jnp.exp2(b_end - b).astype(BF16)
        ops_ref[h, 2] = st.astype(BF16)
        st_ref[h] = st * _key_decay_column(b_end, st.shape)
        for n, s in enumerate(sizes):
            half = s // 2
            r = _block_row(b, s, half if rev else half - 1)
            q_side = [((r0 % s) >= half) != rev for r0 in range(0, c, half)]
            blocks = [slice(r0, r0 + half) for r0 in range(0, c, half)]
            diff = jnp.concatenate([b[rs] - r[rs] if qs else r[rs] - b[rs] for rs, qs in zip(blocks, q_side)],
                                   axis=0)
            picks = jnp.concatenate([(qq if qs else kk)[rs] for rs, qs in zip(blocks, q_side)], axis=0)
            x = picks * jnp.exp2(diff).astype(BF16)
            ops_ref[h, 3 + 2 * n] = x
            ops_ref[h, 4 + 2 * n] = x.T
        d = b - _block_row(b, HGRN_DIAG_BLOCK, HGRN_DIAG_BLOCK - 1 if rev else 0)
        ops_ref[h, 3 + 2 * len(sizes)] = qq * jnp.exp2(d).astype(BF16)
        ops_ref[h, 4 + 2 * len(sizes)] = (kk * jnp.exp2(-d).astype(BF16)).T

    for h in range(n_heads):
        a = jnp.zeros((c, c), F32)
        for n, s in enumerate(sizes + [HGRN_DIAG_BLOCK]):
            a = jnp.where(lvl == int(np.log2(s)), _dot(ops_ref[h, 3 + 2 * n], ops_ref[h, 4 + 2 * n]), a)
        a_ref[h] = a.astype(BF16)

    for h in range(n_heads):
        sl = slice(h * HGRN_HEAD_DIM, (h + 1) * HGRN_HEAD_DIM)
        vb = v_ref[:, sl].astype(BF16)
        o = _dot(ops_ref[h, 0], ops_ref[h, 2]) + _dot(a_ref[h], vb)
        o_ref[:, sl] = o.astype(o_ref.dtype)
        st_ref[h] = st_ref[h] + _dot_tn(ops_ref[h, 1], vb)


def _hgrn_kernel(qf_ref, vf_ref, lf_ref, kf_ref, qb_ref, vb_ref, lb_ref, kb_ref, s0_ref, tri_ref, lvl_ref,
                 of_ref, ob_ref, sout_ref, s_scr, b_s, span_s, ops_s, a_s):
    t = pl.program_id(1)

    @pl.when(t == 0)
    def _():
        s_scr[...] = s0_ref[...]

    c = HGRN_CHUNK
    n_sub = qf_ref.shape[0] // c
    n_heads = qf_ref.shape[-1] // HGRN_HEAD_DIM

    for i in range(n_sub):
        weakest = None
        for d, (logf_ref, off) in enumerate(((lf_ref, i * c), (lb_ref, (n_sub - 1 - i) * c))):
            logf = logf_ref[off:off + c, :]
            hi = logf.astype(BF16)
            lo = (logf - hi.astype(F32)).astype(BF16)
            b = _dot(tri_ref[d], jnp.concatenate([hi, lo], axis=0))
            b_s[i, d] = b
            for m in range(c // HGRN_DIAG_BLOCK):
                lo_row, hi_row = m * HGRN_DIAG_BLOCK, (m + 1) * HGRN_DIAG_BLOCK - 1
                if d == 0:
                    span = b[hi_row:hi_row + 1] - b[lo_row:lo_row + 1]
                else:
                    span = b[lo_row:lo_row + 1] - b[hi_row:hi_row + 1]
                weakest = span if weakest is None else jnp.minimum(weakest, span)
        span_s[i] = jnp.min(weakest)

    def chunk(i, carry):
        offs = (pl.multiple_of(i * c, c), pl.multiple_of((n_sub - 1 - i) * c, c))
        dirs = tuple(tuple(ref.at[pl.ds(off, c), :] for ref in refs)
                     for off, refs in zip(offs, ((qf_ref, vf_ref, lf_ref, kf_ref, of_ref),
                                                 (qb_ref, vb_ref, lb_ref, kb_ref, ob_ref))))
        in_range = span_s[i] > -HGRN_SAFE_LOG2_SPAN

        @pl.when(in_range)
        def _():
            for d, (q_ref, v_ref, logf_ref, kk_ref, o_ref) in enumerate(dirs):
                _hgrn_mild_direction(d == 1, q_ref, kk_ref, b_s.at[i, d], v_ref, o_ref, s_scr.at[d],
                                     lvl_ref[d], ops_s.at[d], a_s.at[d])

        @pl.when(jnp.logical_not(in_range))
        def _():
            for d, (q_ref, v_ref, logf_ref, kk_ref, o_ref) in enumerate(dirs):
                lvl = lvl_ref[2 + d]
                for h in range(n_heads):
                    sl = slice(h * HGRN_HEAD_DIM, (h + 1) * HGRN_HEAD_DIM)
                    o_h, st_new = _hgrn_head(q_ref[:, sl].astype(F32), kk_ref[:, sl].astype(F32), v_ref[:, sl],
                                             jnp.exp2(logf_ref[:, sl]), b_s[i, d, :, sl], s_scr[d, h], lvl,
                                             d == 1)
                    o_ref[:, sl] = o_h.astype(o_ref.dtype)
                    s_scr[d, h] = st_new

        return carry

    lax.fori_loop(0, n_sub, chunk, 0)

    @pl.when(t == pl.num_programs(1) - 1)
    def _():
        sout_ref[...] = s_scr[...]


def _hgrn_scan(q, v, logf_fwd, kk_fwd, logf_bwd, kk_bwd, s0, *, rows=512):
    bsz, length, dh = q.shape
    c = HGRN_CHUNK
    rows = min(rows, length)
    n = length // rows
    n_heads = dh // HGRN_HEAD_DIM
    assert c == HGRN_HEAD_DIM and c > HGRN_DIAG_BLOCK
    n_mild = int(np.log2(c // HGRN_DIAG_BLOCK)) + 1
    lvl_fast = _level_table(c, HGRN_DIAG_BLOCK)
    lvl_full = _level_table(c, 1)
    lvl = jnp.asarray(np.stack([lvl_fast, lvl_fast.T, lvl_full, lvl_full.T]))
    tri_f = np.tril(np.ones((c, c), np.float32))
    tri = jnp.asarray(np.stack([np.tile(tri_f, (1, 2)), np.tile(tri_f.T, (1, 2))]), BF16)
    fwd = pl.BlockSpec((None, rows, dh), lambda b, t: (b, t, 0))
    bwd = pl.BlockSpec((None, rows, dh), lambda b, t: (b, n - 1 - t, 0))
    s_shape = (2, n_heads, HGRN_HEAD_DIM, HGRN_HEAD_DIM)
    s_spec = pl.BlockSpec((None,) + s_shape, lambda b, t: (b, 0, 0, 0, 0))
    return pl.pallas_call(
        _hgrn_kernel,
        out_shape=[jax.ShapeDtypeStruct(q.shape, BF16), jax.ShapeDtypeStruct(q.shape, BF16),
                   jax.ShapeDtypeStruct((bsz,) + s_shape, F32)],
        grid=(bsz, n),
        in_specs=[fwd, fwd, fwd, fwd, bwd, bwd, bwd, bwd, s_spec, _const_spec((2, c, 2 * c)),
                  _const_spec((4, c, c))],
        out_specs=[fwd, bwd, s_spec],
        scratch_shapes=[pltpu.VMEM(s_shape, F32), pltpu.VMEM((rows // c, 2, c, dh), F32),
                        pltpu.SMEM((rows // c,), F32),
                        pltpu.VMEM((2, n_heads, 3 + 2 * n_mild, c, HGRN_HEAD_DIM), BF16),
                        pltpu.VMEM((2, n_heads, c, c), BF16)],
        compiler_params=_cparams(2),
        name="hgrn2_scan",
    )(q, v, logf_fwd, kk_fwd, q, v, logf_bwd, kk_bwd, s0, tri, lvl)


def _pool_tables(tm, row_len):
    t = np.arange(tm)
    same_row = (t[:, None] // row_len) == (t[None, :] // row_len)
    pos = t % row_len
    bands, inv = [], []
    for w in POOL_WINDOWS:
        lo = np.clip(pos - w // 2, 0, row_len)
        hi = np.clip(pos - w // 2 + w, 0, row_len)
        u = pos[None, :]
        bands.append(same_row & (u >= lo[:, None]) & (u < hi[:, None]))
        inv.append(1.0 / (hi - lo).astype(np.float64))
    return np.stack(bands).astype(np.float32), np.stack(inv, axis=1).astype(np.float32)


def _about_kernel(of_ref, ob_ref, sg_ref, a_ref, gate_ref, gn_ref, band_ref, icnt_ref, wp_ref,
                  ps_ref, wo_ref, o_ref, mix_ref):
    sub_rows = band_ref.shape[-1]
    dp = a_ref.shape[-1]
    n_groups = band_ref.shape[0]
    n_sub = of_ref.shape[0] // sub_rows
    for r in range(n_sub):
        rows = slice(r * sub_rows, (r + 1) * sub_rows)
        o = of_ref[rows, :].astype(F32) + ob_ref[rows, :].astype(F32)
        heads = []
        for h in range(o.shape[-1] // HGRN_HEAD_DIM):
            oh = o[:, h * HGRN_HEAD_DIM:(h + 1) * HGRN_HEAD_DIM]
            heads.append(oh * lax.rsqrt(jnp.mean(oh * oh, axis=-1, keepdims=True) + EPS))
        b_mix = jnp.concatenate(heads, axis=-1) * gn_ref[...] * sg_ref[rows, :].astype(F32)

        xab = a_ref[rows, :].astype(BF16)
        xa = xab.astype(F32)
        lane_group = lax.broadcasted_iota(jnp.int32, xa.shape, 1) // (dp // n_groups)
        total = jnp.zeros_like(xa)
        for gi in range(n_groups):
            total = jnp.where(lane_group == gi, _dot(band_ref[gi], xab), total)
        y = total * icnt_ref[...] - xa
        a_mix = _dot(y.astype(BF16), wp_ref[...]) * ps_ref[...]
        mix_ref[rows, 0:dp] = a_mix.astype(BF16)
        mix_ref[rows, dp:] = b_mix.astype(BF16)
    for r in range(n_sub):
        rows = slice(r * sub_rows, (r + 1) * sub_rows)
        o_ref[rows, :] = (gate_ref[...] * _dot(mix_ref[rows, :], wo_ref[...])).astype(o_ref.dtype)


def _ab_out(o_f, o_b, g, xa, mod5, layer, row_of_batch, g_norm, e, w_pool_bd, pool_scale, w_out, row_len,
            *, tm=1024, sub_rows=256):
    bsz, length, dh = o_f.shape
    d = w_out.shape[-1]
    dp = xa.shape[-1]
    tm = min(tm, length)
    sub_rows = min(sub_rows, tm)
    n_groups = len(POOL_WINDOWS)
    bands, inv = _pool_tables(sub_rows, row_len)
    icnt = jnp.asarray(np.repeat(inv, dp // n_groups, axis=1))
    _, _, gate_spec = _mod_block_specs(d, layer, 1, row_of_batch)
    return pl.pallas_call(
        _about_kernel,
        out_shape=jax.ShapeDtypeStruct((bsz, length, d), BF16),
        grid=(bsz, length // tm),
        in_specs=[_tok_spec(tm, dh), _tok_spec(tm, dh), _tok_spec(tm, dh), _tok_spec(tm, dp),
                  gate_spec, _pick_spec(g_norm.shape, (e,)), _const_spec((n_groups, sub_rows, sub_rows)),
                  _const_spec((sub_rows, dp)), _const_spec((dp, dp)), _pick_spec(pool_scale.shape, (e,)),
                  _const_spec(w_out.shape)],
        out_specs=_tok_spec(tm, d),
        scratch_shapes=[pltpu.VMEM((tm, dp + dh), BF16)],
        compiler_params=_cparams(2),
        name="ab_out_proj",
    )(o_f, o_b, g, xa, mod5, g_norm, jnp.asarray(bands, BF16), icnt, w_pool_bd, pool_scale, w_out)


def _gelu_tanh(x):
    k1 = float(-2.0 * np.sqrt(2.0 / np.pi) * np.log2(np.e))
    k2 = 0.044715 * k1
    return x / (1.0 + jnp.exp2(x * (k1 + k2 * (x * x))))


def _gmlp_kernel(x_ref, shift_ref, scale_ref, gate_ref, g_ref, win_ref, lng_ref, lnb_ref, ws_ref, bs_ref,
                 wout_ref, o_ref, u_ref, v_ref, s_ref, *, sub_rows, n_chunk):
    dg = win_ref.shape[1] // 2
    n_sub = x_ref.shape[0] // sub_rows
    means = []
    for r in range(n_sub):
        rows = slice(r * sub_rows, (r + 1) * sub_rows)
        h = _adaln(x_ref[rows, :], g_ref[...], shift_ref[...], scale_ref[...]).astype(BF16)
        total = jnp.zeros((sub_rows, 1), F32)
        for c0 in range(0, dg, n_chunk):
            cols = slice(c0, c0 + n_chunk)
            u_ref[rows, cols] = _gelu_tanh(_dot(h, win_ref[:, c0:c0 + n_chunk]))
            zv = _gelu_tanh(_dot(h, win_ref[:, dg + c0:dg + c0 + n_chunk]))
            v_ref[rows, cols] = zv
            total = total + jnp.sum(zv, axis=-1, keepdims=True)
        means.append(total / dg)
    for r in range(n_sub):
        rows = slice(r * sub_rows, (r + 1) * sub_rows)
        mean = means[r]
        sq = jnp.zeros((sub_rows, 1), F32)
        for c0 in range(0, dg, n_chunk):
            vc = v_ref[rows, c0:c0 + n_chunk] - mean
            sq = sq + jnp.sum(vc * vc, axis=-1, keepdims=True)
        rstd = lax.rsqrt(sq / dg + EPS)
        for c in range(sub_rows // GMLP_CHUNK):
            crows = slice(c * GMLP_CHUNK, (c + 1) * GMLP_CHUNK)
            srows = slice(r * sub_rows + c * GMLP_CHUNK, r * sub_rows + (c + 1) * GMLP_CHUNK)
            for gi in range(dg // GMLP_GROUP_DIM):
                cols = slice(gi * GMLP_GROUP_DIM, (gi + 1) * GMLP_GROUP_DIM)
                vn = (v_ref[srows, cols] - mean[crows]) * rstd[crows] * lng_ref[:, cols] + lnb_ref[:, cols]
                sv = _dot(ws_ref[gi], vn.astype(BF16)) + bs_ref[gi]
                s_ref[srows, cols] = (u_ref[srows, cols] * sv).astype(BF16)
        o_ref[rows, :] = x_ref[rows, :] + gate_ref[...] * _dot(s_ref[rows, :], wout_ref[...])


def _gmlp(x, mod5, layer, row_of_batch, norm_g4, o, w_in, ln_g, ln_b, w_s, b_s_wide, w_out, *, tm=1024,
          sub_rows=512, n_chunk=512):
    bsz, length, d = x.shape
    dg = w_in.shape[-1] // 2
    tm = min(tm, length)
    kern = functools.partial(_gmlp_kernel, sub_rows=min(sub_rows, tm), n_chunk=n_chunk)
    return pl.pallas_call(
        kern,
        out_shape=jax.ShapeDtypeStruct(x.shape, F32),
        grid=(bsz, length // tm),
        in_specs=[_tok_spec(tm, d)]
        + _mod_block_specs(d, layer, 1, row_of_batch)
        + [_pick_spec(norm_g4.shape, (layer, 1)), _const_spec(w_in.shape), _pick_spec(ln_g.shape, (o,)),
           _pick_spec(ln_b.shape, (o,)), _const_spec(w_s.shape), _pick_spec(b_s_wide.shape, (o,)),
           _const_spec(w_out.shape)],
        out_specs=_tok_spec(tm, d),
        scratch_shapes=[pltpu.VMEM((tm, dg), F32), pltpu.VMEM((tm, dg), F32), pltpu.VMEM((tm, dg), BF16)],
        compiler_params=_cparams(2),
        name="gmlp_mixer",
    )(x, mod5, mod5, mod5, norm_g4, w_in, ln_g, ln_b, w_s, b_s_wide, w_out)


def _block_diag(w):
    n, g, a, b = w.shape
    eye = jnp.eye(g, dtype=w.dtype)
    return (w[:, :, :, None, :] * eye[None, :, None, :, None]).reshape(n, g * a, g * b)


def kernel(x, c, ctx, c_ctx, mod_w, mod_b, norm_g, ffn_w1, ffn_w3, ffn_w2, ab_w_in, pool_w, pool_scale,
           hgrn_lb, hgrn_norm_g, ab_w_out, gmlp_w_in, gmlp_ln_g, gmlp_ln_b, gmlp_w_s, gmlp_b_s, gmlp_w_out,
           final_g):
    bsz, _, d = x.shape
    depth = mod_w.shape[0]
    d_pool = pool_scale.shape[-1]
    d_hgrn = hgrn_norm_g.shape[-1]
    n_heads = d_hgrn // HGRN_HEAD_DIM

    c_rows = jnp.concatenate([c, c_ctx[None, :]], axis=0)
    mod = _modulation(c_rows, mod_w, mod_b)
    mod5 = mod.reshape(depth, mod.shape[1], N_MOD, 1, d)
    lat_row = lambda b: b
    ctx_row = lambda b: bsz

    norm_g4 = norm_g[:, :, None, :]
    pool_bd = _block_diag(pool_w).astype(BF16)
    pool_scale3, hgrn_norm_g3 = pool_scale[:, None, :], hgrn_norm_g[:, None, :]
    gm_ln_g, gm_ln_b = gmlp_ln_g[:, None, :], gmlp_ln_b[:, None, :]
    gm_b_wide = jnp.broadcast_to(gmlp_b_s[:, :, :, None], gmlp_b_s.shape + (GMLP_GROUP_DIM,))
    gmlp_w_s2 = gmlp_w_s.reshape(gmlp_w_s.shape[0], -1, gmlp_w_s.shape[-1])

    def ffn_sources(i, j):
        return [(ffn_w1, (i, j)), (ffn_w3, (i, j)), (ffn_w2, (i, j))]

    def mixer_sources(i):
        if i % 2 == 0:
            return [(ab_w_in, (i // 2,)), (ab_w_out, (i // 2,))]
        return [(gmlp_w_in, (i // 2,)), (gmlp_w_s2, (i // 2,)), (gmlp_w_out, (i // 2,))]

    ffn_order = [(i, j) for i in range(depth) for j in range(2)]
    ffn_w = {ffn_order[0]: [arr[lead].astype(BF16) for arr, lead in ffn_sources(*ffn_order[0])]}
    mixer_w = {}

    def latent_ffn(xl, i, j, **kw):
        k = ffn_order.index((i, j))
        cast = ffn_sources(*ffn_order[k + 1]) if k + 1 < len(ffn_order) else []
        n_next = len(cast)
        if j == 0:
            cast = cast + mixer_sources(i)
        xl, done = _ffn(xl, mod5, i, 2 * j, lat_row, norm_g4, *ffn_w[(i, j)], final_g, cast=cast, **kw)
        if n_next:
            ffn_w[ffn_order[k + 1]] = done[:n_next]
        if j == 0:
            mixer_w[i] = done[n_next:]
        return xl

    def context_ffn(xc, i, j, **kw):
        return _ffn(xc, mod5, i, 2 * j, ctx_row, norm_g4, *ffn_w[(i, j)], final_g, **kw)[0]

    xl, xc = x, ctx
    for i in range(depth):
        ctx_live = any(j % 2 == 0 for j in range(i, depth))
        last = i == depth - 1

        xl = latent_ffn(xl, i, 0)
        if ctx_live:
            xc = context_ffn(xc, i, 0)

        if i % 2 == 0:
            e = i // 2
            ab_in_w, ab_out_w = mixer_w[i]
            ab_in = functools.partial(_ab_in, mod5=mod5, layer=i, norm_g4=norm_g4, w_in=ab_in_w,
                                      hgrn_lb=hgrn_lb, slot=e, d_pool=d_pool)
            q_c, v_c, lf_c, kf_c, lb_c, kb_c, sg_c, a_c = ab_in(xc, row_of_batch=ctx_row)
            q_l, v_l, lf_l, kf_l, lb_l, kb_l, sg_l, a_l = ab_in(xl, row_of_batch=lat_row)
            s0 = jnp.zeros((bsz, 2, n_heads, HGRN_HEAD_DIM, HGRN_HEAD_DIM), F32)
            of_c, ob_c, s_ctx = _hgrn_scan(q_c, v_c, lf_c, kf_c, lb_c, kb_c, s0)
            of_l, ob_l, _ = _hgrn_scan(q_l, v_l, lf_l, kf_l, lb_l, kb_l, s_ctx)
            ab_out = functools.partial(_ab_out, mod5=mod5, layer=i, g_norm=hgrn_norm_g3, e=e,
                                       w_pool_bd=pool_bd[e], pool_scale=pool_scale3, w_out=ab_out_w)
            dc = ab_out(of_c, ob_c, sg_c, a_c, row_of_batch=ctx_row, row_len=xc.shape[1])
            dl = ab_out(of_l, ob_l, sg_l, a_l, row_of_batch=lat_row, row_len=GRID_W)
        else:
            dc = dl = None
            o = i // 2
            gm_in, gm_s2, gm_out = mixer_w[i]
            gmlp = functools.partial(_gmlp, mod5=mod5, layer=i, norm_g4=norm_g4, o=o, w_in=gm_in, ln_g=gm_ln_g,
                                     ln_b=gm_ln_b, w_s=gm_s2.reshape(gmlp_w_s.shape[1:]), b_s_wide=gm_b_wide,
                                     w_out=gm_out)
            xl = gmlp(xl, row_of_batch=lat_row)
            if ctx_live:
                xc = gmlp(xc, row_of_batch=ctx_row)

        xl = latent_ffn(xl, i, 1, final_norm=last, delta=dl)
        if ctx_live:
            xc = context_ffn(xc, i, 1, delta=dc)
    return xl
```

```python
import functools

import numpy as np
import jax
import jax.numpy as jnp
from jax import lax
from jax.experimental import pallas as pl
from jax.experimental.pallas import tpu as pltpu

F32 = jnp.float32
BF16 = jnp.bfloat16

EPS = 1e-6
N_MOD = 9
GRID_W = 64
POOL_WINDOWS = (2, 4, 8, 16)
HGRN_HEAD_DIM = 128
GMLP_CHUNK = 128
GMLP_GROUP_DIM = 128

LOG2_E = float(np.log2(np.e))
HGRN_CHUNK = 128
HGRN_DIAG_BLOCK = 32
HGRN_SAFE_LOG2_SPAN = 100.0
MOD_COL_CHUNK = 768
VMEM_LIMIT_BYTES = 56 * 1024 * 1024
HGRN_VMEM_LIMIT_BYTES = 58 * 1024 * 1024


def _cparams(n_grid_dims, vmem_limit_bytes=VMEM_LIMIT_BYTES):
    return pltpu.CompilerParams(
        dimension_semantics=("arbitrary",) * n_grid_dims,
        vmem_limit_bytes=vmem_limit_bytes)


def _sigmoid(x):
    return 1.0 / (1.0 + jnp.exp2(x * (-LOG2_E)))


def _silu(x):
    return x * _sigmoid(x)


def _rms(x, g):
    return x * lax.rsqrt(jnp.mean(x * x, axis=-1, keepdims=True) + EPS) * g


def _adaln(x, g, shift, scale):
    return _rms(x, g) * (1.0 + scale) + shift


def _dot(a, b):
    return jnp.dot(a, b, preferred_element_type=F32)


def _dot_tn(a, b):
    return lax.dot_general(a, b, (((0,), (0,)), ((), ())), preferred_element_type=F32)


def _const_spec(shape):
    nd = len(shape)
    return pl.BlockSpec(shape, lambda *_: (0,) * nd, pipeline_mode=pl.Buffered(1))


def _pick_spec(full_shape, lead):
    tail = tuple(full_shape[len(lead):])
    idx = tuple(lead) + (0,) * len(tail)
    return pl.BlockSpec((None,) * len(lead) + tail, lambda *_: idx, pipeline_mode=pl.Buffered(1))


def _tok_spec(tm, n):
    return pl.BlockSpec((None, tm, n), lambda b, t: (b, t, 0))


def _split_bf16(x):
    hi = x.astype(BF16)
    return hi, (x - hi.astype(F32)).astype(BF16)


def _mod_kernel(c_ref, w_ref, b_ref, o_ref):
    a_hi, a_lo = _split_bf16(_silu(c_ref[...]))
    for g0 in range(0, w_ref.shape[1], MOD_COL_CHUNK):
        cols = slice(g0, g0 + MOD_COL_CHUNK)
        w_hi, w_lo = _split_bf16(w_ref[:, cols])
        o_ref[:, cols] = _dot(a_hi, w_hi) + _dot(a_hi, w_lo) + _dot(a_lo, w_hi) + b_ref[:, cols]


def _modulation(c_rows, mod_w, mod_b, tn=2304):
    depth, d, n = mod_w.shape
    rows = 16
    c_pad = jnp.zeros((rows, d), F32).at[:c_rows.shape[0]].set(c_rows)
    return pl.pallas_call(
        _mod_kernel,
        out_shape=jax.ShapeDtypeStruct((depth, rows, n), F32),
        grid=(depth, n // tn),
        in_specs=[
            _const_spec((rows, d)),
            pl.BlockSpec((None, d, tn), lambda l, j: (l, 0, j)),
            pl.BlockSpec((None, 1, tn), lambda l, j: (l, 0, j)),
        ],
        out_specs=pl.BlockSpec((None, rows, tn), lambda l, j: (l, 0, j)),
        compiler_params=_cparams(2),
        name="modulation",
    )(c_pad, mod_w, mod_b.reshape(depth, 1, n))


def _mod_index(layer, col, row_of_batch, b, t):
    return (layer, row_of_batch(b), col, 0, 0)


def _mod_block_specs(d_model, layer, sub, row_of_batch):
    return [pl.BlockSpec((None, None, None, 1, d_model),
                         functools.partial(_mod_index, layer, 3 * sub + k, row_of_batch))
            for k in range(3)]


def _ffn_kernel(x_ref, shift_ref, scale_ref, gate_ref, g_ref, w1_ref, w3_ref, w2_ref, fg_ref, *rest,
                n_cast, n_chunk, sub_rows, final_norm, has_delta):
    d_ref = rest[0] if has_delta else None
    rest = rest[1:] if has_delta else rest
    cast_in, o_ref, cast_out, a_ref = rest[:n_cast], rest[n_cast], rest[n_cast + 1:-1], rest[-1]
    for src, dst in zip(cast_in, cast_out):
        dst[...] = src[...].astype(BF16)
    d_ff = w1_ref.shape[1]
    for r in range(x_ref.shape[0] // sub_rows):
        rows = slice(r * sub_rows, (r + 1) * sub_rows)
        x = x_ref[rows, :]
        if has_delta:
            x = x + d_ref[rows, :].astype(F32)
        h = _adaln(x, g_ref[...], shift_ref[...], scale_ref[...]).astype(BF16)
        for c in range(d_ff // n_chunk):
            sl = slice(c * n_chunk, (c + 1) * n_chunk)
            u = _dot(h, w1_ref[:, sl])
            v = _dot(h, w3_ref[:, sl])
            a_ref[rows, sl] = (_silu(u) * v).astype(BF16)
        y = _dot(a_ref[rows, :], w2_ref[...])
        out = x + (0.5 * gate_ref[...]) * y
        if final_norm:
            out = _rms(out, fg_ref[...])
        o_ref[rows, :] = out


def _ffn(x, mod5, layer, sub, row_of_batch, norm_g4, w1, w3, w2, final_g, *, cast=(), delta=None,
         final_norm=False, tm=1024, n_chunk=256, sub_rows=256):
    bsz, length, d = x.shape
    d_ff = w1.shape[-1]
    tm = min(tm, length)
    n_t = length // tm
    n_steps = bsz * n_t
    bf16_rows = 16
    cast_in_specs, cast_out_specs, cast_shapes = [], [], []
    for arr, lead in cast:
        rows, cols = arr.shape[len(lead):]
        band = rows // n_steps
        assert band * n_steps == rows and band % bf16_rows == 0, (arr.shape, n_steps)
        cast_in_specs.append(pl.BlockSpec(
            (None,) * len(lead) + (band, cols),
            functools.partial(lambda lead, b, t: tuple(lead) + (b * n_t + t, 0), lead)))
        cast_out_specs.append(pl.BlockSpec((band, cols), lambda b, t: (b * n_t + t, 0)))
        cast_shapes.append(jax.ShapeDtypeStruct((rows, cols), BF16))
    has_delta = delta is not None
    kern = functools.partial(_ffn_kernel, n_cast=len(cast), n_chunk=n_chunk, sub_rows=min(sub_rows, tm),
                             final_norm=final_norm, has_delta=has_delta)
    out = pl.pallas_call(
        kern,
        out_shape=[jax.ShapeDtypeStruct(x.shape, F32)] + cast_shapes,
        grid=(bsz, n_t),
        in_specs=[_tok_spec(tm, d)]
        + _mod_block_specs(d, layer, sub, row_of_batch)
        + [_pick_spec(norm_g4.shape, (layer, sub)), _const_spec(w1.shape), _const_spec(w3.shape),
           _const_spec(w2.shape), _const_spec((1, d))]
        + ([_tok_spec(tm, d)] if has_delta else [])
        + cast_in_specs,
        out_specs=[_tok_spec(tm, d)] + cast_out_specs,
        scratch_shapes=[pltpu.VMEM((tm, d_ff), BF16)],
        compiler_params=_cparams(2),
        name="swiglu_ffn",
    )(x, mod5, mod5, mod5, norm_g4, w1, w3, w2, final_g.reshape(1, d), *([delta] if has_delta else []),
      *[arr for arr, _ in cast])
    return out[0], list(out[1:])


def _hgrn_lower_bounds(raw, slot):
    ex = jnp.exp(raw - jnp.max(raw, axis=0, keepdims=True))
    sm = ex / jnp.sum(ex, axis=0, keepdims=True)
    lb = sm[0]
    for k in range(1, slot + 1):
        lb = lb + sm[k]
    return lb


def _abin_kernel(x_ref, shift_ref, scale_ref, g_ref, w_ref, lb_ref, q_ref, v_ref, lff_ref, kf_ref, lfb_ref,
                 kb_ref, sg_ref, p_ref, *, slot, sub_rows, n_chunk):
    dp, dh = p_ref.shape[-1], q_ref.shape[-1]
    lb_all = _hgrn_lower_bounds(lb_ref[...], slot)
    for r in range(x_ref.shape[0] // sub_rows):
        rows = slice(r * sub_rows, (r + 1) * sub_rows)
        h = _adaln(x_ref[rows, :], g_ref[...], shift_ref[...], scale_ref[...]).astype(BF16)

        p_ref[rows, :] = _dot(h, w_ref[:, 0:dp]).astype(BF16)
        for k in (2, 1, 3, 0, 4):
            for c0 in range(0, dh, n_chunk):
                cols = slice(c0, c0 + n_chunk)
                z = _dot(h, w_ref[:, dp + k * dh + c0:dp + k * dh + c0 + n_chunk])
                if k == 0:
                    q_ref[rows, cols] = (_silu(z) * HGRN_HEAD_DIM ** -0.5).astype(BF16)
                elif k == 1:
                    v_ref[rows, cols] = z.astype(BF16)
                elif k == 4:
                    sg_ref[rows, cols] = _silu(z).astype(BF16)
                else:
                    lf_ref, kk_ref = ((lff_ref, kf_ref), (lfb_ref, kb_ref))[k - 2]
                    lb = lb_all[k - 2:k - 1, cols]
                    forget = lb + (1.0 - lb) * _sigmoid(z)
                    lf_ref[rows, cols] = jnp.log(forget) * LOG2_E
                    kk_ref[rows, cols] = (1.0 - forget).astype(BF16)


def _ab_in(x, mod5, layer, row_of_batch, norm_g4, w_in, hgrn_lb, slot, d_pool, *, tm=1024, sub_rows=256,
           n_chunk=768):
    bsz, length, d = x.shape
    tm = min(tm, length)
    dh = (w_in.shape[-1] - d_pool) // 5
    shift_spec, scale_spec, _ = _mod_block_specs(d, layer, 1, row_of_batch)
    kern = functools.partial(_abin_kernel, slot=slot, sub_rows=min(sub_rows, tm), n_chunk=n_chunk)
    widths = (dh, dh, dh, dh, dh, dh, dh, d_pool)
    dtypes = (BF16, BF16, F32, BF16, F32, BF16, BF16, BF16)
    return pl.pallas_call(
        kern,
        out_shape=[jax.ShapeDtypeStruct((bsz, length, n), dt) for n, dt in zip(widths, dtypes)],
        grid=(bsz, length // tm),
        in_specs=[_tok_spec(tm, d), shift_spec, scale_spec, _pick_spec(norm_g4.shape, (layer, 1)),
                  _const_spec(w_in.shape), _const_spec(hgrn_lb.shape)],
        out_specs=[_tok_spec(tm, n) for n in widths],
        compiler_params=_cparams(2),
        name="ab_in_proj",
    )(x, mod5, mod5, norm_g4, w_in, hgrn_lb)


def _level_table(c, diag_block):
    j = np.arange(c)[:, None]
    i = np.arange(c)[None, :]
    x = j ^ i
    bits = np.zeros((c, c), np.int32)
    for k in range(int(np.log2(c))):
        bits = np.where(x >= (1 << k), k + 1, bits)
    lvl = np.where(i < j, bits, 0)
    if diag_block > 1:
        lvl = np.where((i <= j) & (bits <= int(np.log2(diag_block))), int(np.log2(diag_block)), lvl)
    return lvl.astype(np.int32)


def _fine_decay(forget, rev):
    c, w = forget.shape
    f3 = forget.reshape(c // 8, 8, w)
    pos = lax.broadcasted_iota(jnp.int32, (c // 8, 8, w), 1)

    def at(offset):
        return f3 if offset == 0 else pltpu.roll(f3, (-offset) % 8, 1)

    sgn = 1 if rev else -1
    cq = [at(0)]
    ck = [None]
    for t in range(1, 4):
        cq.append(cq[-1] * at(sgn * t))
        ck.append(at(-sgn * t) if ck[-1] is None else ck[-1] * at(-sgn * t))
    out = {}
    for s in (2, 4, 8):
        half = s // 2
        p = pos % s
        e = jnp.ones_like(f3)
        for t in range(half):
            q_pos = (half - 1 - t) if rev else (half + t)
            k_pos = (half + t) if rev else (half - 1 - t)
            e = jnp.where(p == q_pos, cq[t], e)
            if t > 0:
                e = jnp.where(p == k_pos, ck[t], e)
        out[s] = e.reshape(c, w)
    return out


def _block_row(b, s, ridx):
    c, w = b.shape
    r = b.reshape(c // s, s, w)[:, ridx:ridx + 1, :]
    return jnp.broadcast_to(r, (c // s, s, w)).reshape(c, w)


def _key_decay_column(b_end, shape):
    return jnp.broadcast_to(jnp.exp2(b_end), shape).T


def _hgrn_head(qq, kk, v, forget, b, st, lvl, rev):
    c = qq.shape[0]
    b_end = b[0:1] if rev else b[c - 1:c]
    vb = v.astype(BF16)
    o = _dot((qq * jnp.exp2(b)).astype(BF16), st.astype(BF16))
    k_out = (kk * jnp.exp2(b_end - b)).astype(BF16)
    st_new = st * _key_decay_column(b_end, st.shape) + _dot_tn(k_out, vb)

    row = lax.broadcasted_iota(jnp.int32, (c, 1), 0)
    fine = _fine_decay(forget, rev)
    a = jnp.zeros((c, c), F32)
    s = c
    while s > 1:
        half = s // 2
        q_side = ((row % s) >= half) != rev
        if s >= 16:
            e = jnp.exp2(-jnp.abs(b - _block_row(b, s, half if rev else half - 1)))
        else:
            e = fine[s]
        x = jnp.where(q_side, qq, kk) * e
        a = jnp.where(lvl == int(np.log2(s)), _dot(x.astype(BF16), x.T.astype(BF16)), a)
        s //= 2
    o = o + _dot(a.astype(BF16), vb) + jnp.sum(qq * kk, axis=-1, keepdims=True) * v.astype(F32)
    return o, st_new


def _hgrn_mild_direction(rev, qq_ref, kk_ref, b_ref, v_ref, o_ref, st_ref, lvl, ops_ref, a_ref):
    c = qq_ref.shape[0]
    n_heads = st_ref.shape[0]
    sizes = []
    s = c
    while s > HGRN_DIAG_BLOCK:
        sizes.append(s)
        s //= 2

    for h in range(n_heads):
        sl = slice(h * HGRN_HEAD_DIM, (h + 1) * HGRN_HEAD_DIM)
        qq, kk, b = qq_ref[:, sl], kk_ref[:, sl], b_ref[:, sl]
        b_end = b[0:1] if rev else b[c - 1:c]
        st = st_ref[h]
        ops_ref[h, 0] = qq * jnp.exp2(b).astype(BF16)
        ops_ref[h, 1] = kk * jnp.exp2(b_end - b).astype(BF16)
        ops_ref[h, 2] = st.astype(BF16)
        st_ref[h] = st * _key_decay_column(b_end, st.shape)
        for n, s in enumerate(sizes):
            half = s // 2
            r = _block_row(b, s, half if rev else half - 1)
            q_side = [((r0 % s) >= half) != rev for r0 in range(0, c, half)]
            blocks = [slice(r0, r0 + half) for r0 in range(0, c, half)]
            diff = jnp.concatenate([b[rs] - r[rs] if qs else r[rs] - b[rs] for rs, qs in zip(blocks, q_side)],
                                   axis=0)
            picks = jnp.concatenate([(qq if qs else kk)[rs] for rs, qs in zip(blocks, q_side)], axis=0)
            x = picks * jnp.exp2(diff).astype(BF16)
            ops_ref[h, 3 + 2 * n] = x
            ops_ref[h, 4 + 2 * n] = x.T
        d = b - _block_row(b, HGRN_DIAG_BLOCK, HGRN_DIAG_BLOCK - 1 if rev else 0)
        ops_ref[h, 3 + 2 * len(sizes)] = qq * jnp.exp2(d).astype(BF16)
        ops_ref[h, 4 + 2 * len(sizes)] = (kk * jnp.exp2(-d).astype(BF16)).T

    for h in range(n_heads):
        a = jnp.zeros((c, c), F32)
        for n, s in enumerate(sizes + [HGRN_DIAG_BLOCK]):
            a = jnp.where(lvl == int(np.log2(s)), _dot(ops_ref[h, 3 + 2 * n], ops_ref[h, 4 + 2 * n]), a)
        a_ref[h] = a.astype(BF16)

    for h in range(n_heads):
        sl = slice(h * HGRN_HEAD_DIM, (h + 1) * HGRN_HEAD_DIM)
        vb = v_ref[:, sl].astype(BF16)
        o = _dot(ops_ref[h, 0], ops_ref[h, 2]) + _dot(a_ref[h], vb)
        o_ref[:, sl] = o.astype(o_ref.dtype)
        st_ref[h] = st_ref[h] + _dot_tn(ops_ref[h, 1], vb)


def _hgrn_kernel(qf_ref, vf_ref, lf_ref, kf_ref, qb_ref, vb_ref, lb_ref, kb_ref, s0_ref, tri_ref, lvl_ref,
                 of_ref, ob_ref, sout_ref, s_scr, b_s, span_s, ops_s, a_s):
    t = pl.program_id(1)

    @pl.when(t == 0)
    def _():
        s_scr[...] = s0_ref[...]

    c = HGRN_CHUNK
    n_sub = qf_ref.shape[0] // c
    n_heads = qf_ref.shape[-1] // HGRN_HEAD_DIM

    for i in range(n_sub):
        weakest = None
        for d, (logf_ref, off) in enumerate(((lf_ref, i * c), (lb_ref, (n_sub - 1 - i) * c))):
            logf = logf_ref[off:off + c, :]
            hi = logf.astype(BF16)
            lo = (logf - hi.astype(F32)).astype(BF16)
            b = _dot(tri_ref[d], jnp.concatenate([hi, lo], axis=0))
            b_s[i, d] = b
            for m in range(c // HGRN_DIAG_BLOCK):
                lo_row, hi_row = m * HGRN_DIAG_BLOCK, (m + 1) * HGRN_DIAG_BLOCK - 1
                if d == 0:
                    span = b[hi_row:hi_row + 1] - b[lo_row:lo_row + 1]
                else:
                    span = b[lo_row:lo_row + 1] - b[hi_row:hi_row + 1]
                weakest = span if weakest is None else jnp.minimum(weakest, span)
        span_s[i] = jnp.min(weakest)

    def chunk(i, carry):
        offs = (pl.multiple_of(i * c, c), pl.multiple_of((n_sub - 1 - i) * c, c))
        dirs = tuple(tuple(ref.at[pl.ds(off, c), :] for ref in refs)
                     for off, refs in zip(offs, ((qf_ref, vf_ref, lf_ref, kf_ref, of_ref),
                                                 (qb_ref, vb_ref, lb_ref, kb_ref, ob_ref))))
        in_range = span_s[i] > -HGRN_SAFE_LOG2_SPAN

        @pl.when(in_range)
        def _():
            for d, (q_ref, v_ref, logf_ref, kk_ref, o_ref) in enumerate(dirs):
                _hgrn_mild_direction(d == 1, q_ref, kk_ref, b_s.at[i, d], v_ref, o_ref, s_scr.at[d],
                                     lvl_ref[d], ops_s.at[d], a_s.at[d])

        @pl.when(jnp.logical_not(in_range))
        def _():
            for d, (q_ref, v_ref, logf_ref, kk_ref, o_ref) in enumerate(dirs):
                lvl = lvl_ref[2 + d]
                for h in range(n_heads):
                    sl = slice(h * HGRN_HEAD_DIM, (h + 1) * HGRN_HEAD_DIM)
                    o_h, st_new = _hgrn_head(q_ref[:, sl].astype(F32), kk_ref[:, sl].astype(F32), v_ref[:, sl],
                                             jnp.exp2(logf_ref[:, sl]), b_s[i, d, :, sl], s_scr[d, h], lvl,
                                             d == 1)
                    o_ref[:, sl] = o_h.astype(o_ref.dtype)
                    s_scr[d, h] = st_new

        return carry

    lax.fori_loop(0, n_sub, chunk, 0)

    @pl.when(t == pl.num_programs(1) - 1)
    def _():
        sout_ref[...] = s_scr[...]


def _hgrn_scan(q, v, logf_fwd, kk_fwd, logf_bwd, kk_bwd, s0, *, rows=1024):
    bsz, length, dh = q.shape
    c = HGRN_CHUNK
    rows = min(rows, length)
    n = length // rows
    n_heads = dh // HGRN_HEAD_DIM
    assert c == HGRN_HEAD_DIM and c > HGRN_DIAG_BLOCK
    n_mild = int(np.log2(c // HGRN_DIAG_BLOCK)) + 1
    lvl_fast = _level_table(c, HGRN_DIAG_BLOCK)
    lvl_full = _level_table(c, 1)
    lvl = jnp.asarray(np.stack([lvl_fast, lvl_fast.T, lvl_full, lvl_full.T]))
    tri_f = np.tril(np.ones((c, c), np.float32))
    tri = jnp.asarray(np.stack([np.tile(tri_f, (1, 2)), np.tile(tri_f.T, (1, 2))]), BF16)
    fwd = pl.BlockSpec((None, rows, dh), lambda b, t: (b, t, 0))
    bwd = pl.BlockSpec((None, rows, dh), lambda b, t: (b, n - 1 - t, 0))
    s_shape = (2, n_heads, HGRN_HEAD_DIM, HGRN_HEAD_DIM)
    s_spec = pl.BlockSpec((None,) + s_shape, lambda b, t: (b, 0, 0, 0, 0))
    return pl.pallas_call(
        _hgrn_kernel,
        out_shape=[jax.ShapeDtypeStruct(q.shape, BF16), jax.ShapeDtypeStruct(q.shape, BF16),
                   jax.ShapeDtypeStruct((bsz,) + s_shape, F32)],
        grid=(bsz, n),
        in_specs=[fwd, fwd, fwd, fwd, bwd, bwd, bwd, bwd, s_spec, _const_spec((2, c, 2 * c)),
                  _const_spec((4, c, c))],
        out_specs=[fwd, bwd, s_spec],
        scratch_shapes=[pltpu.VMEM(s_shape, F32), pltpu.VMEM((rows // c, 2, c, dh), F32),
                        pltpu.SMEM((rows // c,), F32),
                        pltpu.VMEM((2, n_heads, 3 + 2 * n_mild, c, HGRN_HEAD_DIM), BF16),
                        pltpu.VMEM((2, n_heads, c, c), BF16)],
        compiler_params=_cparams(2, HGRN_VMEM_LIMIT_BYTES),
        name="hgrn2_scan",
    )(q, v, logf_fwd, kk_fwd, q, v, logf_bwd, kk_bwd, s0, tri, lvl)


def _pool_tables(tm, row_len):
    t = np.arange(tm)
    same_row = (t[:, None] // row_len) == (t[None, :] // row_len)
    pos = t % row_len
    bands, inv = [], []
    for w in POOL_WINDOWS:
        lo = np.clip(pos - w // 2, 0, row_len)
        hi = np.clip(pos - w // 2 + w, 0, row_len)
        u = pos[None, :]
        bands.append(same_row & (u >= lo[:, None]) & (u < hi[:, None]))
        inv.append(1.0 / (hi - lo).astype(np.float64))
    return np.stack(bands).astype(np.float32), np.stack(inv, axis=1).astype(np.float32)


def _about_kernel(of_ref, ob_ref, sg_ref, a_ref, gate_ref, gn_ref, band_ref, icnt_ref, wp_ref,
                  ps_ref, wo_ref, o_ref, mix_ref):
    sub_rows = band_ref.shape[-1]
    dp = a_ref.shape[-1]
    n_groups = band_ref.shape[0]
    n_sub = of_ref.shape[0] // sub_rows
    for r in range(n_sub):
        rows = slice(r * sub_rows, (r + 1) * sub_rows)
        o = of_ref[rows, :].astype(F32) + ob_ref[rows, :].astype(F32)
        heads = []
        for h in range(o.shape[-1] // HGRN_HEAD_DIM):
            oh = o[:, h * HGRN_HEAD_DIM:(h + 1) * HGRN_HEAD_DIM]
            heads.append(oh * lax.rsqrt(jnp.mean(oh * oh, axis=-1, keepdims=True) + EPS))
        b_mix = jnp.concatenate(heads, axis=-1) * gn_ref[...] * sg_ref[rows, :].astype(F32)

        xab = a_ref[rows, :].astype(BF16)
        xa = xab.astype(F32)
        lane_group = lax.broadcasted_iota(jnp.int32, xa.shape, 1) // (dp // n_groups)
        total = jnp.zeros_like(xa)
        for gi in range(n_groups):
            total = jnp.where(lane_group == gi, _dot(band_ref[gi], xab), total)
        y = total * icnt_ref[...] - xa
        a_mix = _dot(y.astype(BF16), wp_ref[...]) * ps_ref[...]
        mix_ref[rows, 0:dp] = a_mix.astype(BF16)
        mix_ref[rows, dp:] = b_mix.astype(BF16)
    for r in range(n_sub):
        rows = slice(r * sub_rows, (r + 1) * sub_rows)
        o_ref[rows, :] = (gate_ref[...] * _dot(mix_ref[rows, :], wo_ref[...])).astype(o_ref.dtype)


def _ab_out(o_f, o_b, g, xa, mod5, layer, row_of_batch, g_norm, e, w_pool_bd, pool_scale, w_out, row_len,
            *, tm=1024, sub_rows=256):
    bsz, length, dh = o_f.shape
    d = w_out.shape[-1]
    dp = xa.shape[-1]
    tm = min(tm, length)
    sub_rows = min(sub_rows, tm)
    n_groups = len(POOL_WINDOWS)
    bands, inv = _pool_tables(sub_rows, row_len)
    icnt = jnp.asarray(np.repeat(inv, dp // n_groups, axis=1))
    _, _, gate_spec = _mod_block_specs(d, layer, 1, row_of_batch)
    return pl.pallas_call(
        _about_kernel,
        out_shape=jax.ShapeDtypeStruct((bsz, length, d), BF16),
        grid=(bsz, length // tm),
        in_specs=[_tok_spec(tm, dh), _tok_spec(tm, dh), _tok_spec(tm, dh), _tok_spec(tm, dp),
                  gate_spec, _pick_spec(g_norm.shape, (e,)), _const_spec((n_groups, sub_rows, sub_rows)),
                  _const_spec((sub_rows, dp)), _const_spec((dp, dp)), _pick_spec(pool_scale.shape, (e,)),
                  _const_spec(w_out.shape)],
        out_specs=_tok_spec(tm, d),
        scratch_shapes=[pltpu.VMEM((tm, dp + dh), BF16)],
        compiler_params=_cparams(2),
        name="ab_out_proj",
    )(o_f, o_b, g, xa, mod5, g_norm, jnp.asarray(bands, BF16), icnt, w_pool_bd, pool_scale, w_out)


def _gelu_tanh(x):
    k1 = float(-2.0 * np.sqrt(2.0 / np.pi) * np.log2(np.e))
    k2 = 0.044715 * k1
    return x / (1.0 + jnp.exp2(x * (k1 + k2 * (x * x))))


def _gmlp_kernel(x_ref, shift_ref, scale_ref, gate_ref, g_ref, win_ref, lng_ref, lnb_ref, ws_ref, bs_ref,
                 wout_ref, o_ref, u_ref, v_ref, s_ref, *, sub_rows, n_chunk):
    dg = win_ref.shape[1] // 2
    n_sub = x_ref.shape[0] // sub_rows
    means = []
    for r in range(n_sub):
        rows = slice(r * sub_rows, (r + 1) * sub_rows)
        h = _adaln(x_ref[rows, :], g_ref[...], shift_ref[...], scale_ref[...]).astype(BF16)
        total = jnp.zeros((sub_rows, 1), F32)
        for c0 in range(0, dg, n_chunk):
            cols = slice(c0, c0 + n_chunk)
            u_ref[rows, cols] = _gelu_tanh(_dot(h, win_ref[:, c0:c0 + n_chunk]))
            zv = _gelu_tanh(_dot(h, win_ref[:, dg + c0:dg + c0 + n_chunk]))
            v_ref[rows, cols] = zv
            total = total + jnp.sum(zv, axis=-1, keepdims=True)
        means.append(total / dg)
    for r in range(n_sub):
        rows = slice(r * sub_rows, (r + 1) * sub_rows)
        mean = means[r]
        sq = jnp.zeros((sub_rows, 1), F32)
        for c0 in range(0, dg, n_chunk):
            vc = v_ref[rows, c0:c0 + n_chunk] - mean
            sq = sq + jnp.sum(vc * vc, axis=-1, keepdims=True)
        rstd = lax.rsqrt(sq / dg + EPS)
        for c in range(sub_rows // GMLP_CHUNK):
            crows = slice(c * GMLP_CHUNK, (c + 1) * GMLP_CHUNK)
            srows = slice(r * sub_rows + c * GMLP_CHUNK, r * sub_rows + (c + 1) * GMLP_CHUNK)
            for gi in range(dg // GMLP_GROUP_DIM):
                cols = slice(gi * GMLP_GROUP_DIM, (gi + 1) * GMLP_GROUP_DIM)
                vn = (v_ref[srows, cols] - mean[crows]) * rstd[crows] * lng_ref[:, cols] + lnb_ref[:, cols]
                sv = _dot(ws_ref[gi], vn.astype(BF16)) + bs_ref[gi]
                s_ref[srows, cols] = (u_ref[srows, cols] * sv).astype(BF16)
        o_ref[rows, :] = x_ref[rows, :] + gate_ref[...] * _dot(s_ref[rows, :], wout_ref[...])


def _gmlp(x, mod5, layer, row_of_batch, norm_g4, o, w_in, ln_g, ln_b, w_s, b_s_wide, w_out, *, tm=1024,
          sub_rows=512, n_chunk=512):
    bsz, length, d = x.shape
    dg = w_in.shape[-1] // 2
    tm = min(tm, length)
    kern = functools.partial(_gmlp_kernel, sub_rows=min(sub_rows, tm), n_chunk=n_chunk)
    return pl.pallas_call(
        kern,
        out_shape=jax.ShapeDtypeStruct(x.shape, F32),
        grid=(bsz, length // tm),
        in_specs=[_tok_spec(tm, d)]
        + _mod_block_specs(d, layer, 1, row_of_batch)
        + [_pick_spec(norm_g4.shape, (layer, 1)), _const_spec(w_in.shape), _pick_spec(ln_g.shape, (o,)),
           _pick_spec(ln_b.shape, (o,)), _const_spec(w_s.shape), _pick_spec(b_s_wide.shape, (o,)),
           _const_spec(w_out.shape)],
        out_specs=_tok_spec(tm, d),
        scratch_shapes=[pltpu.VMEM((tm, dg), F32), pltpu.VMEM((tm, dg), F32), pltpu.VMEM((tm, dg), BF16)],
        compiler_params=_cparams(2),
        name="gmlp_mixer",
    )(x, mod5, mod5, mod5, norm_g4, w_in, ln_g, ln_b, w_s, b_s_wide, w_out)


def _block_diag(w):
    n, g, a, b = w.shape
    eye = jnp.eye(g, dtype=w.dtype)
    return (w[:, :, :, None, :] * eye[None, :, None, :, None]).reshape(n, g * a, g * b)


def kernel(x, c, ctx, c_ctx, mod_w, mod_b, norm_g, ffn_w1, ffn_w3, ffn_w2, ab_w_in, pool_w, pool_scale,
           hgrn_lb, hgrn_norm_g, ab_w_out, gmlp_w_in, gmlp_ln_g, gmlp_ln_b, gmlp_w_s, gmlp_b_s, gmlp_w_out,
           final_g):
    bsz, _, d = x.shape
    depth = mod_w.shape[0]
    d_pool = pool_scale.shape[-1]
    d_hgrn = hgrn_norm_g.shape[-1]
    n_heads = d_hgrn // HGRN_HEAD_DIM

    c_rows = jnp.concatenate([c, c_ctx[None, :]], axis=0)
    mod = _modulation(c_rows, mod_w, mod_b)
    mod5 = mod.reshape(depth, mod.shape[1], N_MOD, 1, d)
    lat_row = lambda b: b
    ctx_row = lambda b: bsz

    norm_g4 = norm_g[:, :, None, :]
    pool_bd = _block_diag(pool_w).astype(BF16)
    pool_scale3, hgrn_norm_g3 = pool_scale[:, None, :], hgrn_norm_g[:, None, :]
    gm_ln_g, gm_ln_b = gmlp_ln_g[:, None, :], gmlp_ln_b[:, None, :]
    gm_b_wide = jnp.broadcast_to(gmlp_b_s[:, :, :, None], gmlp_b_s.shape + (GMLP_GROUP_DIM,))
    gmlp_w_s2 = gmlp_w_s.reshape(gmlp_w_s.shape[0], -1, gmlp_w_s.shape[-1])

    def ffn_sources(i, j):
        return [(ffn_w1, (i, j)), (ffn_w3, (i, j)), (ffn_w2, (i, j))]

    def mixer_sources(i):
        if i % 2 == 0:
            return [(ab_w_in, (i // 2,)), (ab_w_out, (i // 2,))]
        return [(gmlp_w_in, (i // 2,)), (gmlp_w_s2, (i // 2,)), (gmlp_w_out, (i // 2,))]

    ffn_order = [(i, j) for i in range(depth) for j in range(2)]
    ffn_w = {ffn_order[0]: [arr[lead].astype(BF16) for arr, lead in ffn_sources(*ffn_order[0])]}
    mixer_w = {}

    def latent_ffn(xl, i, j, **kw):
        k = ffn_order.index((i, j))
        cast = ffn_sources(*ffn_order[k + 1]) if k + 1 < len(ffn_order) else []
        n_next = len(cast)
        if j == 0:
            cast = cast + mixer_sources(i)
        xl, done = _ffn(xl, mod5, i, 2 * j, lat_row, norm_g4, *ffn_w[(i, j)], final_g, cast=cast, **kw)
        if n_next:
            ffn_w[ffn_order[k + 1]] = done[:n_next]
        if j == 0:
            mixer_w[i] = done[n_next:]
        return xl

    def context_ffn(xc, i, j, **kw):
        return _ffn(xc, mod5, i, 2 * j, ctx_row, norm_g4, *ffn_w[(i, j)], final_g, **kw)[0]

    xl, xc = x, ctx
    for i in range(depth):
        ctx_live = any(j % 2 == 0 for j in range(i, depth))
        last = i == depth - 1

        xl = latent_ffn(xl, i, 0)
        if ctx_live:
            xc = context_ffn(xc, i, 0)

        if i % 2 == 0:
            e = i // 2
            ab_in_w, ab_out_w = mixer_w[i]
            ab_in = functools.partial(_ab_in, mod5=mod5, layer=i, norm_g4=norm_g4, w_in=ab_in_w,
                                      hgrn_lb=hgrn_lb, slot=e, d_pool=d_pool)
            q_c, v_c, lf_c, kf_c, lb_c, kb_c, sg_c, a_c = ab_in(xc, row_of_batch=ctx_row)
            q_l, v_l, lf_l, kf_l, lb_l, kb_l, sg_l, a_l = ab_in(xl, row_of_batch=lat_row)
            s0 = jnp.zeros((bsz, 2, n_heads, HGRN_HEAD_DIM, HGRN_HEAD_DIM), F32)
            of_c, ob_c, s_ctx = _hgrn_scan(q_c, v_c, lf_c, kf_c, lb_c, kb_c, s0)
            of_l, ob_l, _ = _hgrn_scan(q_l, v_l, lf_l, kf_l, lb_l, kb_l, s_ctx)
            ab_out = functools.partial(_ab_out, mod5=mod5, layer=i, g_norm=hgrn_norm_g3, e=e,
                                       w_pool_bd=pool_bd[e], pool_scale=pool_scale3, w_out=ab_out_w)
            dc = ab_out(of_c, ob_c, sg_c, a_c, row_of_batch=ctx_row, row_len=xc.shape[1])
            dl = ab_out(of_l, ob_l, sg_l, a_l, row_of_batch=lat_row, row_len=GRID_W)
        else:
            dc = dl = None
            o = i // 2
            gm_in, gm_s2, gm_out = mixer_w[i]
            gmlp = functools.partial(_gmlp, mod5=mod5, layer=i, norm_g4=norm_g4, o=o, w_in=gm_in, ln_g=gm_ln_g,
                                     ln_b=gm_ln_b, w_s=gm_s2.reshape(gmlp_w_s.shape[1:]), b_s_wide=gm_b_wide,
                                     w_out=gm_out)
            xl = gmlp(xl, row_of_batch=lat_row)
            if ctx_live:
                xc = gmlp(xc, row_of_batch=ctx_row)

        xl = latent_ffn(xl, i, 1, final_norm=last, delta=dl)
        if ctx_live:
            xc = context_ffn(xc, i, 1, delta=dc)
    return xl
```

```python
import functools

import numpy as np
import jax
import jax.numpy as jnp
from jax import lax
from jax.experimental import pallas as pl
from jax.experimental.pallas import tpu as pltpu

F32 = jnp.float32
BF16 = jnp.bfloat16

EPS = 1e-6
N_MOD = 9
GRID_W = 64
POOL_WINDOWS = (2, 4, 8, 16)
HGRN_HEAD_DIM = 128
GMLP_CHUNK = 128
GMLP_GROUP_DIM = 128

LOG2_E = float(np.log2(np.e))
HGRN_CHUNK = 128
HGRN_DIAG_BLOCK = 32
HGRN_SAFE_LOG2_SPAN = 100.0
MOD_COL_CHUNK = 768
VMEM_LIMIT_BYTES = 56 * 1024 * 1024


def _cparams(n_grid_dims):
    return pltpu.CompilerParams(
        dimension_semantics=("arbitrary",) * n_grid_dims,
        vmem_limit_bytes=VMEM_LIMIT_BYTES)


def _sigmoid(x):
    return 1.0 / (1.0 + jnp.exp2(x * (-LOG2_E)))


def _silu(x):
    return x * _sigmoid(x)


def _rms(x, g):
    return x * lax.rsqrt(jnp.mean(x * x, axis=-1, keepdims=True) + EPS) * g


def _adaln(x, g, shift, scale):
    return _rms(x, g) * (1.0 + scale) + shift


def _dot(a, b):
    return jnp.dot(a, b, preferred_element_type=F32)


def _dot_tn(a, b):
    return lax.dot_general(a, b, (((0,), (0,)), ((), ())), preferred_element_type=F32)


def _const_spec(shape):
    nd = len(shape)
    return pl.BlockSpec(shape, lambda *_: (0,) * nd, pipeline_mode=pl.Buffered(1))


def _pick_spec(full_shape, lead):
    tail = tuple(full_shape[len(lead):])
    idx = tuple(lead) + (0,) * len(tail)
    return pl.BlockSpec((None,) * len(lead) + tail, lambda *_: idx, pipeline_mode=pl.Buffered(1))


def _tok_spec(tm, n):
    return pl.BlockSpec((None, tm, n), lambda b, t: (b, t, 0))


def _split_bf16(x):
    hi = x.astype(BF16)
    return hi, (x - hi.astype(F32)).astype(BF16)


def _mod_kernel(c_ref, w_ref, b_ref, o_ref):
    a_hi, a_lo = _split_bf16(_silu(c_ref[...]))
    for g0 in range(0, w_ref.shape[1], MOD_COL_CHUNK):
        cols = slice(g0, g0 + MOD_COL_CHUNK)
        w_hi, w_lo = _split_bf16(w_ref[:, cols])
        o_ref[:, cols] = _dot(a_hi, w_hi) + _dot(a_hi, w_lo) + _dot(a_lo, w_hi) + b_ref[:, cols]


def _modulation(c_rows, mod_w, mod_b, tn=2304):
    depth, d, n = mod_w.shape
    rows = 16
    c_pad = jnp.zeros((rows, d), F32).at[:c_rows.shape[0]].set(c_rows)
    return pl.pallas_call(
        _mod_kernel,
        out_shape=jax.ShapeDtypeStruct((depth, rows, n), F32),
        grid=(depth, n // tn),
        in_specs=[
            _const_spec((rows, d)),
            pl.BlockSpec((None, d, tn), lambda l, j: (l, 0, j)),
            pl.BlockSpec((None, 1, tn), lambda l, j: (l, 0, j)),
        ],
        out_specs=pl.BlockSpec((None, rows, tn), lambda l, j: (l, 0, j)),
        compiler_params=_cparams(2),
        name="modulation",
    )(c_pad, mod_w, mod_b.reshape(depth, 1, n))


def _mod_index(layer, col, row_of_batch, b, t):
    return (layer, row_of_batch(b), col, 0, 0)


def _mod_block_specs(d_model, layer, sub, row_of_batch):
    return [pl.BlockSpec((None, None, None, 1, d_model),
                         functools.partial(_mod_index, layer, 3 * sub + k, row_of_batch))
            for k in range(3)]


def _ffn_kernel(x_ref, shift_ref, scale_ref, gate_ref, g_ref, w1_ref, w3_ref, w2_ref, fg_ref, *rest,
                n_cast, n_chunk, sub_rows, final_norm, has_delta):
    d_ref = rest[0] if has_delta else None
    rest = rest[1:] if has_delta else rest
    cast_in, o_ref, cast_out, a_ref = rest[:n_cast], rest[n_cast], rest[n_cast + 1:-1], rest[-1]
    for src, dst in zip(cast_in, cast_out):
        dst[...] = src[...].astype(BF16)
    d_ff = w1_ref.shape[1]
    for r in range(x_ref.shape[0] // sub_rows):
        rows = slice(r * sub_rows, (r + 1) * sub_rows)
        x = x_ref[rows, :]
        if has_delta:
            x = x + d_ref[rows, :].astype(F32)
        h = _adaln(x, g_ref[...], shift_ref[...], scale_ref[...]).astype(BF16)
        for c in range(d_ff // n_chunk):
            sl = slice(c * n_chunk, (c + 1) * n_chunk)
            u = _dot(h, w1_ref[:, sl])
            v = _dot(h, w3_ref[:, sl])
            a_ref[rows, sl] = (_silu(u) * v).astype(BF16)
        y = _dot(a_ref[rows, :], w2_ref[...])
        out = x + (0.5 * gate_ref[...]) * y
        if final_norm:
            out = _rms(out, fg_ref[...])
        o_ref[rows, :] = out


def _ffn(x, mod5, layer, sub, row_of_batch, norm_g4, w1, w3, w2, final_g, *, cast=(), delta=None,
         final_norm=False, tm=1024, n_chunk=256, sub_rows=256):
    bsz, length, d = x.shape
    d_ff = w1.shape[-1]
    tm = min(tm, length)
    n_t = length // tm
    n_steps = bsz * n_t
    bf16_rows = 16
    cast_in_specs, cast_out_specs, cast_shapes = [], [], []
    for arr, lead in cast:
        rows, cols = arr.shape[len(lead):]
        band = rows // n_steps
        assert band * n_steps == rows and band % bf16_rows == 0, (arr.shape, n_steps)
        cast_in_specs.append(pl.BlockSpec(
            (None,) * len(lead) + (band, cols),
            functools.partial(lambda lead, b, t: tuple(lead) + (b * n_t + t, 0), lead)))
        cast_out_specs.append(pl.BlockSpec((band, cols), lambda b, t: (b * n_t + t, 0)))
        cast_shapes.append(jax.ShapeDtypeStruct((rows, cols), BF16))
    has_delta = delta is not None
    kern = functools.partial(_ffn_kernel, n_cast=len(cast), n_chunk=n_chunk, sub_rows=min(sub_rows, tm),
                             final_norm=final_norm, has_delta=has_delta)
    out = pl.pallas_call(
        kern,
        out_shape=[jax.ShapeDtypeStruct(x.shape, F32)] + cast_shapes,
        grid=(bsz, n_t),
        in_specs=[_tok_spec(tm, d)]
        + _mod_block_specs(d, layer, sub, row_of_batch)
        + [_pick_spec(norm_g4.shape, (layer, sub)), _const_spec(w1.shape), _const_spec(w3.shape),
           _const_spec(w2.shape), _const_spec((1, d))]
        + ([_tok_spec(tm, d)] if has_delta else [])
        + cast_in_specs,
        out_specs=[_tok_spec(tm, d)] + cast_out_specs,
        scratch_shapes=[pltpu.VMEM((tm, d_ff), BF16)],
        compiler_params=_cparams(2),
        name="swiglu_ffn",
    )(x, mod5, mod5, mod5, norm_g4, w1, w3, w2, final_g.reshape(1, d), *([delta] if has_delta else []),
      *[arr for arr, _ in cast])
    return out[0], list(out[1:])


def _hgrn_lower_bounds(raw, slot):
    ex = jnp.exp(raw - jnp.max(raw, axis=0, keepdims=True))
    sm = ex / jnp.sum(ex, axis=0, keepdims=True)
    lb = sm[0]
    for k in range(1, slot + 1):
        lb = lb + sm[k]
    return lb


def _abin_kernel(x_ref, shift_ref, scale_ref, g_ref, w_ref, lb_ref, tri_ref, q_ref, v_ref, lff_ref, kf_ref,
                 lfb_ref, kb_ref, sg_ref, p_ref, *, slot, sub_rows, n_chunk):
    dp, dh = p_ref.shape[-1], q_ref.shape[-1]
    lb_all = _hgrn_lower_bounds(lb_ref[...], slot)
    for r in range(x_ref.shape[0] // sub_rows):
        rows = slice(r * sub_rows, (r + 1) * sub_rows)
        h = _adaln(x_ref[rows, :], g_ref[...], shift_ref[...], scale_ref[...]).astype(BF16)

        p_ref[rows, :] = _dot(h, w_ref[:, 0:dp]).astype(BF16)
        for k in (2, 1, 3, 0, 4):
            for c0 in range(0, dh, n_chunk):
                cols = slice(c0, c0 + n_chunk)
                z = _dot(h, w_ref[:, dp + k * dh + c0:dp + k * dh + c0 + n_chunk])
                if k == 0:
                    q_ref[rows, cols] = (_silu(z) * HGRN_HEAD_DIM ** -0.5).astype(BF16)
                elif k == 1:
                    v_ref[rows, cols] = z.astype(BF16)
                elif k == 4:
                    sg_ref[rows, cols] = _silu(z).astype(BF16)
                else:
                    lf_ref, kk_ref = ((lff_ref, kf_ref), (lfb_ref, kb_ref))[k - 2]
                    lb = lb_all[k - 2:k - 1, cols]
                    forget = lb + (1.0 - lb) * _sigmoid(z)
                    logf = jnp.log(forget) * LOG2_E
                    for r0 in range(0, sub_rows, HGRN_CHUNK):
                        hi, lo = _split_bf16(logf[r0:r0 + HGRN_CHUNK])
                        lf_ref[r * sub_rows + r0:r * sub_rows + r0 + HGRN_CHUNK, cols] = _dot(
                            tri_ref[k - 2], jnp.concatenate([hi, lo], axis=0))
                    kk_ref[rows, cols] = (1.0 - forget).astype(BF16)


def _ab_in(x, mod5, layer, row_of_batch, norm_g4, w_in, hgrn_lb, slot, d_pool, *, tm=1024, sub_rows=256,
           n_chunk=768):
    bsz, length, d = x.shape
    tm = min(tm, length)
    dh = (w_in.shape[-1] - d_pool) // 5
    shift_spec, scale_spec, _ = _mod_block_specs(d, layer, 1, row_of_batch)
    tri_f = np.tril(np.ones((HGRN_CHUNK, HGRN_CHUNK), np.float32))
    tri = jnp.asarray(np.stack([np.tile(tri_f, (1, 2)), np.tile(tri_f.T, (1, 2))]), BF16)
    kern = functools.partial(_abin_kernel, slot=slot, sub_rows=min(sub_rows, tm), n_chunk=n_chunk)
    widths = (dh, dh, dh, dh, dh, dh, dh, d_pool)
    dtypes = (BF16, BF16, F32, BF16, F32, BF16, BF16, BF16)
    return pl.pallas_call(
        kern,
        out_shape=[jax.ShapeDtypeStruct((bsz, length, n), dt) for n, dt in zip(widths, dtypes)],
        grid=(bsz, length // tm),
        in_specs=[_tok_spec(tm, d), shift_spec, scale_spec, _pick_spec(norm_g4.shape, (layer, 1)),
                  _const_spec(w_in.shape), _const_spec(hgrn_lb.shape), _const_spec(tri.shape)],
        out_specs=[_tok_spec(tm, n) for n in widths],
        compiler_params=_cparams(2),
        name="ab_in_proj",
    )(x, mod5, mod5, norm_g4, w_in, hgrn_lb, tri)


def _level_table(c, diag_block):
    j = np.arange(c)[:, None]
    i = np.arange(c)[None, :]
    x = j ^ i
    bits = np.zeros((c, c), np.int32)
    for k in range(int(np.log2(c))):
        bits = np.where(x >= (1 << k), k + 1, bits)
    lvl = np.where(i < j, bits, 0)
    if diag_block > 1:
        lvl = np.where((i <= j) & (bits <= int(np.log2(diag_block))), int(np.log2(diag_block)), lvl)
    return lvl.astype(np.int32)


def _fine_decay(forget, rev):
    c, w = forget.shape
    f3 = forget.reshape(c // 8, 8, w)
    pos = lax.broadcasted_iota(jnp.int32, (c // 8, 8, w), 1)

    def at(offset):
        return f3 if offset == 0 else pltpu.roll(f3, (-offset) % 8, 1)

    sgn = 1 if rev else -1
    cq = [at(0)]
    ck = [None]
    for t in range(1, 4):
        cq.append(cq[-1] * at(sgn * t))
        ck.append(at(-sgn * t) if ck[-1] is None else ck[-1] * at(-sgn * t))
    out = {}
    for s in (2, 4, 8):
        half = s // 2
        p = pos % s
        e = jnp.ones_like(f3)
        for t in range(half):
            q_pos = (half - 1 - t) if rev else (half + t)
            k_pos = (half + t) if rev else (half - 1 - t)
            e = jnp.where(p == q_pos, cq[t], e)
            if t > 0:
                e = jnp.where(p == k_pos, ck[t], e)
        out[s] = e.reshape(c, w)
    return out


def _block_row(b, s, ridx):
    c, w = b.shape
    r = b.reshape(c // s, s, w)[:, ridx:ridx + 1, :]
    return jnp.broadcast_to(r, (c // s, s, w)).reshape(c, w)


def _key_decay_column(b_end, shape):
    return jnp.broadcast_to(jnp.exp2(b_end), shape).T


def _hgrn_head(qq, kk, v, forget, b, st, lvl, rev):
    c = qq.shape[0]
    b_end = b[0:1] if rev else b[c - 1:c]
    vb = v.astype(BF16)
    o = _dot((qq * jnp.exp2(b)).astype(BF16), st.astype(BF16))
    k_out = (kk * jnp.exp2(b_end - b)).astype(BF16)
    st_new = st * _key_decay_column(b_end, st.shape) + _dot_tn(k_out, vb)

    row = lax.broadcasted_iota(jnp.int32, (c, 1), 0)
    fine = _fine_decay(forget, rev)
    a = jnp.zeros((c, c), F32)
    s = c
    while s > 1:
        half = s // 2
        q_side = ((row % s) >= half) != rev
        if s >= 16:
            e = jnp.exp2(-jnp.abs(b - _block_row(b, s, half if rev else half - 1)))
        else:
            e = fine[s]
        x = jnp.where(q_side, qq, kk) * e
        a = jnp.where(lvl == int(np.log2(s)), _dot(x.astype(BF16), x.T.astype(BF16)), a)
        s //= 2
    o = o + _dot(a.astype(BF16), vb) + jnp.sum(qq * kk, axis=-1, keepdims=True) * v.astype(F32)
    return o, st_new


def _hgrn_mild_direction(rev, qq_ref, kk_ref, b_ref, v_ref, o_ref, st_ref, lvl, ops_ref, a_ref):
    c = qq_ref.shape[0]
    n_heads = st_ref.shape[0]
    sizes = []
    s = c
    while s > HGRN_DIAG_BLOCK:
        sizes.append(s)
        s //= 2

    for h in range(n_heads):
        sl = slice(h * HGRN_HEAD_DIM, (h + 1) * HGRN_HEAD_DIM)
        qq, kk, b = qq_ref[:, sl], kk_ref[:, sl], b_ref[:, sl]
        b_end = b[0:1] if rev else b[c - 1:c]
        st = st_ref[h]
        ops_ref[h, 0] = qq * jnp.exp2(b).astype(BF16)
        ops_ref[h, 1] = kk * jnp.exp2(b_end - b).astype(BF16)
        ops_ref[h, 2] = st.astype(BF16)
        st_ref[h] = st * _key_decay_column(b_end, st.shape)
        for n, s in enumerate(sizes):
            half = s // 2
            r = _block_row(b, s, half if rev else half - 1)
            q_side = [((r0 % s) >= half) != rev for r0 in range(0, c, half)]
            blocks = [slice(r0, r0 + half) for r0 in range(0, c, half)]
            diff = jnp.concatenate([b[rs] - r[rs] if qs else r[rs] - b[rs] for rs, qs in zip(blocks, q_side)],
                                   axis=0)
            picks = jnp.concatenate([(qq if qs else kk)[rs] for rs, qs in zip(blocks, q_side)], axis=0)
            x = picks * jnp.exp2(diff).astype(BF16)
            ops_ref[h, 3 + 2 * n] = x
            ops_ref[h, 4 + 2 * n] = x.T
        d = b - _block_row(b, HGRN_DIAG_BLOCK, HGRN_DIAG_BLOCK - 1 if rev else 0)
        ops_ref[h, 3 + 2 * len(sizes)] = qq * jnp.exp2(d).astype(BF16)
        ops_ref[h, 4 + 2 * len(sizes)] = (kk * jnp.exp2(-d).astype(BF16)).T

    for h in range(n_heads):
        a = jnp.zeros((c, c), F32)
        for n, s in enumerate(sizes + [HGRN_DIAG_BLOCK]):
            a = jnp.where(lvl == int(np.log2(s)), _dot(ops_ref[h, 3 + 2 * n], ops_ref[h, 4 + 2 * n]), a)
        a_ref[h] = a.astype(BF16)

    for h in range(n_heads):
        sl = slice(h * HGRN_HEAD_DIM, (h + 1) * HGRN_HEAD_DIM)
        vb = v_ref[:, sl].astype(BF16)
        o = _dot(ops_ref[h, 0], ops_ref[h, 2]) + _dot(a_ref[h], vb)
        o_ref[:, sl] = o.astype(o_ref.dtype)
        st_ref[h] = st_ref[h] + _dot_tn(ops_ref[h, 1], vb)


def _hgrn_kernel(qf_ref, vf_ref, lf_ref, kf_ref, qb_ref, vb_ref, lb_ref, kb_ref, s0_ref, lvl_ref,
                 of_ref, ob_ref, sout_ref, s_scr, span_s, ops_s, a_s):
    t = pl.program_id(1)

    @pl.when(t == 0)
    def _():
        s_scr[...] = s0_ref[...]

    c = HGRN_CHUNK
    n_sub = qf_ref.shape[0] // c
    n_heads = qf_ref.shape[-1] // HGRN_HEAD_DIM

    for i in range(n_sub):
        weakest = None
        for d, (b_ref, off) in enumerate(((lf_ref, i * c), (lb_ref, (n_sub - 1 - i) * c))):
            for m in range(c // HGRN_DIAG_BLOCK):
                lo_row, hi_row = off + m * HGRN_DIAG_BLOCK, off + (m + 1) * HGRN_DIAG_BLOCK - 1
                b_lo, b_hi = b_ref[lo_row:lo_row + 1, :], b_ref[hi_row:hi_row + 1, :]
                span = b_hi - b_lo if d == 0 else b_lo - b_hi
                weakest = span if weakest is None else jnp.minimum(weakest, span)
        span_s[i] = jnp.min(weakest)

    def chunk(i, carry):
        offs = (pl.multiple_of(i * c, c), pl.multiple_of((n_sub - 1 - i) * c, c))
        dirs = tuple(tuple(ref.at[pl.ds(off, c), :] for ref in refs)
                     for off, refs in zip(offs, ((qf_ref, vf_ref, lf_ref, kf_ref, of_ref),
                                                 (qb_ref, vb_ref, lb_ref, kb_ref, ob_ref))))
        in_range = span_s[i] > -HGRN_SAFE_LOG2_SPAN

        @pl.when(in_range)
        def _():
            for d, (q_ref, v_ref, b_ref, kk_ref, o_ref) in enumerate(dirs):
                _hgrn_mild_direction(d == 1, q_ref, kk_ref, b_ref, v_ref, o_ref, s_scr.at[d],
                                     lvl_ref[d], ops_s.at[d], a_s.at[d])

        @pl.when(jnp.logical_not(in_range))
        def _():
            row = lax.broadcasted_iota(jnp.int32, (c, 1), 0)
            for d, (q_ref, v_ref, b_ref, kk_ref, o_ref) in enumerate(dirs):
                lvl = lvl_ref[2 + d]
                for h in range(n_heads):
                    sl = slice(h * HGRN_HEAD_DIM, (h + 1) * HGRN_HEAD_DIM)
                    b = b_ref[:, sl]
                    before = pltpu.roll(b, c - 1 if d == 1 else 1, 0)
                    first = row == (c - 1 if d == 1 else 0)
                    forget = jnp.exp2(b - jnp.where(first, 0.0, before))
                    o_h, st_new = _hgrn_head(q_ref[:, sl].astype(F32), kk_ref[:, sl].astype(F32), v_ref[:, sl],
                                             forget, b, s_scr[d, h], lvl, d == 1)
                    o_ref[:, sl] = o_h.astype(o_ref.dtype)
                    s_scr[d, h] = st_new

        return carry

    lax.fori_loop(0, n_sub, chunk, 0)

    @pl.when(t == pl.num_programs(1) - 1)
    def _():
        sout_ref[...] = s_scr[...]


def _hgrn_scan(q, v, logf_fwd, kk_fwd, logf_bwd, kk_bwd, s0, *, rows=512):
    bsz, length, dh = q.shape
    c = HGRN_CHUNK
    rows = min(rows, length)
    n = length // rows
    n_heads = dh // HGRN_HEAD_DIM
    assert c == HGRN_HEAD_DIM and c > HGRN_DIAG_BLOCK
    n_mild = int(np.log2(c // HGRN_DIAG_BLOCK)) + 1
    lvl_fast = _level_table(c, HGRN_DIAG_BLOCK)
    lvl_full = _level_table(c, 1)
    lvl = jnp.asarray(np.stack([lvl_fast, lvl_fast.T, lvl_full, lvl_full.T]))
    fwd = pl.BlockSpec((None, rows, dh), lambda b, t: (b, t, 0))
    bwd = pl.BlockSpec((None, rows, dh), lambda b, t: (b, n - 1 - t, 0))
    s_shape = (2, n_heads, HGRN_HEAD_DIM, HGRN_HEAD_DIM)
    s_spec = pl.BlockSpec((None,) + s_shape, lambda b, t: (b, 0, 0, 0, 0))
    return pl.pallas_call(
        _hgrn_kernel,
        out_shape=[jax.ShapeDtypeStruct(q.shape, BF16), jax.ShapeDtypeStruct(q.shape, BF16),
                   jax.ShapeDtypeStruct((bsz,) + s_shape, F32)],
        grid=(bsz, n),
        in_specs=[fwd, fwd, fwd, fwd, bwd, bwd, bwd, bwd, s_spec, _const_spec((4, c, c))],
        out_specs=[fwd, bwd, s_spec],
        scratch_shapes=[pltpu.VMEM(s_shape, F32), pltpu.SMEM((rows // c,), F32),
                        pltpu.VMEM((2, n_heads, 3 + 2 * n_mild, c, HGRN_HEAD_DIM), BF16),
                        pltpu.VMEM((2, n_heads, c, c), BF16)],
        compiler_params=_cparams(2),
        name="hgrn2_scan",
    )(q, v, logf_fwd, kk_fwd, q, v, logf_bwd, kk_bwd, s0, lvl)


def _pool_tables(tm, row_len):
    t = np.arange(tm)
    same_row = (t[:, None] // row_len) == (t[None, :] // row_len)
    pos = t % row_len
    bands, inv = [], []
    for w in POOL_WINDOWS:
        lo = np.clip(pos - w // 2, 0, row_len)
        hi = np.clip(pos - w // 2 + w, 0, row_len)
        u = pos[None, :]
        bands.append(same_row & (u >= lo[:, None]) & (u < hi[:, None]))
        inv.append(1.0 / (hi - lo).astype(np.float64))
    return np.stack(bands).astype(np.float32), np.stack(inv, axis=1).astype(np.float32)


def _about_kernel(of_ref, ob_ref, sg_ref, a_ref, gate_ref, gn_ref, band_ref, icnt_ref, wp_ref,
                  ps_ref, wo_ref, o_ref, mix_ref):
    sub_rows = band_ref.shape[-1]
    dp = a_ref.shape[-1]
    n_groups = band_ref.shape[0]
    n_sub = of_ref.shape[0] // sub_rows
    for r in range(n_sub):
        rows = slice(r * sub_rows, (r + 1) * sub_rows)
        o = of_ref[rows, :].astype(F32) + ob_ref[rows, :].astype(F32)
        heads = []
        for h in range(o.shape[-1] // HGRN_HEAD_DIM):
            oh = o[:, h * HGRN_HEAD_DIM:(h + 1) * HGRN_HEAD_DIM]
            heads.append(oh * lax.rsqrt(jnp.mean(oh * oh, axis=-1, keepdims=True) + EPS))
        b_mix = jnp.concatenate(heads, axis=-1) * gn_ref[...] * sg_ref[rows, :].astype(F32)

        xab = a_ref[rows, :].astype(BF16)
        xa = xab.astype(F32)
        lane_group = lax.broadcasted_iota(jnp.int32, xa.shape, 1) // (dp // n_groups)
        total = jnp.zeros_like(xa)
        for gi in range(n_groups):
            total = jnp.where(lane_group == gi, _dot(band_ref[gi], xab), total)
        y = total * icnt_ref[...] - xa
        a_mix = _dot(y.astype(BF16), wp_ref[...]) * ps_ref[...]
        mix_ref[rows, 0:dp] = a_mix.astype(BF16)
        mix_ref[rows, dp:] = b_mix.astype(BF16)
    for r in range(n_sub):
        rows = slice(r * sub_rows, (r + 1) * sub_rows)
        o_ref[rows, :] = (gate_ref[...] * _dot(mix_ref[rows, :], wo_ref[...])).astype(o_ref.dtype)


def _ab_out(o_f, o_b, g, xa, mod5, layer, row_of_batch, g_norm, e, w_pool_bd, pool_scale, w_out, row_len,
            *, tm=1024, sub_rows=256):
    bsz, length, dh = o_f.shape
    d = w_out.shape[-1]
    dp = xa.shape[-1]
    tm = min(tm, length)
    sub_rows = min(sub_rows, tm)
    n_groups = len(POOL_WINDOWS)
    bands, inv = _pool_tables(sub_rows, row_len)
    icnt = jnp.asarray(np.repeat(inv, dp // n_groups, axis=1))
    _, _, gate_spec = _mod_block_specs(d, layer, 1, row_of_batch)
    return pl.pallas_call(
        _about_kernel,
        out_shape=jax.ShapeDtypeStruct((bsz, length, d), BF16),
        grid=(bsz, length // tm),
        in_specs=[_tok_spec(tm, dh), _tok_spec(tm, dh), _tok_spec(tm, dh), _tok_spec(tm, dp),
                  gate_spec, _pick_spec(g_norm.shape, (e,)), _const_spec((n_groups, sub_rows, sub_rows)),
                  _const_spec((sub_rows, dp)), _const_spec((dp, dp)), _pick_spec(pool_scale.shape, (e,)),
                  _const_spec(w_out.shape)],
        out_specs=_tok_spec(tm, d),
        scratch_shapes=[pltpu.VMEM((tm, dp + dh), BF16)],
        compiler_params=_cparams(2),
        name="ab_out_proj",
    )(o_f, o_b, g, xa, mod5, g_norm, jnp.asarray(bands, BF16), icnt, w_pool_bd, pool_scale, w_out)


def _gelu_tanh(x):
    k1 = float(-2.0 * np.sqrt(2.0 / np.pi) * np.log2(np.e))
    k2 = 0.044715 * k1
    return x / (1.0 + jnp.exp2(x * (k1 + k2 * (x * x))))


def _gmlp_kernel(x_ref, shift_ref, scale_ref, gate_ref, g_ref, win_ref, lng_ref, lnb_ref, ws_ref, bs_ref,
                 wout_ref, o_ref, u_ref, v_ref, s_ref, *, sub_rows, n_chunk):
    dg = win_ref.shape[1] // 2
    n_sub = x_ref.shape[0] // sub_rows
    means = []
    for r in range(n_sub):
        rows = slice(r * sub_rows, (r + 1) * sub_rows)
        h = _adaln(x_ref[rows, :], g_ref[...], shift_ref[...], scale_ref[...]).astype(BF16)
        total = jnp.zeros((sub_rows, 1), F32)
        for c0 in range(0, dg, n_chunk):
            cols = slice(c0, c0 + n_chunk)
            u_ref[rows, cols] = _gelu_tanh(_dot(h, win_ref[:, c0:c0 + n_chunk]))
            zv = _gelu_tanh(_dot(h, win_ref[:, dg + c0:dg + c0 + n_chunk]))
            v_ref[rows, cols] = zv
            total = total + jnp.sum(zv, axis=-1, keepdims=True)
        means.append(total / dg)
    for r in range(n_sub):
        rows = slice(r * sub_rows, (r + 1) * sub_rows)
        mean = means[r]
        sq = jnp.zeros((sub_rows, 1), F32)
        for c0 in range(0, dg, n_chunk):
            vc = v_ref[rows, c0:c0 + n_chunk] - mean
            sq = sq + jnp.sum(vc * vc, axis=-1, keepdims=True)
        rstd = lax.rsqrt(sq / dg + EPS)
        for c in range(sub_rows // GMLP_CHUNK):
            crows = slice(c * GMLP_CHUNK, (c + 1) * GMLP_CHUNK)
            srows = slice(r * sub_rows + c * GMLP_CHUNK, r * sub_rows + (c + 1) * GMLP_CHUNK)
            for gi in range(dg // GMLP_GROUP_DIM):
                cols = slice(gi * GMLP_GROUP_DIM, (gi + 1) * GMLP_GROUP_DIM)
                vn = (v_ref[srows, cols] - mean[crows]) * rstd[crows] * lng_ref[:, cols] + lnb_ref[:, cols]
                sv = _dot(ws_ref[gi], vn.astype(BF16)) + bs_ref[gi]
                s_ref[srows, cols] = (u_ref[srows, cols] * sv).astype(BF16)
        o_ref[rows, :] = x_ref[rows, :] + gate_ref[...] * _dot(s_ref[rows, :], wout_ref[...])


def _gmlp(x, mod5, layer, row_of_batch, norm_g4, o, w_in, ln_g, ln_b, w_s, b_s_wide, w_out, *, tm=1024,
          sub_rows=512, n_chunk=512):
    bsz, length, d = x.shape
    dg = w_in.shape[-1] // 2
    tm = min(tm, length)
    kern = functools.partial(_gmlp_kernel, sub_rows=min(sub_rows, tm), n_chunk=n_chunk)
    return pl.pallas_call(
        kern,
        out_shape=jax.ShapeDtypeStruct(x.shape, F32),
        grid=(bsz, length // tm),
        in_specs=[_tok_spec(tm, d)]
        + _mod_block_specs(d, layer, 1, row_of_batch)
        + [_pick_spec(norm_g4.shape, (layer, 1)), _const_spec(w_in.shape), _pick_spec(ln_g.shape, (o,)),
           _pick_spec(ln_b.shape, (o,)), _const_spec(w_s.shape), _pick_spec(b_s_wide.shape, (o,)),
           _const_spec(w_out.shape)],
        out_specs=_tok_spec(tm, d),
        scratch_shapes=[pltpu.VMEM((tm, dg), F32), pltpu.VMEM((tm, dg), F32), pltpu.VMEM((tm, dg), BF16)],
        compiler_params=_cparams(2),
        name="gmlp_mixer",
    )(x, mod5, mod5, mod5, norm_g4, w_in, ln_g, ln_b, w_s, b_s_wide, w_out)


def _block_diag(w):
    n, g, a, b = w.shape
    eye = jnp.eye(g, dtype=w.dtype)
    return (w[:, :, :, None, :] * eye[None, :, None, :, None]).reshape(n, g * a, g * b)


def kernel(x, c, ctx, c_ctx, mod_w, mod_b, norm_g, ffn_w1, ffn_w3, ffn_w2, ab_w_in, pool_w, pool_scale,
           hgrn_lb, hgrn_norm_g, ab_w_out, gmlp_w_in, gmlp_ln_g, gmlp_ln_b, gmlp_w_s, gmlp_b_s, gmlp_w_out,
           final_g):
    bsz, _, d = x.shape
    depth = mod_w.shape[0]
    d_pool = pool_scale.shape[-1]
    d_hgrn = hgrn_norm_g.shape[-1]
    n_heads = d_hgrn // HGRN_HEAD_DIM

    c_rows = jnp.concatenate([c, c_ctx[None, :]], axis=0)
    mod = _modulation(c_rows, mod_w, mod_b)
    mod5 = mod.reshape(depth, mod.shape[1], N_MOD, 1, d)
    lat_row = lambda b: b
    ctx_row = lambda b: bsz

    norm_g4 = norm_g[:, :, None, :]
    pool_bd = _block_diag(pool_w).astype(BF16)
    pool_scale3, hgrn_norm_g3 = pool_scale[:, None, :], hgrn_norm_g[:, None, :]
    gm_ln_g, gm_ln_b = gmlp_ln_g[:, None, :], gmlp_ln_b[:, None, :]
    gm_b_wide = jnp.broadcast_to(gmlp_b_s[:, :, :, None], gmlp_b_s.shape + (GMLP_GROUP_DIM,))
    gmlp_w_s2 = gmlp_w_s.reshape(gmlp_w_s.shape[0], -1, gmlp_w_s.shape[-1])

    def ffn_sources(i, j):
        return [(ffn_w1, (i, j)), (ffn_w3, (i, j)), (ffn_w2, (i, j))]

    def mixer_sources(i):
        if i % 2 == 0:
            return [(ab_w_in, (i // 2,)), (ab_w_out, (i // 2,))]
        return [(gmlp_w_in, (i // 2,)), (gmlp_w_s2, (i // 2,)), (gmlp_w_out, (i // 2,))]

    ffn_order = [(i, j) for i in range(depth) for j in range(2)]
    ffn_w = {ffn_order[0]: [arr[lead].astype(BF16) for arr, lead in ffn_sources(*ffn_order[0])]}
    mixer_w = {}

    def latent_ffn(xl, i, j, **kw):
        k = ffn_order.index((i, j))
        cast = ffn_sources(*ffn_order[k + 1]) if k + 1 < len(ffn_order) else []
        n_next = len(cast)
        if j == 0:
            cast = cast + mixer_sources(i)
        xl, done = _ffn(xl, mod5, i, 2 * j, lat_row, norm_g4, *ffn_w[(i, j)], final_g, cast=cast, **kw)
        if n_next:
            ffn_w[ffn_order[k + 1]] = done[:n_next]
        if j == 0:
            mixer_w[i] = done[n_next:]
        return xl

    def context_ffn(xc, i, j, **kw):
        return _ffn(xc, mod5, i, 2 * j, ctx_row, norm_g4, *ffn_w[(i, j)], final_g, **kw)[0]

    xl, xc = x, ctx
    for i in range(depth):
        ctx_live = any(j % 2 == 0 for j in range(i, depth))
        last = i == depth - 1

        xl = latent_ffn(xl, i, 0)
        if ctx_live:
            xc = context_ffn(xc, i, 0)

        if i % 2 == 0:
            e = i // 2
            ab_in_w, ab_out_w = mixer_w[i]
            ab_in = functools.partial(_ab_in, mod5=mod5, layer=i, norm_g4=norm_g4, w_in=ab_in_w,
                                      hgrn_lb=hgrn_lb, slot=e, d_pool=d_pool)
            q_c, v_c, lf_c, kf_c, lb_c, kb_c, sg_c, a_c = ab_in(xc, row_of_batch=ctx_row)
            q_l, v_l, lf_l, kf_l, lb_l, kb_l, sg_l, a_l = ab_in(xl, row_of_batch=lat_row)
            s0 = jnp.zeros((bsz, 2, n_heads, HGRN_HEAD_DIM, HGRN_HEAD_DIM), F32)
            of_c, ob_c, s_ctx = _hgrn_scan(q_c, v_c, lf_c, kf_c, lb_c, kb_c, s0)
            of_l, ob_l, _ = _hgrn_scan(q_l, v_l, lf_l, kf_l, lb_l, kb_l, s_ctx)
            ab_out = functools.partial(_ab_out, mod5=mod5, layer=i, g_norm=hgrn_norm_g3, e=e,
                                       w_pool_bd=pool_bd[e], pool_scale=pool_scale3, w_out=ab_out_w)
            dc = ab_out(of_c, ob_c, sg_c, a_c, row_of_batch=ctx_row, row_len=xc.shape[1])
            dl = ab_out(of_l, ob_l, sg_l, a_l, row_of_batch=lat_row, row_len=GRID_W)
        else:
            dc = dl = None
            o = i // 2
            gm_in, gm_s2, gm_out = mixer_w[i]
            gmlp = functools.partial(_gmlp, mod5=mod5, layer=i, norm_g4=norm_g4, o=o, w_in=gm_in, ln_g=gm_ln_g,
                                     ln_b=gm_ln_b, w_s=gm_s2.reshape(gmlp_w_s.shape[1:]), b_s_wide=gm_b_wide,
                                     w_out=gm_out)
            xl = gmlp(xl, row_of_batch=lat_row)
            if ctx_live:
                xc = gmlp(xc, row_of_batch=ctx_row)

        xl = latent_ffn(xl, i, 1, final_norm=last, delta=dl)
        if ctx_live:
            xc = context_ffn(xc, i, 1, delta=dc)
    return xl
```
